```python
import jax, jax.numpy as jnp
from jax import lax
import numpy as np

D_MODEL = 2048
BATCH = 1
SEQ = 8192
DEPTH = 1
DEC_BATCH = 8
DEC_SEQ = 64
PAST_LEN = 2048

CHUNK = 64
Q_BLOCK = 128
H_A = 16
Q_LORA = 512
KV_LORA = 512
QK_NOPE = 128
QK_ROPE = 64
V_HEAD = D_MODEL // H_A
MLA_SCALE = (QK_NOPE + QK_ROPE) ** -0.5
H_R = 8
DK_R = D_MODEL // H_R
DV_R = D_MODEL // H_R
N_EXPERTS = 32
TOP_K = 4
D_FF = D_MODEL
SWIGLU_LIMIT = 7.0
SWIGLU_ALPHA = 1.702
MOE_BLOCK = 128
ROPE_THETA = 10000.0
NORM_EPS = 1e-6
IN_SIZES = (Q_LORA, KV_LORA, QK_ROPE, H_R * DK_R, H_R * DK_R, H_R * DV_R, D_MODEL, D_MODEL, D_MODEL)
IN_WIDTH = Q_LORA + KV_LORA + QK_ROPE + 2 * H_R * DK_R + H_R * DV_R + 3 * D_MODEL

kernel_name = 'streaming_mla_retention_moe_step'


def rms_norm(x, g):
    xf = x.astype(jnp.float32)
    xf = xf * lax.rsqrt(jnp.mean(xf * xf, axis=-1, keepdims=True) + NORM_EPS)
    return (xf * g.astype(jnp.float32)).astype(x.dtype)


def rope(x, pos):
    half = x.shape[-1] // 2
    inv = jnp.power(ROPE_THETA, -jnp.arange(half, dtype=jnp.float32) / half)
    ang = pos.astype(jnp.float32)[:, None] * inv[None, :]
    ang = ang.reshape((ang.shape[0],) + (1,) * (x.ndim - 3) + (half,))
    cos, sin = jnp.cos(ang), jnp.sin(ang)
    xf = x.astype(jnp.float32)
    x1, x2 = xf[..., :half], xf[..., half:]
    return jnp.concatenate([x1 * cos - x2 * sin, x2 * cos + x1 * sin], axis=-1).astype(x.dtype)


def ret_log_gamma():
    return jnp.log1p(-jnp.exp2(-5.0 - jnp.arange(H_R, dtype=jnp.float32)))


def mixer_inputs(xn, pos, w_in, g_q, w_uq, g_kv):
    B, T, _ = xn.shape
    cuts = [int(c) for c in np.cumsum(IN_SIZES)[:-1]]
    cq, ckv, kr, rq, rk, rv, rg, ga, gb = jnp.split(xn @ w_in, cuts, axis=-1)
    q = jnp.einsum('btc,chd->bthd', rms_norm(cq, g_q), w_uq)
    q_nope = q[..., :QK_NOPE]
    q_rope = rope(q[..., QK_NOPE:], pos)
    c_kv = rms_norm(ckv, g_kv)
    k_rope = rope(kr, pos)
    rq = rope(rq.reshape(B, T, H_R, DK_R), pos)
    rk = rope(rk.reshape(B, T, H_R, DK_R), pos) * (DK_R ** -0.5)
    rv = rv.reshape(B, T, H_R, DV_R)
    return q_nope, q_rope, c_kv, k_rope, rq, rk, rv, rg, ga, gb


def mla_block(q_nope, q_rope, c_kv, k_rope, q_pos, k_pos, w_uk, w_uv):
    q_lat = jnp.einsum('bqhn,chn->bqhc', q_nope, w_uk)
    s = jnp.einsum('bqhc,bkc->bhqk', q_lat, c_kv) + jnp.einsum('bqhr,bkr->bhqk', q_rope, k_rope)
    s = s.astype(jnp.float32) * MLA_SCALE
    visible = (k_pos[None, :] // CHUNK) <= (q_pos[:, None] // CHUNK)
    s = jnp.where(visible[None, None], s, -1e30)
    p = jax.nn.softmax(s, axis=-1).astype(c_kv.dtype)
    o_lat = jnp.einsum('bhqk,bkc->bqhc', p, c_kv)
    return jnp.einsum('bqhc,chv->bqhv', o_lat, w_uv)


def mla_prompt(q_nope, q_rope, c_kv, k_rope, pos, w_uk, w_uv):
    B, T = q_nope.shape[:2]
    nb = T // Q_BLOCK
    blocks = lambda a: a.reshape((B, nb, Q_BLOCK) + a.shape[2:]).swapaxes(0, 1)

    def one_block(args):
        qn, qr, qp = args
        return mla_block(qn, qr, c_kv, k_rope, qp, pos, w_uk, w_uv)

    o = lax.map(one_block, (blocks(q_nope), blocks(q_rope), pos.reshape(nb, Q_BLOCK)))
    return o.swapaxes(0, 1).reshape(B, T, H_A, V_HEAD)


def retention_chunk(state, q, k, v):
    q, k, v = q.astype(jnp.float32), k.astype(jnp.float32), v.astype(jnp.float32)
    L = q.shape[1]
    lg = ret_log_gamma()
    idx = jnp.arange(L, dtype=jnp.float32)
    diff = idx[:, None] - idx[None, :]
    decay = jnp.where(diff >= 0, jnp.exp(jnp.maximum(diff, 0.0)[None] * lg[:, None, None]), 0.0)
    scores = jnp.einsum('blhd,bmhd->bhlm', q, k) * decay[None]
    intra = jnp.einsum('bhlm,bmhe->blhe', scores, v)
    xi = jnp.exp((idx + 1.0)[:, None] * lg[None, :])
    cross = jnp.einsum('blhd,bhde->blhe', q, state) * xi[None, :, :, None]
    zeta = jnp.exp((L - 1.0 - idx)[:, None] * lg[None, :])
    new_state = state * jnp.exp(L * lg)[None, :, None, None] + jnp.einsum('blhd,blhe->bhde', k * zeta[None, :, :, None], v)
    return intra + cross, new_state


def retention_prompt(q, k, v):
    B, T = q.shape[:2]
    nc = T // CHUNK
    chunks = lambda a: a.reshape((B, nc, CHUNK) + a.shape[2:]).swapaxes(0, 1)

    def step(s, qkv):
        o, s = retention_chunk(s, *qkv)
        return s, o

    s0 = jnp.zeros((B, H_R, DK_R, DV_R), jnp.float32)
    s_final, o = lax.scan(step, s0, (chunks(q), chunks(k), chunks(v)))
    return o.swapaxes(0, 1).reshape(B, T, H_R, DV_R), s_final


def mixer_output(o_attn, o_ret, rg, ga, gb, w_o):
    B, T = o_attn.shape[:2]
    y_a = o_attn.reshape(B, T, D_MODEL)
    o_ret = o_ret * lax.rsqrt(jnp.mean(o_ret * o_ret, axis=-1, keepdims=True) + NORM_EPS)
    y_b = jax.nn.silu(rg) * o_ret.reshape(B, T, D_MODEL).astype(rg.dtype)
    merged = jax.nn.sigmoid(ga) * y_a + jax.nn.sigmoid(gb) * y_b
    return merged @ w_o


def moe(x, w_router, b_router, w_gate_up, b_gate_up, w_down, b_down):
    T, D = x.shape
    logits = x.astype(jnp.float32) @ w_router.astype(jnp.float32) + b_router.astype(jnp.float32)
    top_logit, top_idx = lax.top_k(logits, TOP_K)
    top_w = jax.nn.softmax(top_logit, axis=-1)
    A = T * TOP_K
    flat_e = top_idx.reshape(A)
    flat_tok = jnp.repeat(jnp.arange(T, dtype=jnp.int32), TOP_K)
    flat_w = top_w.reshape(A)
    order = jnp.argsort(flat_e)
    e_sorted = flat_e[order]
    counts = jnp.bincount(flat_e, length=N_EXPERTS)
    padded = (counts + MOE_BLOCK - 1) // MOE_BLOCK * MOE_BLOCK
    start = jnp.cumsum(counts) - counts
    pad_end = jnp.cumsum(padded)
    pad_start = pad_end - padded
    dest = pad_start[e_sorted] + jnp.arange(A, dtype=jnp.int32) - start[e_sorted]
    n_blocks = -(-A // MOE_BLOCK) + N_EXPERTS
    slots = n_blocks * MOE_BLOCK
    slot_tok = jnp.full((slots,), T, jnp.int32).at[dest].set(flat_tok[order])
    slot_w = jnp.zeros((slots,), jnp.float32).at[dest].set(flat_w[order])
    block_e = jnp.minimum(jnp.searchsorted(pad_end, jnp.arange(n_blocks, dtype=pad_end.dtype) * MOE_BLOCK, side='right'), N_EXPERTS - 1)
    x_pad = jnp.concatenate([x, jnp.zeros((1, D), x.dtype)], axis=0)
    xb = x_pad[slot_tok].reshape(n_blocks, MOE_BLOCK, D)

    def expert_block(args):
        xe, e = args
        gu = xe @ w_gate_up[e] + b_gate_up[e]
        gate = jnp.minimum(gu[:, :D_FF], SWIGLU_LIMIT)
        up = jnp.clip(gu[:, D_FF:], -SWIGLU_LIMIT, SWIGLU_LIMIT)
        h = (up + 1.0) * (gate * jax.nn.sigmoid(gate * SWIGLU_ALPHA))
        return h @ w_down[e] + b_down[e]

    yb = lax.map(expert_block, (xb, block_e))
    y = jax.ops.segment_sum(yb.reshape(slots, D) * slot_w[:, None].astype(x.dtype), slot_tok, num_segments=T + 1)
    return y[:T]


def moe_3d(x, *w):
    return moe(x.reshape(-1, x.shape[-1]), *w).reshape(x.shape)


def setup_inputs(seed: int = 0) -> dict:
    key = jax.random.key(seed)
    ks = jax.random.split(key, 21)
    nrm = lambda k, shape, scale: jax.random.normal(k, shape, jnp.float32) * scale
    gain = lambda k, shape: 1.0 + 0.01 * jax.random.normal(k, shape, jnp.float32)
    return {
        'x_prompt': nrm(ks[0], (BATCH, SEQ, D_MODEL), 1.0),
        'x_sample': nrm(ks[1], (DEC_BATCH, DEC_SEQ, D_MODEL), 1.0),
        'cache_latent': nrm(ks[2], (DEPTH, DEC_BATCH, PAST_LEN, KV_LORA), 1.0),
        'cache_k_rope': nrm(ks[3], (DEPTH, DEC_BATCH, PAST_LEN, QK_ROPE), 1.0),
        'state_retention': nrm(ks[4], (DEPTH, DEC_BATCH, H_R, DK_R, DV_R), 0.5),
        'g_mix_norm': gain(ks[5], (DEPTH, D_MODEL)),
        'w_in': nrm(ks[6], (DEPTH, D_MODEL, IN_WIDTH), D_MODEL ** -0.5),
        'g_q_norm': gain(ks[7], (DEPTH, Q_LORA)),
        'w_uq': nrm(ks[8], (DEPTH, Q_LORA, H_A, QK_NOPE + QK_ROPE), Q_LORA ** -0.5),
        'g_kv_norm': gain(ks[9], (DEPTH, KV_LORA)),
        'w_uk': nrm(ks[10], (DEPTH, KV_LORA, H_A, QK_NOPE), KV_LORA ** -0.5),
        'w_uv': nrm(ks[11], (DEPTH, KV_LORA, H_A, V_HEAD), KV_LORA ** -0.5),
        'w_o': nrm(ks[12], (DEPTH, D_MODEL, D_MODEL), D_MODEL ** -0.5),
        'g_ffn_norm': gain(ks[13], (DEPTH, D_MODEL)),
        'w_router': nrm(ks[14], (DEPTH, D_MODEL, N_EXPERTS), D_MODEL ** -0.5),
        'b_router': nrm(ks[15], (DEPTH, N_EXPERTS), 0.01),
        'w_gate_up': nrm(ks[16], (DEPTH, N_EXPERTS, D_MODEL, 2 * D_FF), D_MODEL ** -0.5),
        'b_gate_up': nrm(ks[17], (DEPTH, N_EXPERTS, 2 * D_FF), 0.01),
        'w_down': nrm(ks[18], (DEPTH, N_EXPERTS, D_FF, D_MODEL), D_FF ** -0.5),
        'b_down': nrm(ks[19], (DEPTH, N_EXPERTS, D_MODEL), 0.01),
        'g_final': gain(ks[20], (D_MODEL,)),
    }


def reference(x_prompt, x_sample, cache_latent, cache_k_rope, state_retention,
              g_mix_norm, w_in, g_q_norm, w_uq, g_kv_norm, w_uk, w_uv, w_o,
              g_ffn_norm, w_router, b_router, w_gate_up, b_gate_up, w_down, b_down, g_final):
    t_p = x_prompt.shape[1]
    t_s = x_sample.shape[1]
    pos_p = jnp.arange(t_p, dtype=jnp.int32)
    pos_s = PAST_LEN + jnp.arange(t_s, dtype=jnp.int32)
    k_pos_s = jnp.arange(PAST_LEN + t_s, dtype=jnp.int32)
    hp, hs = x_prompt, x_sample
    lat_p, kr_p, st_p, lat_s, kr_s, st_s = [], [], [], [], [], []
    for l in range(DEPTH):
        moe_w = (w_router[l], b_router[l], w_gate_up[l], b_gate_up[l], w_down[l], b_down[l])
        qn, qr, ckv, kr, rq, rk, rv, rg, ga, gb = mixer_inputs(rms_norm(hp, g_mix_norm[l]), pos_p, w_in[l], g_q_norm[l], w_uq[l], g_kv_norm[l])
        o_a = mla_prompt(qn, qr, ckv, kr, pos_p, w_uk[l], w_uv[l])
        o_r, s_new = retention_prompt(rq, rk, rv)
        hp = hp + mixer_output(o_a, o_r, rg, ga, gb, w_o[l])
        hp = hp + moe_3d(rms_norm(hp, g_ffn_norm[l]), *moe_w)
        lat_p.append(ckv)
        kr_p.append(kr)
        st_p.append(s_new)
        qn, qr, ckv, kr, rq, rk, rv, rg, ga, gb = mixer_inputs(rms_norm(hs, g_mix_norm[l]), pos_s, w_in[l], g_q_norm[l], w_uq[l], g_kv_norm[l])
        c_all = jnp.concatenate([cache_latent[l].astype(ckv.dtype), ckv], axis=1)
        kr_all = jnp.concatenate([cache_k_rope[l].astype(kr.dtype), kr], axis=1)
        o_a = mla_block(qn, qr, c_all, kr_all, pos_s, k_pos_s, w_uk[l], w_uv[l])
        o_r, s_new = retention_chunk(state_retention[l].astype(jnp.float32), rq, rk, rv)
        hs = hs + mixer_output(o_a, o_r, rg, ga, gb, w_o[l])
        hs = hs + moe_3d(rms_norm(hs, g_ffn_norm[l]), *moe_w)
        lat_s.append(ckv)
        kr_s.append(kr)
        st_s.append(s_new)
    y_prompt = rms_norm(hp, g_final)
    y_sample = rms_norm(hs, g_final)
    new_latent_prompt = jnp.stack(lat_p)
    new_k_rope_prompt = jnp.stack(kr_p)
    new_state_prompt = jnp.stack(st_p)
    new_latent_sample = jnp.stack(lat_s)
    new_k_rope_sample = jnp.stack(kr_s)
    new_state_sample = jnp.stack(st_s)
    return (y_prompt, y_sample, new_latent_prompt, new_k_rope_prompt, new_state_prompt, new_latent_sample, new_k_rope_sample, new_state_sample)
```

```python
import functools

import jax
import jax.numpy as jnp
from jax import lax
from jax.experimental import pallas as pl
from jax.experimental.pallas import tpu as pltpu

F32 = jnp.float32
BF16 = jnp.bfloat16

NORM_EPS = 1e-6
ROPE_THETA = 10000.0
CHUNK = 64
TOP_K = 4
SWIGLU_LIMIT = 7.0
SWIGLU_ALPHA = 1.702

LANES = 128
VMEM_LIMIT_BYTES = 56 * 1024 * 1024

RET_CHUNK = 256
MOE_ROWS = 256
MOE_GROUP = 5
MOE_FTILE = 256
COMBINE_ROWS = 128


def _params(sem, vmem=None):
    return pltpu.CompilerParams(dimension_semantics=sem, vmem_limit_bytes=vmem)


def _rms(x, g):
    return x * lax.rsqrt(jnp.mean(x * x, axis=-1, keepdims=True) + NORM_EPS) * g


def _sigmoid(x):
    return 1.0 / (1.0 + jnp.exp(-x))


def _norm_cast_kernel(x_ref, g_ref, o_ref):
    o_ref[...] = _rms(x_ref[...], g_ref[...]).astype(o_ref.dtype)


def norm_cast(x, g, tm=512):
    m, d = x.shape
    return pl.pallas_call(
        _norm_cast_kernel,
        out_shape=jax.ShapeDtypeStruct((m, d), BF16),
        grid=(m // tm,),
        in_specs=[pl.BlockSpec((tm, d), lambda i: (i, 0)),
                  pl.BlockSpec((1, d), lambda i: (0, 0))],
        out_specs=pl.BlockSpec((tm, d), lambda i: (i, 0)),
        compiler_params=_params(("parallel",)),
        name="norm_cast",
    )(x, g)


def _rope_hi(x, c, sa, sb):
    return x * c + pltpu.roll(x, 32, 1) * sa + pltpu.roll(x, 96, 1) * sb


def _proj_small_kernel(x_ref, w_ref, gq_ref, gkv_ref, c_ref, sa_ref, sb_ref,
                       cqn_ref, ckv_ref, ckvb_ref, kr_ref, krp_ref, *, ql, kl, rope):
    acc = jnp.dot(x_ref[...], w_ref[...], preferred_element_type=F32)
    cqn_ref[...] = _rms(acc[:, :ql], gq_ref[...]).astype(cqn_ref.dtype)
    ckv = _rms(acc[:, ql:ql + kl], gkv_ref[...])
    ckv_ref[...] = ckv
    ckvb_ref[...] = ckv.astype(ckvb_ref.dtype)
    kr = _rope_hi(acc[:, ql + kl:ql + kl + LANES], c_ref[...], sa_ref[...], sb_ref[...])
    kr_ref[...] = kr[:, :rope]
    krp_ref[...] = kr.astype(krp_ref.dtype)


def proj_small(xn, wa, gq, gkv, c, sa, sb, ql, kl, rope, tm=256):
    m, d = xn.shape
    n = wa.shape[1]
    row = lambda w: pl.BlockSpec((tm, w), lambda i: (i, 0))
    full = lambda a: pl.BlockSpec(a.shape, lambda i: (0, 0))
    return pl.pallas_call(
        functools.partial(_proj_small_kernel, ql=ql, kl=kl, rope=rope),
        out_shape=(jax.ShapeDtypeStruct((m, ql), BF16),
                   jax.ShapeDtypeStruct((m, kl), F32),
                   jax.ShapeDtypeStruct((m, kl), BF16),
                   jax.ShapeDtypeStruct((m, rope), F32),
                   jax.ShapeDtypeStruct((m, LANES), BF16)),
        grid=(m // tm,),
        in_specs=[row(d), full(wa), full(gq), full(gkv), row(LANES), row(LANES), row(LANES)],
        out_specs=(row(ql), row(kl), row(kl), row(rope), row(LANES)),
        compiler_params=_params(("parallel",), VMEM_LIMIT_BYTES),
        name="proj_small",
    )(xn, wa, gq, gkv, c, sa, sb)


def _proj_big_kernel(x_ref, w_ref, cos_ref, sin_ref, o_ref, *, n_rope, dk, k_scale):
    j = pl.program_id(0)
    acc = jnp.dot(x_ref[...], w_ref[...], preferred_element_type=F32)

    @pl.when(j >= n_rope)
    def _():
        o_ref[...] = acc.astype(o_ref.dtype)

    @pl.when(j < n_rope)
    def _():
        scale = jnp.where(j >= n_rope // 2, k_scale, 1.0).astype(F32)
        c = cos_ref[...] * scale
        s = sin_ref[...] * scale
        half = dk // 2
        for h in range(acc.shape[1] // dk):
            x1 = acc[:, h * dk:h * dk + half]
            x2 = acc[:, h * dk + half:(h + 1) * dk]
            o_ref[:, h * dk:h * dk + half] = (x1 * c - x2 * s).astype(o_ref.dtype)
            o_ref[:, h * dk + half:(h + 1) * dk] = (x2 * c + x1 * s).astype(o_ref.dtype)


def proj_big(xn, wb, cos_r, sin_r, rope_cols, dk, k_scale, tm=512, tn=1024):
    m, d = xn.shape
    n = wb.shape[1]
    return pl.pallas_call(
        functools.partial(_proj_big_kernel, n_rope=rope_cols // tn, dk=dk, k_scale=k_scale),
        out_shape=jax.ShapeDtypeStruct((m, n), BF16),
        grid=(n // tn, m // tm),
        in_specs=[pl.BlockSpec((tm, d), lambda j, i: (i, 0)),
                  pl.BlockSpec((d, tn), lambda j, i: (0, j)),
                  pl.BlockSpec((tm, dk // 2), lambda j, i: (i, 0)),
                  pl.BlockSpec((tm, dk // 2), lambda j, i: (i, 0))],
        out_specs=pl.BlockSpec((tm, tn), lambda j, i: (i, j)),
        compiler_params=_params(("parallel", "parallel"), VMEM_LIMIT_BYTES),
        name="proj_big",
    )(xn, wb, cos_r, sin_r)


def _q_proj_kernel(x_ref, w_ref, c_ref, sa_ref, sb_ref, o_ref, *, heads, scale):
    acc = jnp.dot(x_ref[...], w_ref[...], preferred_element_type=F32)
    c = c_ref[...] * scale
    sa = sa_ref[...] * scale
    sb = sb_ref[...] * scale
    for h in range(heads):
        lo = 2 * h * LANES
        o_ref[:, lo:lo + LANES] = (acc[:, lo:lo + LANES] * scale).astype(o_ref.dtype)
        o_ref[:, lo + LANES:lo + 2 * LANES] = _rope_hi(
            acc[:, lo + LANES:lo + 2 * LANES], c, sa, sb).astype(o_ref.dtype)


def q_proj(cqn, wq, c, sa, sb, heads, scale, tm=256):
    m, k = cqn.shape
    n = wq.shape[1]
    row = lambda w: pl.BlockSpec((tm, w), lambda i: (i, 0))
    return pl.pallas_call(
        functools.partial(_q_proj_kernel, heads=heads, scale=scale),
        out_shape=jax.ShapeDtypeStruct((m, n), BF16),
        grid=(m // tm,),
        in_specs=[row(k), pl.BlockSpec((k, n), lambda i: (0, 0)), row(LANES), row(LANES), row(LANES)],
        out_specs=row(n),
        compiler_params=_params(("parallel",), VMEM_LIMIT_BYTES),
        name="q_proj",
    )(cqn, wq, c, sa, sb)


def _kv_expand_kernel(c_ref, kr_ref, wk_ref, wv_ref, k_ref, v_ref, *, heads):
    c = c_ref[...]
    kn = jnp.dot(c, wk_ref[...], preferred_element_type=F32)
    v_ref[...] = jnp.dot(c, wv_ref[...], preferred_element_type=F32).astype(v_ref.dtype)
    kr = kr_ref[...]
    for h in range(heads):
        k_ref[:, 2 * h * LANES:(2 * h + 1) * LANES] = kn[:, h * LANES:(h + 1) * LANES].astype(k_ref.dtype)
        k_ref[:, (2 * h + 1) * LANES:(2 * h + 2) * LANES] = kr


def kv_expand(c_bf, kr_pad, wk, wv, rows, heads, tm=256):
    kl = c_bf.shape[1]
    row = lambda w: pl.BlockSpec((tm, w), lambda i: (i, 0))
    full = lambda a: pl.BlockSpec(a.shape, lambda i: (0, 0))
    return pl.pallas_call(
        functools.partial(_kv_expand_kernel, heads=heads),
        out_shape=(jax.ShapeDtypeStruct((rows, 2 * heads * LANES), BF16),
                   jax.ShapeDtypeStruct((rows, heads * LANES), BF16)),
        grid=(rows // tm,),
        in_specs=[row(kl), row(LANES), full(wk), full(wv)],
        out_specs=(row(2 * heads * LANES), row(heads * LANES)),
        compiler_params=_params(("parallel",), VMEM_LIMIT_BYTES),
        name="kv_expand",
    )(c_bf, kr_pad, wk, wv)


def _attn_kernel(q_ref, k_ref, v_ref, o_ref, *, tq, tk, n_kv, causal):
    i = pl.program_id(2)
    q = q_ref[...]

    def step(j, carry, masked):
        m, l, acc = carry
        off = pl.multiple_of(j * tk, tk)
        k = k_ref[pl.ds(off, tk), :]
        v = v_ref[pl.ds(off, tk), :]
        s = lax.dot_general(q, k, (((1,), (1,)), ((), ())), preferred_element_type=F32)
        if masked:
            shift = CHUNK.bit_length() - 1
            qc = lax.shift_right_logical(i * tq + lax.broadcasted_iota(jnp.int32, s.shape, 0), shift)
            kc = lax.shift_right_logical(j * tk + lax.broadcasted_iota(jnp.int32, s.shape, 1), shift)
            s = jnp.where(kc <= qc, s, -1e30)
        m_new = jnp.maximum(m, jnp.max(s, axis=-1, keepdims=True))
        alpha = jnp.exp(m - m_new)
        p = jnp.exp(s - m_new)
        l = alpha * l + jnp.sum(p, axis=-1, keepdims=True)
        acc = alpha * acc + jnp.dot(p.astype(BF16), v, preferred_element_type=F32)
        return m_new, l, acc

    carry = (jnp.full((tq, 1), -jnp.inf, F32), jnp.zeros((tq, 1), F32),
             jnp.zeros((tq, v_ref.shape[1]), F32))
    if causal:
        carry = lax.fori_loop(0, i, functools.partial(step, masked=False), carry)
        carry = step(i, carry, True)
    else:
        carry = lax.fori_loop(0, n_kv, functools.partial(step, masked=False), carry)
    _, l, acc = carry
    o_ref[...] = (acc / l).astype(o_ref.dtype)


def attention(q, k, v, heads, *, batch, tq_total, q_row0, tk_total, tq, tk, causal):
    assert CHUNK & (CHUNK - 1) == 0 and tq % CHUNK == 0
    if causal:
        assert tq == tk
    nq = tq_total // tq
    qb0 = q_row0 // tq
    dq = 2 * LANES
    return pl.pallas_call(
        functools.partial(_attn_kernel, tq=tq, tk=tk, n_kv=tk_total // tk, causal=causal),
        out_shape=jax.ShapeDtypeStruct((batch * tq_total, heads * LANES), BF16),
        grid=(batch, heads, nq),
        in_specs=[pl.BlockSpec((tq, dq), lambda b, h, i: (qb0 + b * nq + i, h)),
                  pl.BlockSpec((tk_total, dq), lambda b, h, i: (b, h)),
                  pl.BlockSpec((tk_total, LANES), lambda b, h, i: (b, h))],
        out_specs=pl.BlockSpec((tq, LANES), lambda b, h, i: (b * nq + i, h)),
        compiler_params=_params(("parallel", "parallel", "arbitrary"), VMEM_LIMIT_BYTES),
        name="attention",
    )(q, k, v)


def _retention_kernel(q_ref, k_ref, v_ref, s0_ref, d_ref, xi_ref, zeta_ref, gc_ref,
                      o_ref, sout_ref, s_scr, *, n_chunks):
    c = pl.program_id(2)

    @pl.when(c == 0)
    def _():
        s_scr[...] = s0_ref[...]

    q = q_ref[...]
    k = k_ref[...]
    v = v_ref[...]
    s = s_scr[...]
    a = lax.dot_general(q, k, (((1,), (1,)), ((), ())), preferred_element_type=F32) * d_ref[...]
    o = jnp.dot(a.astype(BF16), v, preferred_element_type=F32)
    o = o + jnp.dot(q, s.astype(BF16), preferred_element_type=F32) * xi_ref[...]
    kz = (k.astype(F32) * zeta_ref[...]).astype(BF16)
    s_new = s * gc_ref[...] + lax.dot_general(kz, v, (((0,), (0,)), ((), ())),
                                              preferred_element_type=F32)
    s_scr[...] = s_new
    o_ref[...] = (o * lax.rsqrt(jnp.mean(o * o, axis=-1, keepdims=True) + NORM_EPS)).astype(o_ref.dtype)

    @pl.when(c == n_chunks - 1)
    def _():
        sout_ref[...] = s_new


def retention(big, s0, tabs, heads, dk, *, batch, t_total, row0, chunk):
    d_tab, xi_tab, zeta_tab, gc_tab = tabs
    nc = t_total // chunk
    rb0 = row0 // chunk
    rowblk = lambda col0: pl.BlockSpec((chunk, dk), lambda b, h, c: (rb0 + b * nc + c, col0 + h))
    per_head = lambda a: pl.BlockSpec((None,) + a.shape[1:], lambda b, h, c: (h, 0, 0))
    st_spec = pl.BlockSpec((None, None, dk, dk), lambda b, h, c: (b, h, 0, 0))
    return pl.pallas_call(
        functools.partial(_retention_kernel, n_chunks=nc),
        out_shape=(jax.ShapeDtypeStruct((batch * t_total, heads * dk), BF16),
                   jax.ShapeDtypeStruct((batch, heads, dk, dk), F32)),
        grid=(batch, heads, nc),
        in_specs=[rowblk(0), rowblk(heads), rowblk(2 * heads), st_spec,
                  per_head(d_tab), per_head(xi_tab), per_head(zeta_tab), per_head(gc_tab)],
        out_specs=(pl.BlockSpec((chunk, dk), lambda b, h, c: (b * nc + c, h)), st_spec),
        scratch_shapes=[pltpu.VMEM((dk, dk), F32)],
        compiler_params=_params(("parallel", "parallel", "arbitrary"), VMEM_LIMIT_BYTES),
        name="retention",
    )(big, big, big, s0, d_tab, xi_tab, zeta_tab, gc_tab)


def retention_tables(length, heads, dk):
    lg = jnp.log1p(-jnp.exp2(-5.0 - jnp.arange(heads, dtype=F32)))
    idx = jnp.arange(length, dtype=F32)
    diff = idx[:, None] - idx[None, :]
    decay = jnp.where(diff >= 0, jnp.exp(jnp.maximum(diff, 0.0)[None] * lg[:, None, None]), 0.0)
    xi = jnp.exp((idx + 1.0)[None, :] * lg[:, None])
    zeta = jnp.exp((length - 1.0 - idx)[None, :] * lg[:, None])
    gc = jnp.exp(length * lg)
    bc = lambda a: jnp.broadcast_to(a[:, :, None], (heads, a.shape[1], dk))
    return decay, bc(xi), bc(zeta), jnp.broadcast_to(gc[:, None, None], (heads, 1, dk))


def _mixer_out_kernel(oap_ref, oas_ref, orp_ref, ors_ref, rg_ref, ga_ref, gb_ref, x_ref, wo_ref, g_ref,
                      wr_ref, br_ref, h_ref, hn_ref, lg_ref, *, prompt_blocks):
    is_prompt = pl.program_id(0) < prompt_blocks
    oa = jnp.where(is_prompt, oap_ref[...], oas_ref[...]).astype(F32)
    orn = jnp.where(is_prompt, orp_ref[...], ors_ref[...]).astype(F32)
    rg = rg_ref[...].astype(F32)
    yb = rg * _sigmoid(rg) * orn
    merged = _sigmoid(ga_ref[...].astype(F32)) * oa + _sigmoid(gb_ref[...].astype(F32)) * yb
    h = x_ref[...] + jnp.dot(merged.astype(BF16), wo_ref[...], preferred_element_type=F32)
    h_ref[...] = h
    hn = _rms(h, g_ref[...])
    hn_ref[...] = hn
    lg_ref[...] = jnp.dot(hn, wr_ref[...], preferred_element_type=F32,
                          precision=lax.Precision.HIGHEST) + br_ref[...]


def mixer_out(oa_p, oa_s, or_p, or_s, big, gate_blk0, x, wo, g, wr, br, tm=256):
    m, d = x.shape
    e = wr.shape[1]
    npb = oa_p.shape[0] // tm
    nsb = oa_s.shape[0] // tm
    row = lambda col: pl.BlockSpec((tm, d), lambda i: (i, col))
    prompt = pl.BlockSpec((tm, d), lambda i: (jnp.minimum(i, npb - 1), 0))
    sample = pl.BlockSpec((tm, d), lambda i: (jnp.clip(i - npb, 0, nsb - 1), 0))
    full = lambda a: pl.BlockSpec(a.shape, lambda i: (0, 0))
    return pl.pallas_call(
        functools.partial(_mixer_out_kernel, prompt_blocks=npb),
        out_shape=(jax.ShapeDtypeStruct((m, d), F32),
                   jax.ShapeDtypeStruct((m, d), F32),
                   jax.ShapeDtypeStruct((m, e), F32)),
        grid=(m // tm,),
        in_specs=[prompt, sample, prompt, sample, row(gate_blk0), row(gate_blk0 + 1), row(gate_blk0 + 2),
                  row(0), full(wo), full(g), full(wr), full(br)],
        out_specs=(row(0), row(0), pl.BlockSpec((tm, e), lambda i: (i, 0))),
        compiler_params=_params(("parallel",), VMEM_LIMIT_BYTES),
        name="mixer_out",
    )(oa_p, oa_s, or_p, or_s, big, big, big, x, wo, g, wr, br)


def _gather_rows_kernel(tok_ref, src_ref, dst_ref, sems, *, rows, n_steps):
    i = pl.program_id(0)

    def copy(slot, tok, par):
        return pltpu.make_async_copy(src_ref.at[pl.ds(tok, 1), :], dst_ref.at[pl.ds(slot, 1), :],
                                     sems.at[par])

    def issue(r, _):
        slot = i * rows + r
        copy(slot, tok_ref[slot], i % 2).start()
        return 0

    lax.fori_loop(0, rows, issue, 0, unroll=8)

    def drain(par):
        def body(r, _):
            copy(0, 0, par).wait()
            return 0
        lax.fori_loop(0, rows, body, 0, unroll=8)

    @pl.when(i > 0)
    def _():
        drain((i + 1) % 2)

    @pl.when(i == n_steps - 1)
    def _():
        drain(i % 2)


def gather_rows(slot_tok, src, rows=MOE_ROWS):
    n = slot_tok.shape[0]
    n_steps = n // rows
    return pl.pallas_call(
        functools.partial(_gather_rows_kernel, rows=rows, n_steps=n_steps),
        out_shape=jax.ShapeDtypeStruct((n, src.shape[1]), src.dtype),
        grid_spec=pltpu.PrefetchScalarGridSpec(
            num_scalar_prefetch=1, grid=(n_steps,),
            in_specs=[pl.BlockSpec(memory_space=pl.ANY)],
            out_specs=pl.BlockSpec(memory_space=pl.ANY),
            scratch_shapes=[pltpu.SemaphoreType.DMA((2,))]),
        compiler_params=_params(("arbitrary",)),
        name="moe_gather",
    )(slot_tok, src)


def _expert_kernel(ie_ref, ib_ref, inb_ref, nv_ref, xs_ref, wg_ref, wu_ref, bg_ref, bu_ref, wd_ref, bd_ref,
                   ye_ref, x32, xb, y32, wgb, wub, wdb, sem_in, sem_out, *, rows, n_f):
    k = pl.program_id(0)
    f = pl.program_id(1)
    nb = inb_ref[k]
    blk0 = ib_ref[k]

    def in_copy(b):
        return pltpu.make_async_copy(xs_ref.at[pl.ds((blk0 + b) * rows, rows), :],
                                     x32.at[pl.ds(b * rows, rows), :], sem_in)

    def out_copy(b):
        return pltpu.make_async_copy(y32.at[pl.ds(b * rows, rows), :],
                                     ye_ref.at[pl.ds((blk0 + b) * rows, rows), :], sem_out)

    def for_blocks(fn):
        def body(b, _):
            fn(b)
            return 0
        lax.fori_loop(0, nb, body, 0)

    @pl.when(f == 0)
    def _():
        for_blocks(lambda b: in_copy(b).start())
        for_blocks(lambda b: in_copy(b).wait())

        def cast(b):
            r = pl.ds(pl.multiple_of(b * rows, rows), rows)
            xb[r, :] = x32[r, :].astype(BF16)
            y32[r, :] = jnp.broadcast_to(bd_ref[...], (rows, y32.shape[1]))
        for_blocks(cast)

    @pl.when(nb > 0)
    def _():
        wgb[...] = wg_ref[...].astype(BF16)
        wub[...] = wu_ref[...].astype(BF16)
        wdb[...] = wd_ref[...].astype(BF16)

    def compute(b):
        r = pl.ds(pl.multiple_of(b * rows, rows), rows)
        x = xb[r, :]
        gate = jnp.minimum(jnp.dot(x, wgb[...], preferred_element_type=F32) + bg_ref[...], SWIGLU_LIMIT)
        up = jnp.clip(jnp.dot(x, wub[...], preferred_element_type=F32) + bu_ref[...],
                      -SWIGLU_LIMIT, SWIGLU_LIMIT)
        h = (up + 1.0) * (gate * _sigmoid(gate * SWIGLU_ALPHA))
        y32[r, :] += jnp.dot(h.astype(BF16), wdb[...], preferred_element_type=F32)
    for_blocks(compute)

    @pl.when(f == n_f - 1)
    def _():
        for_blocks(lambda b: out_copy(b).start())
        for_blocks(lambda b: out_copy(b).wait())


def expert_mlp(items, xs, w_gate_up, b_gate_up, w_down, b_down, n_items):
    item_e, item_b0, item_nb, n_valid = items
    n_exp, d, two_f = w_gate_up.shape
    d_ff = two_f // 2
    tf = MOE_FTILE
    n_f = d_ff // tf
    rows = MOE_ROWS
    rs = rows * MOE_GROUP

    def fi(k, f, nv):
        return jnp.where(k < nv[0], f, n_f - 1)

    in_specs = [
        pl.BlockSpec(memory_space=pl.ANY),
        pl.BlockSpec((None, d, tf), lambda k, f, ie, ib, inb, nv: (ie[k], 0, fi(k, f, nv))),
        pl.BlockSpec((None, d, tf), lambda k, f, ie, ib, inb, nv: (ie[k], 0, n_f + fi(k, f, nv))),
        pl.BlockSpec((None, 1, tf), lambda k, f, ie, ib, inb, nv: (ie[k], 0, fi(k, f, nv))),
        pl.BlockSpec((None, 1, tf), lambda k, f, ie, ib, inb, nv: (ie[k], 0, n_f + fi(k, f, nv))),
        pl.BlockSpec((None, tf, d), lambda k, f, ie, ib, inb, nv: (ie[k], fi(k, f, nv), 0)),
        pl.BlockSpec((None, 1, d), lambda k, f, ie, ib, inb, nv: (ie[k], 0, 0)),
    ]
    return pl.pallas_call(
        functools.partial(_expert_kernel, rows=rows, n_f=n_f),
        out_shape=jax.ShapeDtypeStruct(xs.shape, F32),
        grid_spec=pltpu.PrefetchScalarGridSpec(
            num_scalar_prefetch=4, grid=(n_items, n_f),
            in_specs=in_specs,
            out_specs=pl.BlockSpec(memory_space=pl.ANY),
            scratch_shapes=[pltpu.VMEM((rs, d), F32), pltpu.VMEM((rs, d), BF16), pltpu.VMEM((rs, d), F32),
                            pltpu.VMEM((d, tf), BF16), pltpu.VMEM((d, tf), BF16), pltpu.VMEM((tf, d), BF16),
                            pltpu.SemaphoreType.DMA, pltpu.SemaphoreType.DMA]),
        input_output_aliases={4: 0},
        compiler_params=_params(("arbitrary", "arbitrary"), VMEM_LIMIT_BYTES),
        name="moe_experts",
    )(item_e, item_b0, item_nb, n_valid, xs, w_gate_up, w_gate_up,
      b_gate_up.reshape(n_exp, 1, two_f), b_gate_up.reshape(n_exp, 1, two_f),
      w_down, b_down.reshape(n_exp, 1, d))


def _combine_kernel(pos_ref, ye_ref, w_ref, h_ref, g_ref, o_ref, buf, sems, *, tc, blk0, n_steps, final_norm):
    i = pl.program_id(0)
    n_rows = TOP_K * tc

    def copy(r, src_row, par):
        return pltpu.make_async_copy(ye_ref.at[pl.ds(src_row, 1), :], buf.at[par, pl.ds(r, 1), :],
                                     sems.at[par])

    def issue(step, par):
        def body(r, _):
            copy(r, pos_ref[(blk0 + step) * n_rows + r], par).start()
            return 0
        lax.fori_loop(0, n_rows, body, 0, unroll=8)

    @pl.when(i == 0)
    def _():
        issue(0, 0)

    @pl.when(i + 1 < n_steps)
    def _():
        issue(i + 1, (i + 1) % 2)

    par = i % 2

    def drain(r, _):
        copy(0, 0, par).wait()
        return 0
    lax.fori_loop(0, n_rows, drain, 0, unroll=8)

    w = w_ref[...]
    y = h_ref[...]
    for k in range(TOP_K):
        y = y + w[:, k:k + 1] * buf[par, k * tc:(k + 1) * tc, :]
    o_ref[...] = _rms(y, g_ref[...]) if final_norm else y


def combine(pos, ye, top_w, h, g, row0, rows, final_norm, tc=COMBINE_ROWS):
    d = h.shape[1]
    blk0 = row0 // tc
    n_steps = rows // tc
    return pl.pallas_call(
        functools.partial(_combine_kernel, tc=tc, blk0=blk0, n_steps=n_steps, final_norm=final_norm),
        out_shape=jax.ShapeDtypeStruct((rows, d), F32),
        grid_spec=pltpu.PrefetchScalarGridSpec(
            num_scalar_prefetch=1, grid=(n_steps,),
            in_specs=[pl.BlockSpec(memory_space=pl.ANY),
                      pl.BlockSpec((tc, TOP_K), lambda i, p: (blk0 + i, 0)),
                      pl.BlockSpec((tc, d), lambda i, p: (blk0 + i, 0)),
                      pl.BlockSpec((1, d), lambda i, p: (0, 0))],
            out_specs=pl.BlockSpec((tc, d), lambda i, p: (i, 0)),
            scratch_shapes=[pltpu.VMEM((2, TOP_K * tc, d), F32), pltpu.SemaphoreType.DMA((2,))]),
        compiler_params=_params(("arbitrary",), VMEM_LIMIT_BYTES),
        name="moe_combine",
    )(pos, ye, top_w, h, g)


def route(logits, tc):
    m, n_exp = logits.shape
    a = m * TOP_K
    rows, group = MOE_ROWS, MOE_GROUP
    top_logit, top_idx = lax.top_k(logits, TOP_K)
    top_w = jax.nn.softmax(top_logit, axis=-1)
    flat_e = top_idx.reshape(a)
    onehot = (flat_e[:, None] == jnp.arange(n_exp, dtype=jnp.int32)[None, :]).astype(jnp.int32)
    csum = jnp.cumsum(onehot, axis=0)
    counts = csum[-1]
    rank = jnp.take_along_axis(csum, flat_e[:, None], axis=1)[:, 0] - 1
    nb_e = (counts + rows - 1) // rows
    blk_start = jnp.cumsum(nb_e) - nb_e
    dest = blk_start[flat_e] * rows + rank
    n_blocks = a // rows + n_exp
    slot_tok = jnp.zeros((n_blocks * rows,), jnp.int32).at[dest].set(
        jnp.arange(a, dtype=jnp.int32) // TOP_K)
    ni_e = (nb_e + group - 1) // group
    item_end = jnp.cumsum(ni_e)
    n_valid = item_end[-1]
    n_items = (n_blocks + n_exp * (group - 1)) // group
    kk = jnp.arange(n_items, dtype=jnp.int32)
    item_e = jnp.minimum(jnp.searchsorted(item_end, kk, side="right"), n_exp - 1).astype(jnp.int32)
    local = kk - (item_end - ni_e)[item_e]
    valid = kk < n_valid
    item_e = jnp.where(valid, item_e, item_e[jnp.maximum(n_valid - 1, 0)])
    item_b0 = jnp.where(valid, blk_start[item_e] + local * group, 0).astype(jnp.int32)
    item_nb = jnp.where(valid, jnp.clip(nb_e[item_e] - local * group, 0, group), 0).astype(jnp.int32)
    pos = dest.reshape(m // tc, tc, TOP_K).transpose(0, 2, 1).reshape(a)
    items = (item_e, item_b0, item_nb, n_valid.reshape(1).astype(jnp.int32))
    return top_w, slot_tok, pos.astype(jnp.int32), items, n_items


def _rope_tables(pos, half):
    inv = jnp.power(ROPE_THETA, -jnp.arange(half, dtype=F32) / half)
    ang = pos.astype(F32)[:, None] * inv[None, :]
    return jnp.cos(ang), jnp.sin(ang)


def kernel(x_prompt, x_sample, cache_latent, cache_k_rope, state_retention, g_mix_norm, w_in, g_q_norm, w_uq, g_kv_norm, w_uk, w_uv, w_o, g_ffn_norm, w_router, b_router, w_gate_up, b_gate_up, w_down, b_down, g_final):
    bp, t, d = x_prompt.shape
    db, ts, _ = x_sample.shape
    depth = w_in.shape[0]
    past = cache_latent.shape[2]
    ql, ha, qk = w_uq.shape[1:]
    kl = w_uk.shape[1]
    rope = cache_k_rope.shape[-1]
    nope = qk - rope
    hr, dk = state_retention.shape[2:4]
    assert bp == 1 and nope == LANES and w_uv.shape[-1] == LANES and 2 * rope == LANES
    mp, ms = bp * t, db * ts
    m = mp + ms
    tk_s = past + ts
    mla_scale = float(qk) ** -0.5

    pos = jnp.concatenate([jnp.arange(t, dtype=jnp.int32),
                           jnp.tile(past + jnp.arange(ts, dtype=jnp.int32), db)])
    cos_a, sin_a = _rope_tables(pos, rope // 2)
    z = lambda w: jnp.zeros((m, w), F32)
    rope_c = jnp.concatenate([cos_a, cos_a, z(LANES - rope)], axis=1)
    rope_sa = jnp.concatenate([z(rope // 2), sin_a, z(LANES - rope)], axis=1)
    rope_sb = jnp.concatenate([-sin_a, z(LANES - rope // 2)], axis=1)
    cos_r, sin_r = _rope_tables(pos, dk // 2)
    tabs_p = retention_tables(RET_CHUNK, hr, dk)
    tabs_s = retention_tables(ts, hr, dk)

    x = jnp.concatenate([x_prompt.reshape(mp, d), x_sample.reshape(ms, d)], axis=0)
    lat_p, kr_p, st_p, lat_s, kr_s, st_s = [], [], [], [], [], []
    small = ql + kl + rope
    for l in range(depth):
        wa = jnp.pad(w_in[l][:, :small], ((0, 0), (0, LANES - rope))).astype(BF16)
        wb = w_in[l][:, small:].astype(BF16)
        wq = jnp.pad(w_uq[l], ((0, 0), (0, 0), (0, 2 * LANES - qk))).reshape(ql, ha * 2 * LANES).astype(BF16)
        wk = w_uk[l].reshape(kl, ha * nope).astype(BF16)
        wv = w_uv[l].reshape(kl, ha * LANES).astype(BF16)
        wo = w_o[l].astype(BF16)

        xn = norm_cast(x, g_mix_norm[l].reshape(1, d))
        cqn, ckv, ckv_b, kr, kr_pad = proj_small(xn, wa, g_q_norm[l].reshape(1, ql), g_kv_norm[l].reshape(1, kl),
                                                 rope_c, rope_sa, rope_sb, ql, kl, rope)
        big = proj_big(xn, wb, cos_r, sin_r, 2 * hr * dk, dk, float(dk) ** -0.5)
        qcat = q_proj(cqn, wq, rope_c, rope_sa, rope_sb, ha, mla_scale)

        k_p, v_p = kv_expand(ckv_b, kr_pad, wk, wv, mp, ha)
        oa_p = attention(qcat, k_p, v_p, ha, batch=1, tq_total=mp, q_row0=0, tk_total=mp,
                         tq=512, tk=512, causal=True)
        c_all = jnp.concatenate([cache_latent[l], ckv[mp:].reshape(db, ts, kl)], axis=1)
        kr_all = jnp.concatenate([cache_k_rope[l], kr[mp:].reshape(db, ts, rope)], axis=1)
        kr_all = jnp.pad(kr_all, ((0, 0), (0, 0), (0, LANES - rope)))
        k_s, v_s = kv_expand(c_all.reshape(db * tk_s, kl).astype(BF16),
                             kr_all.reshape(db * tk_s, LANES).astype(BF16), wk, wv, db * tk_s, ha)
        oa_s = attention(qcat, k_s, v_s, ha, batch=db, tq_total=ts, q_row0=mp, tk_total=tk_s,
                         tq=ts, tk=tk_s, causal=False)

        or_p, s_p = retention(big, jnp.zeros((bp, hr, dk, dk), F32), tabs_p, hr, dk,
                              batch=bp, t_total=t, row0=0, chunk=RET_CHUNK)
        or_s, s_s = retention(big, state_retention[l].astype(F32), tabs_s, hr, dk,
                              batch=db, t_total=ts, row0=mp, chunk=ts)

        h1, hn, logits = mixer_out(oa_p, oa_s, or_p, or_s, big, 3, x, wo, g_ffn_norm[l].reshape(1, d),
                                   w_router[l], b_router[l].reshape(1, -1))

        top_w, slot_tok, pos_flat, items, n_items = route(logits, COMBINE_ROWS)
        xs = gather_rows(slot_tok, hn)
        ye = expert_mlp(items, xs, w_gate_up[l], b_gate_up[l], w_down[l], b_down[l], n_items)
        last = l == depth - 1
        g_out = g_final.reshape(1, d)
        y_p = combine(pos_flat, ye, top_w, h1, g_out, 0, mp, last)
        y_s = combine(pos_flat, ye, top_w, h1, g_out, mp, ms, last)
        x = jnp.concatenate([y_p, y_s], axis=0) if not last else None

        lat_p.append(ckv[:mp].reshape(bp, t, kl))
        kr_p.append(kr[:mp].reshape(bp, t, rope))
        st_p.append(s_p)
        lat_s.append(ckv[mp:].reshape(db, ts, kl))
        kr_s.append(kr[mp:].reshape(db, ts, rope))
        st_s.append(s_s)

    return (y_p.reshape(bp, t, d), y_s.reshape(db, ts, d),
            jnp.stack(lat_p), jnp.stack(kr_p), jnp.stack(st_p),
            jnp.stack(lat_s), jnp.stack(kr_s), jnp.stack(st_s))
```

```python
import functools

import jax
import jax.numpy as jnp
from jax import lax
from jax.experimental import pallas as pl
from jax.experimental.pallas import tpu as pltpu

F32 = jnp.float32
BF16 = jnp.bfloat16

NORM_EPS = 1e-6
ROPE_THETA = 10000.0
CHUNK = 64
TOP_K = 4
SWIGLU_LIMIT = 7.0
SWIGLU_ALPHA = 1.702

LANES = 128
VMEM_LIMIT_BYTES = 56 * 1024 * 1024

RET_CHUNK = 256
MOE_ROWS = 256
MOE_GROUP = 5
MOE_FTILE = 256
COMBINE_ROWS = 128


def _params(sem, vmem=None):
    return pltpu.CompilerParams(dimension_semantics=sem, vmem_limit_bytes=vmem)


def _rms(x, g):
    return x * lax.rsqrt(jnp.mean(x * x, axis=-1, keepdims=True) + NORM_EPS) * g


def _sigmoid(x):
    return 1.0 / (1.0 + jnp.exp(-x))


def _norm_cast_kernel(x_ref, g_ref, o_ref):
    o_ref[...] = _rms(x_ref[...], g_ref[...]).astype(o_ref.dtype)


def norm_cast(x, g, tm=512):
    m, d = x.shape
    return pl.pallas_call(
        _norm_cast_kernel,
        out_shape=jax.ShapeDtypeStruct((m, d), BF16),
        grid=(m // tm,),
        in_specs=[pl.BlockSpec((tm, d), lambda i: (i, 0)),
                  pl.BlockSpec((1, d), lambda i: (0, 0))],
        out_specs=pl.BlockSpec((tm, d), lambda i: (i, 0)),
        compiler_params=_params(("parallel",)),
        name="norm_cast",
    )(x, g)


def _rope_hi(x, c, sa, sb):
    return x * c + pltpu.roll(x, 32, 1) * sa + pltpu.roll(x, 96, 1) * sb


def _proj_small_kernel(x_ref, w_ref, gq_ref, gkv_ref, c_ref, sa_ref, sb_ref,
                       cqn_ref, ckv_ref, ckvb_ref, kr_ref, krp_ref, *, ql, kl, rope):
    acc = jnp.dot(x_ref[...], w_ref[...], preferred_element_type=F32)
    cqn_ref[...] = _rms(acc[:, :ql], gq_ref[...]).astype(cqn_ref.dtype)
    ckv = _rms(acc[:, ql:ql + kl], gkv_ref[...])
    ckv_ref[...] = ckv
    ckvb_ref[...] = ckv.astype(ckvb_ref.dtype)
    kr = _rope_hi(acc[:, ql + kl:ql + kl + LANES], c_ref[...], sa_ref[...], sb_ref[...])
    kr_ref[...] = kr[:, :rope]
    krp_ref[...] = kr.astype(krp_ref.dtype)


def proj_small(xn, wa, gq, gkv, c, sa, sb, ql, kl, rope, tm=256):
    m, d = xn.shape
    n = wa.shape[1]
    row = lambda w: pl.BlockSpec((tm, w), lambda i: (i, 0))
    full = lambda a: pl.BlockSpec(a.shape, lambda i: (0, 0))
    return pl.pallas_call(
        functools.partial(_proj_small_kernel, ql=ql, kl=kl, rope=rope),
        out_shape=(jax.ShapeDtypeStruct((m, ql), BF16),
                   jax.ShapeDtypeStruct((m, kl), F32),
                   jax.ShapeDtypeStruct((m, kl), BF16),
                   jax.ShapeDtypeStruct((m, rope), F32),
                   jax.ShapeDtypeStruct((m, LANES), BF16)),
        grid=(m // tm,),
        in_specs=[row(d), full(wa), full(gq), full(gkv), row(LANES), row(LANES), row(LANES)],
        out_specs=(row(ql), row(kl), row(kl), row(rope), row(LANES)),
        compiler_params=_params(("parallel",), VMEM_LIMIT_BYTES),
        name="proj_small",
    )(xn, wa, gq, gkv, c, sa, sb)


def _proj_big_kernel(x_ref, w_ref, cos_ref, sin_ref, o_ref, *, n_rope, dk, k_scale):
    j = pl.program_id(0)
    acc = jnp.dot(x_ref[...], w_ref[...], preferred_element_type=F32)

    @pl.when(j >= n_rope)
    def _():
        o_ref[...] = acc.astype(o_ref.dtype)

    @pl.when(j < n_rope)
    def _():
        scale = jnp.where(j >= n_rope // 2, k_scale, 1.0).astype(F32)
        c = cos_ref[...] * scale
        s = sin_ref[...] * scale
        half = dk // 2
        for h in range(acc.shape[1] // dk):
            x1 = acc[:, h * dk:h * dk + half]
            x2 = acc[:, h * dk + half:(h + 1) * dk]
            o_ref[:, h * dk:h * dk + half] = (x1 * c - x2 * s).astype(o_ref.dtype)
            o_ref[:, h * dk + half:(h + 1) * dk] = (x2 * c + x1 * s).astype(o_ref.dtype)


def proj_big(xn, wb, cos_r, sin_r, rope_cols, dk, k_scale, tm=512, tn=1024):
    m, d = xn.shape
    n = wb.shape[1]
    return pl.pallas_call(
        functools.partial(_proj_big_kernel, n_rope=rope_cols // tn, dk=dk, k_scale=k_scale),
        out_shape=jax.ShapeDtypeStruct((m, n), BF16),
        grid=(n // tn, m // tm),
        in_specs=[pl.BlockSpec((tm, d), lambda j, i: (i, 0)),
                  pl.BlockSpec((d, tn), lambda j, i: (0, j)),
                  pl.BlockSpec((tm, dk // 2), lambda j, i: (i, 0)),
                  pl.BlockSpec((tm, dk // 2), lambda j, i: (i, 0))],
        out_specs=pl.BlockSpec((tm, tn), lambda j, i: (i, j)),
        compiler_params=_params(("parallel", "parallel"), VMEM_LIMIT_BYTES),
        name="proj_big",
    )(xn, wb, cos_r, sin_r)


def _q_proj_kernel(x_ref, w_ref, c_ref, sa_ref, sb_ref, o_ref, *, heads, scale):
    acc = jnp.dot(x_ref[...], w_ref[...], preferred_element_type=F32)
    c = c_ref[...] * scale
    sa = sa_ref[...] * scale
    sb = sb_ref[...] * scale
    for h in range(heads):
        lo = 2 * h * LANES
        o_ref[:, lo:lo + LANES] = (acc[:, lo:lo + LANES] * scale).astype(o_ref.dtype)
        o_ref[:, lo + LANES:lo + 2 * LANES] = _rope_hi(
            acc[:, lo + LANES:lo + 2 * LANES], c, sa, sb).astype(o_ref.dtype)


def q_proj(cqn, wq, c, sa, sb, heads, scale, tm=256):
    m, k = cqn.shape
    n = wq.shape[1]
    row = lambda w: pl.BlockSpec((tm, w), lambda i: (i, 0))
    return pl.pallas_call(
        functools.partial(_q_proj_kernel, heads=heads, scale=scale),
        out_shape=jax.ShapeDtypeStruct((m, n), BF16),
        grid=(m // tm,),
        in_specs=[row(k), pl.BlockSpec((k, n), lambda i: (0, 0)), row(LANES), row(LANES), row(LANES)],
        out_specs=row(n),
        compiler_params=_params(("parallel",), VMEM_LIMIT_BYTES),
        name="q_proj",
    )(cqn, wq, c, sa, sb)


def _kv_expand_kernel(c_ref, kr_ref, wk_ref, wv_ref, k_ref, v_ref, *, heads):
    c = c_ref[...]
    kn = jnp.dot(c, wk_ref[...], preferred_element_type=F32)
    v_ref[...] = jnp.dot(c, wv_ref[...], preferred_element_type=F32).astype(v_ref.dtype)
    kr = kr_ref[...]
    for h in range(heads):
        k_ref[:, 2 * h * LANES:(2 * h + 1) * LANES] = kn[:, h * LANES:(h + 1) * LANES].astype(k_ref.dtype)
        k_ref[:, (2 * h + 1) * LANES:(2 * h + 2) * LANES] = kr


def kv_expand(c_bf, kr_pad, wk, wv, rows, heads, tm=256):
    kl = c_bf.shape[1]
    row = lambda w: pl.BlockSpec((tm, w), lambda i: (i, 0))
    full = lambda a: pl.BlockSpec(a.shape, lambda i: (0, 0))
    return pl.pallas_call(
        functools.partial(_kv_expand_kernel, heads=heads),
        out_shape=(jax.ShapeDtypeStruct((rows, 2 * heads * LANES), BF16),
                   jax.ShapeDtypeStruct((rows, heads * LANES), BF16)),
        grid=(rows // tm,),
        in_specs=[row(kl), row(LANES), full(wk), full(wv)],
        out_specs=(row(2 * heads * LANES), row(heads * LANES)),
        compiler_params=_params(("parallel",), VMEM_LIMIT_BYTES),
        name="kv_expand",
    )(c_bf, kr_pad, wk, wv)


def _attn_kernel(q_ref, k_ref, v_ref, o_ref, *, tq, tk, n_kv, causal):
    i = pl.program_id(2)
    q = q_ref[...]

    def step(j, carry, masked):
        m, l, acc = carry
        off = pl.multiple_of(j * tk, tk)
        k = k_ref[pl.ds(off, tk), :]
        v = v_ref[pl.ds(off, tk), :]
        s = lax.dot_general(q, k, (((1,), (1,)), ((), ())), preferred_element_type=F32)
        if masked:
            shift = CHUNK.bit_length() - 1
            qc = lax.shift_right_logical(i * tq + lax.broadcasted_iota(jnp.int32, s.shape, 0), shift)
            kc = lax.shift_right_logical(j * tk + lax.broadcasted_iota(jnp.int32, s.shape, 1), shift)
            s = jnp.where(kc <= qc, s, -1e30)
        m_new = jnp.maximum(m, jnp.max(s, axis=-1, keepdims=True))
        alpha = jnp.exp(m - m_new)
        p = jnp.exp(s - m_new)
        l = alpha * l + jnp.sum(p, axis=-1, keepdims=True)
        acc = alpha * acc + jnp.dot(p.astype(BF16), v, preferred_element_type=F32)
        return m_new, l, acc

    carry = (jnp.full((tq, 1), -jnp.inf, F32), jnp.zeros((tq, 1), F32),
             jnp.zeros((tq, v_ref.shape[1]), F32))
    if causal:
        carry = lax.fori_loop(0, i, functools.partial(step, masked=False), carry)
        carry = step(i, carry, True)
    else:
        carry = lax.fori_loop(0, n_kv, functools.partial(step, masked=False), carry)
    _, l, acc = carry
    o_ref[...] = (acc / l).astype(o_ref.dtype)


def attention(q, k, v, heads, *, batch, tq_total, q_row0, tk_total, tq, tk, causal):
    assert CHUNK & (CHUNK - 1) == 0 and tq % CHUNK == 0
    if causal:
        assert tq == tk
    nq = tq_total // tq
    qb0 = q_row0 // tq
    dq = 2 * LANES
    return pl.pallas_call(
        functools.partial(_attn_kernel, tq=tq, tk=tk, n_kv=tk_total // tk, causal=causal),
        out_shape=jax.ShapeDtypeStruct((batch * tq_total, heads * LANES), BF16),
        grid=(batch, heads, nq),
        in_specs=[pl.BlockSpec((tq, dq), lambda b, h, i: (qb0 + b * nq + i, h)),
                  pl.BlockSpec((tk_total, dq), lambda b, h, i: (b, h)),
                  pl.BlockSpec((tk_total, LANES), lambda b, h, i: (b, h))],
        out_specs=pl.BlockSpec((tq, LANES), lambda b, h, i: (b * nq + i, h)),
        compiler_params=_params(("parallel", "parallel", "arbitrary"), VMEM_LIMIT_BYTES),
        name="attention",
    )(q, k, v)


def _retention_kernel(q_ref, k_ref, v_ref, s0_ref, d_ref, xi_ref, zeta_ref, gc_ref,
                      o_ref, sout_ref, s_scr, *, n_chunks):
    c = pl.program_id(2)

    @pl.when(c == 0)
    def _():
        s_scr[...] = s0_ref[...]

    q = q_ref[...]
    k = k_ref[...]
    v = v_ref[...]
    s = s_scr[...]
    a = lax.dot_general(q, k, (((1,), (1,)), ((), ())), preferred_element_type=F32) * d_ref[...]
    o = jnp.dot(a.astype(BF16), v, preferred_element_type=F32)
    o = o + jnp.dot(q, s.astype(BF16), preferred_element_type=F32) * xi_ref[...]
    kz = (k.astype(F32) * zeta_ref[...]).astype(BF16)
    s_new = s * gc_ref[...] + lax.dot_general(kz, v, (((0,), (0,)), ((), ())),
                                              preferred_element_type=F32)
    s_scr[...] = s_new
    o_ref[...] = (o * lax.rsqrt(jnp.mean(o * o, axis=-1, keepdims=True) + NORM_EPS)).astype(o_ref.dtype)

    @pl.when(c == n_chunks - 1)
    def _():
        sout_ref[...] = s_new


def retention(big, s0, tabs, heads, dk, *, batch, t_total, row0, chunk):
    d_tab, xi_tab, zeta_tab, gc_tab = tabs
    nc = t_total // chunk
    rb0 = row0 // chunk
    rowblk = lambda col0: pl.BlockSpec((chunk, dk), lambda b, h, c: (rb0 + b * nc + c, col0 + h))
    per_head = lambda a: pl.BlockSpec((None,) + a.shape[1:], lambda b, h, c: (h, 0, 0))
    st_spec = pl.BlockSpec((None, None, dk, dk), lambda b, h, c: (b, h, 0, 0))
    return pl.pallas_call(
        functools.partial(_retention_kernel, n_chunks=nc),
        out_shape=(jax.ShapeDtypeStruct((batch * t_total, heads * dk), BF16),
                   jax.ShapeDtypeStruct((batch, heads, dk, dk), F32)),
        grid=(batch, heads, nc),
        in_specs=[rowblk(0), rowblk(heads), rowblk(2 * heads), st_spec,
                  per_head(d_tab), per_head(xi_tab), per_head(zeta_tab), per_head(gc_tab)],
        out_specs=(pl.BlockSpec((chunk, dk), lambda b, h, c: (b * nc + c, h)), st_spec),
        scratch_shapes=[pltpu.VMEM((dk, dk), F32)],
        compiler_params=_params(("parallel", "parallel", "arbitrary"), VMEM_LIMIT_BYTES),
        name="retention",
    )(big, big, big, s0, d_tab, xi_tab, zeta_tab, gc_tab)


def retention_tables(length, heads, dk):
    lg = jnp.log1p(-jnp.exp2(-5.0 - jnp.arange(heads, dtype=F32)))
    idx = jnp.arange(length, dtype=F32)
    diff = idx[:, None] - idx[None, :]
    decay = jnp.where(diff >= 0, jnp.exp(jnp.maximum(diff, 0.0)[None] * lg[:, None, None]), 0.0)
    xi = jnp.exp((idx + 1.0)[None, :] * lg[:, None])
    zeta = jnp.exp((length - 1.0 - idx)[None, :] * lg[:, None])
    gc = jnp.exp(length * lg)
    bc = lambda a: jnp.broadcast_to(a[:, :, None], (heads, a.shape[1], dk))
    return decay, bc(xi), bc(zeta), jnp.broadcast_to(gc[:, None, None], (heads, 1, dk))


def _mixer_out_kernel(oap_ref, oas_ref, orp_ref, ors_ref, rg_ref, ga_ref, gb_ref, x_ref, wo_ref, g_ref,
                      wr_ref, br_ref, h_ref, hn_ref, lg_ref, *, prompt_blocks):
    is_prompt = pl.program_id(0) < prompt_blocks
    oa = jnp.where(is_prompt, oap_ref[...], oas_ref[...]).astype(F32)
    orn = jnp.where(is_prompt, orp_ref[...], ors_ref[...]).astype(F32)
    rg = rg_ref[...].astype(F32)
    yb = rg * _sigmoid(rg) * orn
    merged = _sigmoid(ga_ref[...].astype(F32)) * oa + _sigmoid(gb_ref[...].astype(F32)) * yb
    h = x_ref[...] + jnp.dot(merged.astype(BF16), wo_ref[...], preferred_element_type=F32)
    h_ref[...] = h
    hn = _rms(h, g_ref[...])
    hn_ref[...] = hn
    lg_ref[...] = jnp.dot(hn, wr_ref[...], preferred_element_type=F32,
                          precision=lax.Precision.HIGHEST) + br_ref[...]


def mixer_out(oa_p, oa_s, or_p, or_s, big, gate_blk0, x, wo, g, wr, br, tm=256):
    m, d = x.shape
    e = wr.shape[1]
    npb = oa_p.shape[0] // tm
    nsb = oa_s.shape[0] // tm
    row = lambda col: pl.BlockSpec((tm, d), lambda i: (i, col))
    prompt = pl.BlockSpec((tm, d), lambda i: (jnp.minimum(i, npb - 1), 0))
    sample = pl.BlockSpec((tm, d), lambda i: (jnp.clip(i - npb, 0, nsb - 1), 0))
    full = lambda a: pl.BlockSpec(a.shape, lambda i: (0, 0))
    return pl.pallas_call(
        functools.partial(_mixer_out_kernel, prompt_blocks=npb),
        out_shape=(jax.ShapeDtypeStruct((m, d), F32),
                   jax.ShapeDtypeStruct((m, d), F32),
                   jax.ShapeDtypeStruct((m, e), F32)),
        grid=(m // tm,),
        in_specs=[prompt, sample, prompt, sample, row(gate_blk0), row(gate_blk0 + 1), row(gate_blk0 + 2),
                  row(0), full(wo), full(g), full(wr), full(br)],
        out_specs=(row(0), row(0), pl.BlockSpec((tm, e), lambda i: (i, 0))),
        compiler_params=_params(("parallel",), VMEM_LIMIT_BYTES),
        name="mixer_out",
    )(oa_p, oa_s, or_p, or_s, big, big, big, x, wo, g, wr, br)


def _gather_rows_kernel(tok_ref, src_ref, o_ref, buf, sems, *, rows, n_steps):
    i = pl.program_id(0)

    def copy(r, tok, par):
        return pltpu.make_async_copy(src_ref.at[pl.ds(tok, 1), :], buf.at[par, pl.ds(r, 1), :],
                                     sems.at[par])

    def issue(step, par):
        def body(r, _):
            copy(r, tok_ref[step * rows + r], par).start()
            return 0
        lax.fori_loop(0, rows, body, 0, unroll=8)

    @pl.when(i == 0)
    def _():
        issue(0, 0)

    @pl.when(i + 1 < n_steps)
    def _():
        issue(i + 1, (i + 1) % 2)

    par = i % 2

    def drain(r, _):
        copy(0, 0, par).wait()
        return 0
    lax.fori_loop(0, rows, drain, 0, unroll=8)
    o_ref[...] = buf[par]


def gather_rows(slot_tok, src, rows=MOE_ROWS):
    n = slot_tok.shape[0]
    d = src.shape[1]
    n_steps = n // rows
    return pl.pallas_call(
        functools.partial(_gather_rows_kernel, rows=rows, n_steps=n_steps),
        out_shape=jax.ShapeDtypeStruct((n, d), src.dtype),
        grid_spec=pltpu.PrefetchScalarGridSpec(
            num_scalar_prefetch=1, grid=(n_steps,),
            in_specs=[pl.BlockSpec(memory_space=pl.ANY)],
            out_specs=pl.BlockSpec((rows, d), lambda i, tok: (i, 0)),
            scratch_shapes=[pltpu.VMEM((2, rows, d), src.dtype), pltpu.SemaphoreType.DMA((2,))]),
        compiler_params=_params(("arbitrary",)),
        name="moe_gather",
    )(slot_tok, src)


def _expert_kernel(ie_ref, ib_ref, inb_ref, nv_ref, xs_ref, wg_ref, wu_ref, bg_ref, bu_ref, wd_ref, bd_ref,
                   ye_ref, x32, xb, y32, wgb, wub, wdb, sem_in, sem_out, *, rows, n_f):
    k = pl.program_id(0)
    f = pl.program_id(1)
    nb = inb_ref[k]
    blk0 = ib_ref[k]

    def in_copy(b):
        return pltpu.make_async_copy(xs_ref.at[pl.ds((blk0 + b) * rows, rows), :],
                                     x32.at[pl.ds(b * rows, rows), :], sem_in)

    def out_copy(b):
        return pltpu.make_async_copy(y32.at[pl.ds(b * rows, rows), :],
                                     ye_ref.at[pl.ds((blk0 + b) * rows, rows), :], sem_out)

    def for_blocks(fn):
        def body(b, _):
            fn(b)
            return 0
        lax.fori_loop(0, nb, body, 0)

    @pl.when(f == 0)
    def _():
        for_blocks(lambda b: in_copy(b).start())
        for_blocks(lambda b: in_copy(b).wait())

        def cast(b):
            r = pl.ds(pl.multiple_of(b * rows, rows), rows)
            xb[r, :] = x32[r, :].astype(BF16)
            y32[r, :] = jnp.broadcast_to(bd_ref[...], (rows, y32.shape[1]))
        for_blocks(cast)

    @pl.when(nb > 0)
    def _():
        wgb[...] = wg_ref[...].astype(BF16)
        wub[...] = wu_ref[...].astype(BF16)
        wdb[...] = wd_ref[...].astype(BF16)

    def compute(b):
        r = pl.ds(pl.multiple_of(b * rows, rows), rows)
        x = xb[r, :]
        gate = jnp.minimum(jnp.dot(x, wgb[...], preferred_element_type=F32) + bg_ref[...], SWIGLU_LIMIT)
        up = jnp.clip(jnp.dot(x, wub[...], preferred_element_type=F32) + bu_ref[...],
                      -SWIGLU_LIMIT, SWIGLU_LIMIT)
        h = (up + 1.0) * (gate * _sigmoid(gate * SWIGLU_ALPHA))
        y32[r, :] += jnp.dot(h.astype(BF16), wdb[...], preferred_element_type=F32)
    for_blocks(compute)

    @pl.when(f == n_f - 1)
    def _():
        for_blocks(lambda b: out_copy(b).start())
        for_blocks(lambda b: out_copy(b).wait())


def expert_mlp(items, xs, w_gate_up, b_gate_up, w_down, b_down, n_items):
    item_e, item_b0, item_nb, n_valid = items
    n_exp, d, two_f = w_gate_up.shape
    d_ff = two_f // 2
    tf = MOE_FTILE
    n_f = d_ff // tf
    rows = MOE_ROWS
    rs = rows * MOE_GROUP

    def fi(k, f, nv):
        return jnp.where(k < nv[0], f, n_f - 1)

    in_specs = [
        pl.BlockSpec(memory_space=pl.ANY),
        pl.BlockSpec((None, d, tf), lambda k, f, ie, ib, inb, nv: (ie[k], 0, fi(k, f, nv))),
        pl.BlockSpec((None, d, tf), lambda k, f, ie, ib, inb, nv: (ie[k], 0, n_f + fi(k, f, nv))),
        pl.BlockSpec((None, 1, tf), lambda k, f, ie, ib, inb, nv: (ie[k], 0, fi(k, f, nv))),
        pl.BlockSpec((None, 1, tf), lambda k, f, ie, ib, inb, nv: (ie[k], 0, n_f + fi(k, f, nv))),
        pl.BlockSpec((None, tf, d), lambda k, f, ie, ib, inb, nv: (ie[k], fi(k, f, nv), 0)),
        pl.BlockSpec((None, 1, d), lambda k, f, ie, ib, inb, nv: (ie[k], 0, 0)),
    ]
    return pl.pallas_call(
        functools.partial(_expert_kernel, rows=rows, n_f=n_f),
        out_shape=jax.ShapeDtypeStruct(xs.shape, F32),
        grid_spec=pltpu.PrefetchScalarGridSpec(
            num_scalar_prefetch=4, grid=(n_items, n_f),
            in_specs=in_specs,
            out_specs=pl.BlockSpec(memory_space=pl.ANY),
            scratch_shapes=[pltpu.VMEM((rs, d), F32), pltpu.VMEM((rs, d), BF16), pltpu.VMEM((rs, d), F32),
                            pltpu.VMEM((d, tf), BF16), pltpu.VMEM((d, tf), BF16), pltpu.VMEM((tf, d), BF16),
                            pltpu.SemaphoreType.DMA, pltpu.SemaphoreType.DMA]),
        input_output_aliases={4: 0},
        compiler_params=_params(("arbitrary", "arbitrary"), VMEM_LIMIT_BYTES),
        name="moe_experts",
    )(item_e, item_b0, item_nb, n_valid, xs, w_gate_up, w_gate_up,
      b_gate_up.reshape(n_exp, 1, two_f), b_gate_up.reshape(n_exp, 1, two_f),
      w_down, b_down.reshape(n_exp, 1, d))


def _combine_kernel(pos_ref, ye_ref, w_ref, h_ref, g_ref, o_ref, buf, sems, *, tc, blk0, n_steps, final_norm):
    i = pl.program_id(0)
    n_rows = TOP_K * tc

    def copy(r, src_row, par):
        return pltpu.make_async_copy(ye_ref.at[pl.ds(src_row, 1), :], buf.at[par, pl.ds(r, 1), :],
                                     sems.at[par])

    def issue(step, par):
        def body(r, _):
            copy(r, pos_ref[(blk0 + step) * n_rows + r], par).start()
            return 0
        lax.fori_loop(0, n_rows, body, 0, unroll=8)

    @pl.when(i == 0)
    def _():
        issue(0, 0)

    @pl.when(i + 1 < n_steps)
    def _():
        issue(i + 1, (i + 1) % 2)

    par = i % 2

    def drain(r, _):
        copy(0, 0, par).wait()
        return 0
    lax.fori_loop(0, n_rows, drain, 0, unroll=8)

    w = w_ref[...]
    y = h_ref[...]
    for k in range(TOP_K):
        y = y + w[:, k:k + 1] * buf[par, k * tc:(k + 1) * tc, :]
    o_ref[...] = _rms(y, g_ref[...]) if final_norm else y


def combine(pos, ye, top_w, h, g, row0, rows, final_norm, tc=COMBINE_ROWS):
    d = h.shape[1]
    blk0 = row0 // tc
    n_steps = rows // tc
    return pl.pallas_call(
        functools.partial(_combine_kernel, tc=tc, blk0=blk0, n_steps=n_steps, final_norm=final_norm),
        out_shape=jax.ShapeDtypeStruct((rows, d), F32),
        grid_spec=pltpu.PrefetchScalarGridSpec(
            num_scalar_prefetch=1, grid=(n_steps,),
            in_specs=[pl.BlockSpec(memory_space=pl.ANY),
                      pl.BlockSpec((tc, TOP_K), lambda i, p: (blk0 + i, 0)),
                      pl.BlockSpec((tc, d), lambda i, p: (blk0 + i, 0)),
                      pl.BlockSpec((1, d), lambda i, p: (0, 0))],
            out_specs=pl.BlockSpec((tc, d), lambda i, p: (i, 0)),
            scratch_shapes=[pltpu.VMEM((2, TOP_K * tc, d), F32), pltpu.SemaphoreType.DMA((2,))]),
        compiler_params=_params(("arbitrary",), VMEM_LIMIT_BYTES),
        name="moe_combine",
    )(pos, ye, top_w, h, g)


def route(logits, tc):
    m, n_exp = logits.shape
    a = m * TOP_K
    rows, group = MOE_ROWS, MOE_GROUP
    top_logit, top_idx = lax.top_k(logits, TOP_K)
    top_w = jax.nn.softmax(top_logit, axis=-1)
    flat_e = top_idx.reshape(a)
    onehot = (flat_e[:, None] == jnp.arange(n_exp, dtype=jnp.int32)[None, :]).astype(jnp.int32)
    csum = jnp.cumsum(onehot, axis=0)
    counts = csum[-1]
    rank = jnp.take_along_axis(csum, flat_e[:, None], axis=1)[:, 0] - 1
    nb_e = (counts + rows - 1) // rows
    blk_start = jnp.cumsum(nb_e) - nb_e
    dest = blk_start[flat_e] * rows + rank
    n_blocks = a // rows + n_exp
    slot_tok = jnp.zeros((n_blocks * rows,), jnp.int32).at[dest].set(
        jnp.arange(a, dtype=jnp.int32) // TOP_K)
    ni_e = (nb_e + group - 1) // group
    item_end = jnp.cumsum(ni_e)
    n_valid = item_end[-1]
    n_items = (n_blocks + n_exp * (group - 1)) // group
    kk = jnp.arange(n_items, dtype=jnp.int32)
    item_e = jnp.minimum(jnp.sum(item_end[None, :] <= kk[:, None], axis=1), n_exp - 1).astype(jnp.int32)
    local = kk - (item_end - ni_e)[item_e]
    valid = kk < n_valid
    item_e = jnp.where(valid, item_e, item_e[jnp.maximum(n_valid - 1, 0)])
    item_b0 = jnp.where(valid, blk_start[item_e] + local * group, 0).astype(jnp.int32)
    item_nb = jnp.where(valid, jnp.clip(nb_e[item_e] - local * group, 0, group), 0).astype(jnp.int32)
    pos = dest.reshape(m // tc, tc, TOP_K).transpose(0, 2, 1).reshape(a)
    items = (item_e, item_b0, item_nb, n_valid.reshape(1).astype(jnp.int32))
    return top_w, slot_tok, pos.astype(jnp.int32), items, n_items


def _rope_tables(pos, half):
    inv = jnp.power(ROPE_THETA, -jnp.arange(half, dtype=F32) / half)
    ang = pos.astype(F32)[:, None] * inv[None, :]
    return jnp.cos(ang), jnp.sin(ang)


def kernel(x_prompt, x_sample, cache_latent, cache_k_rope, state_retention, g_mix_norm, w_in, g_q_norm, w_uq, g_kv_norm, w_uk, w_uv, w_o, g_ffn_norm, w_router, b_router, w_gate_up, b_gate_up, w_down, b_down, g_final):
    bp, t, d = x_prompt.shape
    db, ts, _ = x_sample.shape
    depth = w_in.shape[0]
    past = cache_latent.shape[2]
    ql, ha, qk = w_uq.shape[1:]
    kl = w_uk.shape[1]
    rope = cache_k_rope.shape[-1]
    nope = qk - rope
    hr, dk = state_retention.shape[2:4]
    assert bp == 1 and nope == LANES and w_uv.shape[-1] == LANES and 2 * rope == LANES
    mp, ms = bp * t, db * ts
    m = mp + ms
    tk_s = past + ts
    mla_scale = float(qk) ** -0.5

    pos = jnp.concatenate([jnp.arange(t, dtype=jnp.int32),
                           jnp.tile(past + jnp.arange(ts, dtype=jnp.int32), db)])
    cos_a, sin_a = _rope_tables(pos, rope // 2)
    z = lambda w: jnp.zeros((m, w), F32)
    rope_c = jnp.concatenate([cos_a, cos_a, z(LANES - rope)], axis=1)
    rope_sa = jnp.concatenate([z(rope // 2), sin_a, z(LANES - rope)], axis=1)
    rope_sb = jnp.concatenate([-sin_a, z(LANES - rope // 2)], axis=1)
    cos_r, sin_r = _rope_tables(pos, dk // 2)
    tabs_p = retention_tables(RET_CHUNK, hr, dk)
    tabs_s = retention_tables(ts, hr, dk)

    x = jnp.concatenate([x_prompt.reshape(mp, d), x_sample.reshape(ms, d)], axis=0)
    lat_p, kr_p, st_p, lat_s, kr_s, st_s = [], [], [], [], [], []
    small = ql + kl + rope
    for l in range(depth):
        wa = jnp.pad(w_in[l][:, :small], ((0, 0), (0, LANES - rope))).astype(BF16)
        wb = w_in[l][:, small:].astype(BF16)
        wq = jnp.pad(w_uq[l], ((0, 0), (0, 0), (0, 2 * LANES - qk))).reshape(ql, ha * 2 * LANES).astype(BF16)
        wk = w_uk[l].reshape(kl, ha * nope).astype(BF16)
        wv = w_uv[l].reshape(kl, ha * LANES).astype(BF16)
        wo = w_o[l].astype(BF16)

        xn = norm_cast(x, g_mix_norm[l].reshape(1, d))
        cqn, ckv, ckv_b, kr, kr_pad = proj_small(xn, wa, g_q_norm[l].reshape(1, ql), g_kv_norm[l].reshape(1, kl),
                                                 rope_c, rope_sa, rope_sb, ql, kl, rope)
        big = proj_big(xn, wb, cos_r, sin_r, 2 * hr * dk, dk, float(dk) ** -0.5)
        qcat = q_proj(cqn, wq, rope_c, rope_sa, rope_sb, ha, mla_scale)

        k_p, v_p = kv_expand(ckv_b, kr_pad, wk, wv, mp, ha)
        oa_p = attention(qcat, k_p, v_p, ha, batch=1, tq_total=mp, q_row0=0, tk_total=mp,
                         tq=512, tk=512, causal=True)
        c_all = jnp.concatenate([cache_latent[l], ckv[mp:].reshape(db, ts, kl)], axis=1)
        kr_all = jnp.concatenate([cache_k_rope[l], kr[mp:].reshape(db, ts, rope)], axis=1)
        kr_all = jnp.pad(kr_all, ((0, 0), (0, 0), (0, LANES - rope)))
        k_s, v_s = kv_expand(c_all.reshape(db * tk_s, kl).astype(BF16),
                             kr_all.reshape(db * tk_s, LANES).astype(BF16), wk, wv, db * tk_s, ha)
        oa_s = attention(qcat, k_s, v_s, ha, batch=db, tq_total=ts, q_row0=mp, tk_total=tk_s,
                         tq=ts, tk=tk_s, causal=False)

        or_p, s_p = retention(big, jnp.zeros((bp, hr, dk, dk), F32), tabs_p, hr, dk,
                              batch=bp, t_total=t, row0=0, chunk=RET_CHUNK)
        or_s, s_s = retention(big, state_retention[l].astype(F32), tabs_s, hr, dk,
                              batch=db, t_total=ts, row0=mp, chunk=ts)

        h1, hn, logits = mixer_out(oa_p, oa_s, or_p, or_s, big, 3, x, wo, g_ffn_norm[l].reshape(1, d),
                                   w_router[l], b_router[l].reshape(1, -1))

        top_w, slot_tok, pos_flat, items, n_items = route(logits, COMBINE_ROWS)
        xs = gather_rows(slot_tok, hn)
        ye = expert_mlp(items, xs, w_gate_up[l], b_gate_up[l], w_down[l], b_down[l], n_items)
        last = l == depth - 1
        g_out = g_final.reshape(1, d)
        y_p = combine(pos_flat, ye, top_w, h1, g_out, 0, mp, last)
        y_s = combine(pos_flat, ye, top_w, h1, g_out, mp, ms, last)
        x = jnp.concatenate([y_p, y_s], axis=0) if not last else None

        lat_p.append(ckv[:mp].reshape(bp, t, kl))
        kr_p.append(kr[:mp].reshape(bp, t, rope))
        st_p.append(s_p)
        lat_s.append(ckv[mp:].reshape(db, ts, kl))
        kr_s.append(kr[mp:].reshape(db, ts, rope))
        st_s.append(s_s)

    return (y_p.reshape(bp, t, d), y_s.reshape(db, ts, d),
            jnp.stack(lat_p), jnp.stack(kr_p), jnp.stack(st_p),
            jnp.stack(lat_s), jnp.stack(kr_s), jnp.stack(st_s))
```

```python
import functools

import jax
import jax.numpy as jnp
from jax import lax
from jax.experimental import pallas as pl
from jax.experimental.pallas import tpu as pltpu

F32 = jnp.float32
BF16 = jnp.bfloat16

NORM_EPS = 1e-6
ROPE_THETA = 10000.0
CHUNK = 64
TOP_K = 4
SWIGLU_LIMIT = 7.0
SWIGLU_ALPHA = 1.702

LANES = 128
VMEM_LIMIT_BYTES = 56 * 1024 * 1024

RET_CHUNK = 256
MOE_ROWS = 256
MOE_GROUP = 5
MOE_FTILE = 256
COMBINE_ROWS = 128


def _params(sem, vmem=None):
    return pltpu.CompilerParams(dimension_semantics=sem, vmem_limit_bytes=vmem)


def _rms(x, g):
    return x * lax.rsqrt(jnp.mean(x * x, axis=-1, keepdims=True) + NORM_EPS) * g


def _sigmoid(x):
    return 1.0 / (1.0 + jnp.exp(-x))


def _group_specs(rows_p, rows_s, tm, d):
    npb, nsb = rows_p // tm, rows_s // tm
    prompt = pl.BlockSpec((tm, d), lambda i: (jnp.minimum(i, npb - 1), 0))
    sample = pl.BlockSpec((tm, d), lambda i: (jnp.clip(i - npb, 0, nsb - 1), 0))
    return npb, prompt, sample


def _norm_cast_kernel(xp_ref, xs_ref, g_ref, o_ref, *, prompt_blocks):
    x = jnp.where(pl.program_id(0) < prompt_blocks, xp_ref[...], xs_ref[...])
    o_ref[...] = _rms(x, g_ref[...]).astype(o_ref.dtype)


def norm_cast(xp, xs, g, tm=512):
    d = xp.shape[1]
    m = xp.shape[0] + xs.shape[0]
    npb, prompt, sample = _group_specs(xp.shape[0], xs.shape[0], tm, d)
    return pl.pallas_call(
        functools.partial(_norm_cast_kernel, prompt_blocks=npb),
        out_shape=jax.ShapeDtypeStruct((m, d), BF16),
        grid=(m // tm,),
        in_specs=[prompt, sample, pl.BlockSpec((1, d), lambda i: (0, 0))],
        out_specs=pl.BlockSpec((tm, d), lambda i: (i, 0)),
        compiler_params=_params(("parallel",), VMEM_LIMIT_BYTES),
        name="norm_cast",
    )(xp, xs, g)


def _rope_hi(x, c, sa, sb):
    return x * c + pltpu.roll(x, 32, 1) * sa + pltpu.roll(x, 96, 1) * sb


def _proj_small_kernel(x_ref, w_ref, gq_ref, gkv_ref, c_ref, sa_ref, sb_ref,
                       cqn_ref, ckv_ref, ckvb_ref, kr_ref, krp_ref, *, ql, kl, rope):
    acc = jnp.dot(x_ref[...], w_ref[...], preferred_element_type=F32)
    cqn_ref[...] = _rms(acc[:, :ql], gq_ref[...]).astype(cqn_ref.dtype)
    ckv = _rms(acc[:, ql:ql + kl], gkv_ref[...])
    ckv_ref[...] = ckv
    ckvb_ref[...] = ckv.astype(ckvb_ref.dtype)
    kr = _rope_hi(acc[:, ql + kl:ql + kl + LANES], c_ref[...], sa_ref[...], sb_ref[...])
    kr_ref[...] = kr[:, :rope]
    krp_ref[...] = kr.astype(krp_ref.dtype)


def proj_small(xn, wa, gq, gkv, c, sa, sb, ql, kl, rope, tm=256):
    m, d = xn.shape
    n = wa.shape[1]
    row = lambda w: pl.BlockSpec((tm, w), lambda i: (i, 0))
    full = lambda a: pl.BlockSpec(a.shape, lambda i: (0, 0))
    return pl.pallas_call(
        functools.partial(_proj_small_kernel, ql=ql, kl=kl, rope=rope),
        out_shape=(jax.ShapeDtypeStruct((m, ql), BF16),
                   jax.ShapeDtypeStruct((m, kl), F32),
                   jax.ShapeDtypeStruct((m, kl), BF16),
                   jax.ShapeDtypeStruct((m, rope), F32),
                   jax.ShapeDtypeStruct((m, LANES), BF16)),
        grid=(m // tm,),
        in_specs=[row(d), full(wa), full(gq), full(gkv), row(LANES), row(LANES), row(LANES)],
        out_specs=(row(ql), row(kl), row(kl), row(rope), row(LANES)),
        compiler_params=_params(("parallel",), VMEM_LIMIT_BYTES),
        name="proj_small",
    )(xn, wa, gq, gkv, c, sa, sb)


def _proj_big_kernel(x_ref, w_ref, cos_ref, sin_ref, o_ref, *, n_rope, dk, k_scale):
    j = pl.program_id(0)
    acc = jnp.dot(x_ref[...], w_ref[...], preferred_element_type=F32)

    @pl.when(j >= n_rope)
    def _():
        o_ref[...] = acc.astype(o_ref.dtype)

    @pl.when(j < n_rope)
    def _():
        scale = jnp.where(j >= n_rope // 2, k_scale, 1.0).astype(F32)
        c = cos_ref[...] * scale
        s = sin_ref[...] * scale
        half = dk // 2
        for h in range(acc.shape[1] // dk):
            x1 = acc[:, h * dk:h * dk + half]
            x2 = acc[:, h * dk + half:(h + 1) * dk]
            o_ref[:, h * dk:h * dk + half] = (x1 * c - x2 * s).astype(o_ref.dtype)
            o_ref[:, h * dk + half:(h + 1) * dk] = (x2 * c + x1 * s).astype(o_ref.dtype)


def proj_big(xn, wb, cos_r, sin_r, rope_cols, dk, k_scale, tm=512, tn=1024):
    m, d = xn.shape
    n = wb.shape[1]
    return pl.pallas_call(
        functools.partial(_proj_big_kernel, n_rope=rope_cols // tn, dk=dk, k_scale=k_scale),
        out_shape=jax.ShapeDtypeStruct((m, n), BF16),
        grid=(n // tn, m // tm),
        in_specs=[pl.BlockSpec((tm, d), lambda j, i: (i, 0)),
                  pl.BlockSpec((d, tn), lambda j, i: (0, j)),
                  pl.BlockSpec((tm, dk // 2), lambda j, i: (i, 0)),
                  pl.BlockSpec((tm, dk // 2), lambda j, i: (i, 0))],
        out_specs=pl.BlockSpec((tm, tn), lambda j, i: (i, j)),
        compiler_params=_params(("parallel", "parallel"), VMEM_LIMIT_BYTES),
        name="proj_big",
    )(xn, wb, cos_r, sin_r)


def _q_proj_kernel(x_ref, w_ref, c_ref, sa_ref, sb_ref, o_ref, *, heads, scale):
    acc = jnp.dot(x_ref[...], w_ref[...], preferred_element_type=F32)
    c = c_ref[...] * scale
    sa = sa_ref[...] * scale
    sb = sb_ref[...] * scale
    for h in range(heads):
        lo = 2 * h * LANES
        o_ref[:, lo:lo + LANES] = (acc[:, lo:lo + LANES] * scale).astype(o_ref.dtype)
        o_ref[:, lo + LANES:lo + 2 * LANES] = _rope_hi(
            acc[:, lo + LANES:lo + 2 * LANES], c, sa, sb).astype(o_ref.dtype)


def q_proj(cqn, wq, c, sa, sb, heads, scale, tm=256):
    m, k = cqn.shape
    n = wq.shape[1]
    row = lambda w: pl.BlockSpec((tm, w), lambda i: (i, 0))
    return pl.pallas_call(
        functools.partial(_q_proj_kernel, heads=heads, scale=scale),
        out_shape=jax.ShapeDtypeStruct((m, n), BF16),
        grid=(m // tm,),
        in_specs=[row(k), pl.BlockSpec((k, n), lambda i: (0, 0)), row(LANES), row(LANES), row(LANES)],
        out_specs=row(n),
        compiler_params=_params(("parallel",), VMEM_LIMIT_BYTES),
        name="q_proj",
    )(cqn, wq, c, sa, sb)


def _kv_expand_kernel(c_ref, kr_ref, wk_ref, wv_ref, k_ref, v_ref, *, heads, transpose_v):
    c = c_ref[...]
    kn = jnp.dot(c, wk_ref[...], preferred_element_type=F32)
    if transpose_v:
        v = lax.dot_general(wv_ref[...], c, (((1,), (1,)), ((), ())), preferred_element_type=F32)
    else:
        v = jnp.dot(c, wv_ref[...], preferred_element_type=F32)
    v_ref[...] = v.astype(v_ref.dtype)
    kr = kr_ref[...]
    for h in range(heads):
        k_ref[:, 2 * h * LANES:(2 * h + 1) * LANES] = kn[:, h * LANES:(h + 1) * LANES].astype(k_ref.dtype)
        k_ref[:, (2 * h + 1) * LANES:(2 * h + 2) * LANES] = kr


def kv_expand(c_bf, kr_pad, wk, wv, rows, heads, transpose_v, tm=256):
    kl = c_bf.shape[1]
    dv = heads * LANES
    row = lambda w: pl.BlockSpec((tm, w), lambda i: (i, 0))
    full = lambda a: pl.BlockSpec(a.shape, lambda i: (0, 0))
    v_shape, v_spec = ((dv, rows), pl.BlockSpec((dv, tm), lambda i: (0, i))) if transpose_v else ((rows, dv), row(dv))
    return pl.pallas_call(
        functools.partial(_kv_expand_kernel, heads=heads, transpose_v=transpose_v),
        out_shape=(jax.ShapeDtypeStruct((rows, 2 * heads * LANES), BF16),
                   jax.ShapeDtypeStruct(v_shape, BF16)),
        grid=(rows // tm,),
        in_specs=[row(kl), row(LANES), full(wk), full(wv)],
        out_specs=(row(2 * heads * LANES), v_spec),
        compiler_params=_params(("parallel",), VMEM_LIMIT_BYTES),
        name="kv_expand",
    )(c_bf, kr_pad, wk, wv)


def _attn_kernel(q_ref, k_ref, v_ref, o_ref, *, tq, tk, n_kv, causal):
    i = pl.program_id(2)
    q = q_ref[...]

    def step(j, carry, masked):
        m, l, acc = carry
        off = pl.multiple_of(j * tk, tk)
        k = k_ref[pl.ds(off, tk), :]
        v = v_ref[pl.ds(off, tk), :]
        s = lax.dot_general(q, k, (((1,), (1,)), ((), ())), preferred_element_type=F32)
        if masked:
            shift = CHUNK.bit_length() - 1
            qc = lax.shift_right_logical(i * tq + lax.broadcasted_iota(jnp.int32, s.shape, 0), shift)
            kc = lax.shift_right_logical(j * tk + lax.broadcasted_iota(jnp.int32, s.shape, 1), shift)
            s = jnp.where(kc <= qc, s, -1e30)
        m_new = jnp.maximum(m, jnp.max(s, axis=-1, keepdims=True))
        alpha = jnp.exp2(m - m_new)
        p = jnp.exp2(s - m_new)
        l = alpha * l + jnp.sum(p, axis=-1, keepdims=True)
        acc = alpha * acc + jnp.dot(p.astype(BF16), v, preferred_element_type=F32)
        return m_new, l, acc

    carry = (jnp.full((tq, 1), -jnp.inf, F32), jnp.zeros((tq, 1), F32),
             jnp.zeros((tq, v_ref.shape[1]), F32))
    if causal:
        carry = lax.fori_loop(0, i, functools.partial(step, masked=False), carry)
        carry = step(i, carry, True)
    else:
        carry = lax.fori_loop(0, n_kv, functools.partial(step, masked=False), carry)
    _, l, acc = carry
    o_ref[...] = (acc / l).astype(o_ref.dtype)


def attention(q, k, v, heads, *, batch, tq_total, q_row0, tk_total, tq, tk, causal):
    assert CHUNK & (CHUNK - 1) == 0 and tq % CHUNK == 0
    if causal:
        assert tq == tk
    nq = tq_total // tq
    qb0 = q_row0 // tq
    dq = 2 * LANES
    return pl.pallas_call(
        functools.partial(_attn_kernel, tq=tq, tk=tk, n_kv=tk_total // tk, causal=causal),
        out_shape=jax.ShapeDtypeStruct((batch * tq_total, heads * LANES), BF16),
        grid=(batch, heads, nq),
        in_specs=[pl.BlockSpec((tq, dq), lambda b, h, i: (qb0 + b * nq + i, h)),
                  pl.BlockSpec((tk_total, dq), lambda b, h, i: (b, h)),
                  pl.BlockSpec((tk_total, LANES), lambda b, h, i: (b, h))],
        out_specs=pl.BlockSpec((tq, LANES), lambda b, h, i: (b * nq + i, h)),
        compiler_params=_params(("parallel", "parallel", "arbitrary"), VMEM_LIMIT_BYTES),
        name="attention",
    )(q, k, v)


def _attn_t_kernel(q_ref, k_ref, vt_ref, o_ref, sa_ref, sb_ref, *, tq, tk):
    i = pl.program_id(1)
    dq, dv = 2 * LANES, LANES
    n_heads = q_ref.shape[1] // dq

    def scores(j, s_ref):
        off = pl.multiple_of(j * tk, tk)
        for h in range(n_heads):
            q = q_ref[:, h * dq:(h + 1) * dq]
            k = k_ref[pl.ds(off, tk), h * dq:(h + 1) * dq]
            s_ref[h] = lax.dot_general(k, q, (((1,), (1,)), ((), ())), preferred_element_type=F32)

    def softmax_pv(j, s_ref, carries, masked):
        off = pl.multiple_of(j * tk, tk)
        out = []
        for h in range(n_heads):
            m, l, acc = carries[h]
            s = s_ref[h]
            if masked:
                shift = CHUNK.bit_length() - 1
                kc = lax.shift_right_logical(j * tk + lax.broadcasted_iota(jnp.int32, s.shape, 0), shift)
                qc = lax.shift_right_logical(i * tq + lax.broadcasted_iota(jnp.int32, s.shape, 1), shift)
                s = jnp.where(kc <= qc, s, -1e30)
            m_new = jnp.maximum(m, jnp.max(s, axis=0, keepdims=True))
            alpha = jnp.exp2(m - m_new)
            p = jnp.exp2(s - m_new)
            l = alpha * l + jnp.sum(p, axis=0, keepdims=True)
            vt = vt_ref[h * dv:(h + 1) * dv, pl.ds(off, tk)]
            acc = alpha * acc + jnp.dot(vt, p.astype(BF16), preferred_element_type=F32)
            out.append((m_new, l, acc))
        return tuple(out)

    def finish(carries):
        for h, (_, l, acc) in enumerate(carries):
            o_ref[:, h * dv:(h + 1) * dv] = (acc / l).T.astype(o_ref.dtype)

    scores(0, sa_ref)

    def pair(t, carries):
        scores(2 * t + 1, sb_ref)
        carries = softmax_pv(2 * t, sa_ref, carries, False)
        scores(2 * t + 2, sa_ref)
        return softmax_pv(2 * t + 1, sb_ref, carries, False)

    init = (jnp.full((1, tq), -jnp.inf, F32), jnp.zeros((1, tq), F32), jnp.zeros((dv, tq), F32))
    carries = lax.fori_loop(0, i // 2, pair, (init,) * n_heads)

    @pl.when(i % 2 == 0)
    def _():
        finish(softmax_pv(i, sa_ref, carries, True))

    @pl.when(i % 2 == 1)
    def _():
        scores(i, sb_ref)
        finish(softmax_pv(i, sb_ref, softmax_pv(i - 1, sa_ref, carries, False), True))


def attention_causal(q, k, vt, heads, *, t, tq, hp=2):
    assert CHUNK & (CHUNK - 1) == 0 and tq % CHUNK == 0 and heads % hp == 0
    dq = 2 * LANES
    return pl.pallas_call(
        functools.partial(_attn_t_kernel, tq=tq, tk=tq),
        out_shape=jax.ShapeDtypeStruct((t, heads * LANES), BF16),
        grid=(heads // hp, t // tq),
        in_specs=[pl.BlockSpec((tq, hp * dq), lambda h, i: (i, h)),
                  pl.BlockSpec((t, hp * dq), lambda h, i: (0, h)),
                  pl.BlockSpec((hp * LANES, t), lambda h, i: (h, 0))],
        out_specs=pl.BlockSpec((tq, hp * LANES), lambda h, i: (i, h)),
        scratch_shapes=[pltpu.VMEM((hp, tq, tq), F32), pltpu.VMEM((hp, tq, tq), F32)],
        compiler_params=_params(("parallel", "arbitrary"), VMEM_LIMIT_BYTES),
        name="attention_causal",
    )(q, k, vt)


def _retention_kernel(q_ref, k_ref, v_ref, s0_ref, d_ref, xi_ref, zeta_ref, gc_ref,
                      o_ref, sout_ref, s_scr, *, n_chunks):
    c = pl.program_id(2)

    @pl.when(c == 0)
    def _():
        s_scr[...] = s0_ref[...]

    q = q_ref[...]
    k = k_ref[...]
    v = v_ref[...]
    s = s_scr[...]
    a = lax.dot_general(q, k, (((1,), (1,)), ((), ())), preferred_element_type=F32) * d_ref[...]
    o = jnp.dot(a.astype(BF16), v, preferred_element_type=F32)
    o = o + jnp.dot(q, s.astype(BF16), preferred_element_type=F32) * xi_ref[...]
    kz = (k.astype(F32) * zeta_ref[...]).astype(BF16)
    s_new = s * gc_ref[...] + lax.dot_general(kz, v, (((0,), (0,)), ((), ())),
                                              preferred_element_type=F32)
    s_scr[...] = s_new
    o_ref[...] = (o * lax.rsqrt(jnp.mean(o * o, axis=-1, keepdims=True) + NORM_EPS)).astype(o_ref.dtype)

    @pl.when(c == n_chunks - 1)
    def _():
        sout_ref[...] = s_new


def retention(big, s0, tabs, heads, dk, *, batch, t_total, row0, chunk):
    d_tab, xi_tab, zeta_tab, gc_tab = tabs
    nc = t_total // chunk
    rb0 = row0 // chunk
    rowblk = lambda col0: pl.BlockSpec((chunk, dk), lambda b, h, c: (rb0 + b * nc + c, col0 + h))
    per_head = lambda a: pl.BlockSpec((None,) + a.shape[1:], lambda b, h, c: (h, 0, 0))
    st_spec = pl.BlockSpec((None, None, dk, dk), lambda b, h, c: (b, h, 0, 0))
    return pl.pallas_call(
        functools.partial(_retention_kernel, n_chunks=nc),
        out_shape=(jax.ShapeDtypeStruct((batch * t_total, heads * dk), BF16),
                   jax.ShapeDtypeStruct((batch, heads, dk, dk), F32)),
        grid=(batch, heads, nc),
        in_specs=[rowblk(0), rowblk(heads), rowblk(2 * heads), st_spec,
                  per_head(d_tab), per_head(xi_tab), per_head(zeta_tab), per_head(gc_tab)],
        out_specs=(pl.BlockSpec((chunk, dk), lambda b, h, c: (b * nc + c, h)), st_spec),
        scratch_shapes=[pltpu.VMEM((dk, dk), F32)],
        compiler_params=_params(("parallel", "parallel", "arbitrary"), VMEM_LIMIT_BYTES),
        name="retention",
    )(big, big, big, s0, d_tab, xi_tab, zeta_tab, gc_tab)


def retention_tables(length, heads, dk):
    lg = jnp.log1p(-jnp.exp2(-5.0 - jnp.arange(heads, dtype=F32)))
    idx = jnp.arange(length, dtype=F32)
    diff = idx[:, None] - idx[None, :]
    decay = jnp.where(diff >= 0, jnp.exp(jnp.maximum(diff, 0.0)[None] * lg[:, None, None]), 0.0)
    xi = jnp.exp((idx + 1.0)[None, :] * lg[:, None])
    zeta = jnp.exp((length - 1.0 - idx)[None, :] * lg[:, None])
    gc = jnp.exp(length * lg)
    bc = lambda a: jnp.broadcast_to(a[:, :, None], (heads, a.shape[1], dk))
    return decay, bc(xi), bc(zeta), jnp.broadcast_to(gc[:, None, None], (heads, 1, dk))


def _mixer_out_kernel(oap_ref, oas_ref, orp_ref, ors_ref, rg_ref, ga_ref, gb_ref, xp_ref, xs_ref, wo_ref,
                      g_ref, wr_ref, br_ref, h_ref, hn_ref, lg_ref, *, prompt_blocks):
    is_prompt = pl.program_id(0) < prompt_blocks
    oa = jnp.where(is_prompt, oap_ref[...], oas_ref[...]).astype(F32)
    orn = jnp.where(is_prompt, orp_ref[...], ors_ref[...]).astype(F32)
    x = jnp.where(is_prompt, xp_ref[...], xs_ref[...])
    rg = rg_ref[...].astype(F32)
    yb = rg * _sigmoid(rg) * orn
    merged = _sigmoid(ga_ref[...].astype(F32)) * oa + _sigmoid(gb_ref[...].astype(F32)) * yb
    h = x + jnp.dot(merged.astype(BF16), wo_ref[...], preferred_element_type=F32)
    h_ref[...] = h
    hn = _rms(h, g_ref[...])
    hn_ref[...] = hn
    lg_ref[...] = jnp.dot(hn, wr_ref[...], preferred_element_type=F32,
                          precision=lax.Precision.HIGHEST) + br_ref[...]


def mixer_out(oa_p, oa_s, or_p, or_s, big, gate_blk0, xp, xs, wo, g, wr, br, tm=256):
    d = xp.shape[1]
    m = xp.shape[0] + xs.shape[0]
    e = wr.shape[1]
    npb, prompt, sample = _group_specs(xp.shape[0], xs.shape[0], tm, d)
    row = lambda col: pl.BlockSpec((tm, d), lambda i: (i, col))
    full = lambda a: pl.BlockSpec(a.shape, lambda i: (0, 0))
    return pl.pallas_call(
        functools.partial(_mixer_out_kernel, prompt_blocks=npb),
        out_shape=(jax.ShapeDtypeStruct((m, d), F32),
                   jax.ShapeDtypeStruct((m, d), F32),
                   jax.ShapeDtypeStruct((m, e), F32)),
        grid=(m // tm,),
        in_specs=[prompt, sample, prompt, sample, row(gate_blk0), row(gate_blk0 + 1), row(gate_blk0 + 2),
                  prompt, sample, full(wo), full(g), full(wr), full(br)],
        out_specs=(row(0), row(0), pl.BlockSpec((tm, e), lambda i: (i, 0))),
        compiler_params=_params(("parallel",), VMEM_LIMIT_BYTES),
        name="mixer_out",
    )(oa_p, oa_s, or_p, or_s, big, big, big, xp, xs, wo, g, wr, br)


def _gather_rows_kernel(tok_ref, src_ref, o_ref, buf, sems, *, rows, n_steps):
    i = pl.program_id(0)

    def copy(r, tok, par):
        return pltpu.make_async_copy(src_ref.at[pl.ds(tok, 1), :], buf.at[par, pl.ds(r, 1), :],
                                     sems.at[par])

    def issue(step, par):
        def body(r, _):
            copy(r, tok_ref[step * rows + r], par).start()
            return 0
        lax.fori_loop(0, rows, body, 0, unroll=8)

    @pl.when(i == 0)
    def _():
        issue(0, 0)

    @pl.when(i + 1 < n_steps)
    def _():
        issue(i + 1, (i + 1) % 2)

    par = i % 2

    def drain(r, _):
        copy(0, 0, par).wait()
        return 0
    lax.fori_loop(0, rows, drain, 0, unroll=8)
    o_ref[...] = buf[par]


def gather_rows(slot_tok, src, rows=MOE_ROWS):
    n = slot_tok.shape[0]
    d = src.shape[1]
    n_steps = n // rows
    return pl.pallas_call(
        functools.partial(_gather_rows_kernel, rows=rows, n_steps=n_steps),
        out_shape=jax.ShapeDtypeStruct((n, d), src.dtype),
        grid_spec=pltpu.PrefetchScalarGridSpec(
            num_scalar_prefetch=1, grid=(n_steps,),
            in_specs=[pl.BlockSpec(memory_space=pl.ANY)],
            out_specs=pl.BlockSpec((rows, d), lambda i, tok: (i, 0)),
            scratch_shapes=[pltpu.VMEM((2, rows, d), src.dtype), pltpu.SemaphoreType.DMA((2,))]),
        compiler_params=_params(("arbitrary",)),
        name="moe_gather",
    )(slot_tok, src)


def _expert_kernel(ie_ref, ib_ref, inb_ref, nv_ref, xs_ref, wg_ref, wu_ref, bg_ref, bu_ref, wd_ref, bd_ref,
                   ye_ref, x32, xb, y32, wgb, wub, wdb, sem_in, sem_out, *, rows, n_f):
    k = pl.program_id(0)
    f = pl.program_id(1)
    nb = inb_ref[k]
    blk0 = ib_ref[k]

    def in_copy(b):
        return pltpu.make_async_copy(xs_ref.at[pl.ds((blk0 + b) * rows, rows), :],
                                     x32.at[pl.ds(b * rows, rows), :], sem_in)

    def out_copy(b):
        return pltpu.make_async_copy(y32.at[pl.ds(b * rows, rows), :],
                                     ye_ref.at[pl.ds((blk0 + b) * rows, rows), :], sem_out)

    def for_blocks(fn):
        def body(b, _):
            fn(b)
            return 0
        lax.fori_loop(0, nb, body, 0)

    @pl.when(f == 0)
    def _():
        for_blocks(lambda b: in_copy(b).start())
        for_blocks(lambda b: in_copy(b).wait())

        def cast(b):
            r = pl.ds(pl.multiple_of(b * rows, rows), rows)
            xb[r, :] = x32[r, :].astype(BF16)
            y32[r, :] = jnp.broadcast_to(bd_ref[...], (rows, y32.shape[1]))
        for_blocks(cast)

    @pl.when(nb > 0)
    def _():
        wgb[...] = wg_ref[...].astype(BF16)
        wub[...] = wu_ref[...].astype(BF16)
        wdb[...] = wd_ref[...].astype(BF16)

    def compute(b):
        r = pl.ds(pl.multiple_of(b * rows, rows), rows)
        x = xb[r, :]
        gate = jnp.minimum(jnp.dot(x, wgb[...], preferred_element_type=F32) + bg_ref[...], SWIGLU_LIMIT)
        up = jnp.clip(jnp.dot(x, wub[...], preferred_element_type=F32) + bu_ref[...],
                      -SWIGLU_LIMIT, SWIGLU_LIMIT)
        h = (up + 1.0) * (gate * _sigmoid(gate * SWIGLU_ALPHA))
        y32[r, :] += jnp.dot(h.astype(BF16), wdb[...], preferred_element_type=F32)
    for_blocks(compute)

    @pl.when(f == n_f - 1)
    def _():
        for_blocks(lambda b: out_copy(b).start())
        for_blocks(lambda b: out_copy(b).wait())


def expert_mlp(items, xs, w_gate_up, b_gate_up, w_down, b_down, n_items):
    item_e, item_b0, item_nb, n_valid = items
    n_exp, d, two_f = w_gate_up.shape
    d_ff = two_f // 2
    tf = MOE_FTILE
    n_f = d_ff // tf
    rows = MOE_ROWS
    rs = rows * MOE_GROUP

    def fi(k, f, nv):
        return jnp.where(k < nv[0], f, n_f - 1)

    in_specs = [
        pl.BlockSpec(memory_space=pl.ANY),
        pl.BlockSpec((None, d, tf), lambda k, f, ie, ib, inb, nv: (ie[k], 0, fi(k, f, nv))),
        pl.BlockSpec((None, d, tf), lambda k, f, ie, ib, inb, nv: (ie[k], 0, n_f + fi(k, f, nv))),
        pl.BlockSpec((None, 1, tf), lambda k, f, ie, ib, inb, nv: (ie[k], 0, fi(k, f, nv))),
        pl.BlockSpec((None, 1, tf), lambda k, f, ie, ib, inb, nv: (ie[k], 0, n_f + fi(k, f, nv))),
        pl.BlockSpec((None, tf, d), lambda k, f, ie, ib, inb, nv: (ie[k], fi(k, f, nv), 0)),
        pl.BlockSpec((None, 1, d), lambda k, f, ie, ib, inb, nv: (ie[k], 0, 0)),
    ]
    return pl.pallas_call(
        functools.partial(_expert_kernel, rows=rows, n_f=n_f),
        out_shape=jax.ShapeDtypeStruct(xs.shape, F32),
        grid_spec=pltpu.PrefetchScalarGridSpec(
            num_scalar_prefetch=4, grid=(n_items, n_f),
            in_specs=in_specs,
            out_specs=pl.BlockSpec(memory_space=pl.ANY),
            scratch_shapes=[pltpu.VMEM((rs, d), F32), pltpu.VMEM((rs, d), BF16), pltpu.VMEM((rs, d), F32),
                            pltpu.VMEM((d, tf), BF16), pltpu.VMEM((d, tf), BF16), pltpu.VMEM((tf, d), BF16),
                            pltpu.SemaphoreType.DMA, pltpu.SemaphoreType.DMA]),
        input_output_aliases={4: 0},
        compiler_params=_params(("arbitrary", "arbitrary"), VMEM_LIMIT_BYTES),
        name="moe_experts",
    )(item_e, item_b0, item_nb, n_valid, xs, w_gate_up, w_gate_up,
      b_gate_up.reshape(n_exp, 1, two_f), b_gate_up.reshape(n_exp, 1, two_f),
      w_down, b_down.reshape(n_exp, 1, d))


def _combine_kernel(pos_ref, ye_ref, w_ref, h_ref, g_ref, o_ref, buf, sems, *, tc, blk0, n_steps, final_norm):
    i = pl.program_id(0)
    n_rows = TOP_K * tc

    def copy(r, src_row, par):
        return pltpu.make_async_copy(ye_ref.at[pl.ds(src_row, 1), :], buf.at[par, pl.ds(r, 1), :],
                                     sems.at[par])

    def issue(step, par):
        def body(r, _):
            copy(r, pos_ref[(blk0 + step) * n_rows + r], par).start()
            return 0
        lax.fori_loop(0, n_rows, body, 0, unroll=8)

    @pl.when(i == 0)
    def _():
        issue(0, 0)

    @pl.when(i + 1 < n_steps)
    def _():
        issue(i + 1, (i + 1) % 2)

    par = i % 2

    def drain(r, _):
        copy(0, 0, par).wait()
        return 0
    lax.fori_loop(0, n_rows, drain, 0, unroll=8)

    w = w_ref[...]
    y = h_ref[...]
    for k in range(TOP_K):
        y = y + w[:, k:k + 1] * buf[par, k * tc:(k + 1) * tc, :]
    o_ref[...] = _rms(y, g_ref[...]) if final_norm else y


def combine(pos, ye, top_w, h, g, row0, rows, final_norm, tc=COMBINE_ROWS):
    d = h.shape[1]
    blk0 = row0 // tc
    n_steps = rows // tc
    return pl.pallas_call(
        functools.partial(_combine_kernel, tc=tc, blk0=blk0, n_steps=n_steps, final_norm=final_norm),
        out_shape=jax.ShapeDtypeStruct((rows, d), F32),
        grid_spec=pltpu.PrefetchScalarGridSpec(
            num_scalar_prefetch=1, grid=(n_steps,),
            in_specs=[pl.BlockSpec(memory_space=pl.ANY),
                      pl.BlockSpec((tc, TOP_K), lambda i, p: (blk0 + i, 0)),
                      pl.BlockSpec((tc, d), lambda i, p: (blk0 + i, 0)),
                      pl.BlockSpec((1, d), lambda i, p: (0, 0))],
            out_specs=pl.BlockSpec((tc, d), lambda i, p: (i, 0)),
            scratch_shapes=[pltpu.VMEM((2, TOP_K * tc, d), F32), pltpu.SemaphoreType.DMA((2,))]),
        compiler_params=_params(("arbitrary",), VMEM_LIMIT_BYTES),
        name="moe_combine",
    )(pos, ye, top_w, h, g)


def route(logits, tc):
    m, n_exp = logits.shape
    a = m * TOP_K
    rows, group = MOE_ROWS, MOE_GROUP
    top_logit, top_idx = lax.top_k(logits, TOP_K)
    top_w = jax.nn.softmax(top_logit, axis=-1)
    flat_e = top_idx.reshape(a)
    onehot = (flat_e[:, None] == jnp.arange(n_exp, dtype=jnp.int32)[None, :]).astype(jnp.int32)
    csum = jnp.cumsum(onehot, axis=0)
    counts = csum[-1]
    rank = jnp.take_along_axis(csum, flat_e[:, None], axis=1)[:, 0] - 1
    nb_e = (counts + rows - 1) // rows
    blk_start = jnp.cumsum(nb_e) - nb_e
    dest = blk_start[flat_e] * rows + rank
    n_blocks = a // rows + n_exp
    slot_tok = jnp.zeros((n_blocks * rows,), jnp.int32).at[dest].set(
        jnp.arange(a, dtype=jnp.int32) // TOP_K)
    ni_e = (nb_e + group - 1) // group
    item_end = jnp.cumsum(ni_e)
    n_valid = item_end[-1]
    n_items = (n_blocks + n_exp * (group - 1)) // group
    kk = jnp.arange(n_items, dtype=jnp.int32)
    item_e = jnp.minimum(jnp.sum(item_end[None, :] <= kk[:, None], axis=1), n_exp - 1).astype(jnp.int32)
    local = kk - (item_end - ni_e)[item_e]
    valid = kk < n_valid
    item_e = jnp.where(valid, item_e, item_e[jnp.maximum(n_valid - 1, 0)])
    item_b0 = jnp.where(valid, blk_start[item_e] + local * group, 0).astype(jnp.int32)
    item_nb = jnp.where(valid, jnp.clip(nb_e[item_e] - local * group, 0, group), 0).astype(jnp.int32)
    pos = dest.reshape(m // tc, tc, TOP_K).transpose(0, 2, 1).reshape(a)
    items = (item_e, item_b0, item_nb, n_valid.reshape(1).astype(jnp.int32))
    return top_w, slot_tok, pos.astype(jnp.int32), items, n_items


def _rope_tables(pos, half):
    inv = jnp.power(ROPE_THETA, -jnp.arange(half, dtype=F32) / half)
    ang = pos.astype(F32)[:, None] * inv[None, :]
    return jnp.cos(ang), jnp.sin(ang)


def kernel(x_prompt, x_sample, cache_latent, cache_k_rope, state_retention, g_mix_norm, w_in, g_q_norm, w_uq, g_kv_norm, w_uk, w_uv, w_o, g_ffn_norm, w_router, b_router, w_gate_up, b_gate_up, w_down, b_down, g_final):
    bp, t, d = x_prompt.shape
    db, ts, _ = x_sample.shape
    depth = w_in.shape[0]
    past = cache_latent.shape[2]
    ql, ha, qk = w_uq.shape[1:]
    kl = w_uk.shape[1]
    rope = cache_k_rope.shape[-1]
    nope = qk - rope
    hr, dk = state_retention.shape[2:4]
    assert bp == 1 and nope == LANES and w_uv.shape[-1] == LANES and 2 * rope == LANES
    mp, ms = bp * t, db * ts
    m = mp + ms
    tk_s = past + ts
    mla_scale = float(qk) ** -0.5 * 1.4426950408889634

    pos = jnp.concatenate([jnp.arange(t, dtype=jnp.int32),
                           jnp.tile(past + jnp.arange(ts, dtype=jnp.int32), db)])
    cos_a, sin_a = _rope_tables(pos, rope // 2)
    z = lambda w: jnp.zeros((m, w), F32)
    rope_c = jnp.concatenate([cos_a, cos_a, z(LANES - rope)], axis=1)
    rope_sa = jnp.concatenate([z(rope // 2), sin_a, z(LANES - rope)], axis=1)
    rope_sb = jnp.concatenate([-sin_a, z(LANES - rope // 2)], axis=1)
    cos_r, sin_r = _rope_tables(pos, dk // 2)
    tabs_p = retention_tables(RET_CHUNK, hr, dk)
    tabs_s = retention_tables(ts, hr, dk)

    xp, xs = x_prompt.reshape(mp, d), x_sample.reshape(ms, d)
    lat_p, kr_p, st_p, lat_s, kr_s, st_s = [], [], [], [], [], []
    small = ql + kl + rope
    for l in range(depth):
        wa = jnp.pad(w_in[l][:, :small], ((0, 0), (0, LANES - rope))).astype(BF16)
        wb = w_in[l][:, small:].astype(BF16)
        wq = jnp.pad(w_uq[l], ((0, 0), (0, 0), (0, 2 * LANES - qk))).reshape(ql, ha * 2 * LANES).astype(BF16)
        wk = w_uk[l].reshape(kl, ha * nope).astype(BF16)
        wv = w_uv[l].reshape(kl, ha * LANES).astype(BF16)
        wo = w_o[l].astype(BF16)

        xn = norm_cast(xp, xs, g_mix_norm[l].reshape(1, d))
        cqn, ckv, ckv_b, kr, kr_pad = proj_small(xn, wa, g_q_norm[l].reshape(1, ql), g_kv_norm[l].reshape(1, kl),
                                                 rope_c, rope_sa, rope_sb, ql, kl, rope)
        big = proj_big(xn, wb, cos_r, sin_r, 2 * hr * dk, dk, float(dk) ** -0.5)
        qcat = q_proj(cqn, wq, rope_c, rope_sa, rope_sb, ha, mla_scale)

        k_p, vt_p = kv_expand(ckv_b, kr_pad, wk, wv.T, mp, ha, True)
        oa_p = attention_causal(qcat, k_p, vt_p, ha, t=mp, tq=512)
        c_all = jnp.concatenate([cache_latent[l], ckv[mp:].reshape(db, ts, kl)], axis=1)
        kr_all = jnp.concatenate([cache_k_rope[l], kr[mp:].reshape(db, ts, rope)], axis=1)
        kr_all = jnp.pad(kr_all, ((0, 0), (0, 0), (0, LANES - rope)))
        k_s, v_s = kv_expand(c_all.reshape(db * tk_s, kl).astype(BF16),
                             kr_all.reshape(db * tk_s, LANES).astype(BF16), wk, wv, db * tk_s, ha, False)
        oa_s = attention(qcat, k_s, v_s, ha, batch=db, tq_total=ts, q_row0=mp, tk_total=tk_s,
                         tq=ts, tk=tk_s, causal=False)

        or_p, s_p = retention(big, jnp.zeros((bp, hr, dk, dk), F32), tabs_p, hr, dk,
                              batch=bp, t_total=t, row0=0, chunk=RET_CHUNK)
        or_s, s_s = retention(big, state_retention[l].astype(F32), tabs_s, hr, dk,
                              batch=db, t_total=ts, row0=mp, chunk=ts)

        h1, hn, logits = mixer_out(oa_p, oa_s, or_p, or_s, big, 3, xp, xs, wo, g_ffn_norm[l].reshape(1, d),
                                   w_router[l], b_router[l].reshape(1, -1))

        top_w, slot_tok, pos_flat, items, n_items = route(logits, COMBINE_ROWS)
        xs = gather_rows(slot_tok, hn)
        ye = expert_mlp(items, xs, w_gate_up[l], b_gate_up[l], w_down[l], b_down[l], n_items)
        last = l == depth - 1
        g_out = g_final.reshape(1, d)
        y_p = combine(pos_flat, ye, top_w, h1, g_out, 0, mp, last)
        y_s = combine(pos_flat, ye, top_w, h1, g_out, mp, ms, last)
        xp, xs = y_p, y_s

        lat_p.append(ckv[:mp].reshape(bp, t, kl))
        kr_p.append(kr[:mp].reshape(bp, t, rope))
        st_p.append(s_p)
        lat_s.append(ckv[mp:].reshape(db, ts, kl))
        kr_s.append(kr[mp:].reshape(db, ts, rope))
        st_s.append(s_s)

    return (y_p.reshape(bp, t, d), y_s.reshape(db, ts, d),
            jnp.stack(lat_p), jnp.stack(kr_p), jnp.stack(st_p),
            jnp.stack(lat_s), jnp.stack(kr_s), jnp.stack(st_s))
```

```python
import functools

import jax
import jax.numpy as jnp
from jax import lax
from jax.experimental import pallas as pl
from jax.experimental.pallas import tpu as pltpu

F32 = jnp.float32
BF16 = jnp.bfloat16

NORM_EPS = 1e-6
ROPE_THETA = 10000.0
CHUNK = 64
TOP_K = 4
SWIGLU_LIMIT = 7.0
SWIGLU_ALPHA = 1.702

LANES = 128
VMEM_LIMIT_BYTES = 56 * 1024 * 1024

RET_CHUNK = 256
MOE_ROWS = 256
MOE_GROUP = 5
MOE_FTILE = 256
COMBINE_ROWS = 128


def _params(sem, vmem=None):
    return pltpu.CompilerParams(dimension_semantics=sem, vmem_limit_bytes=vmem)


def _rms(x, g):
    return x * lax.rsqrt(jnp.mean(x * x, axis=-1, keepdims=True) + NORM_EPS) * g


def _sigmoid(x):
    return 1.0 / (1.0 + jnp.exp(-x))


def _group_specs(rows_p, rows_s, tm, d):
    npb, nsb = rows_p // tm, rows_s // tm
    prompt = pl.BlockSpec((tm, d), lambda i: (jnp.minimum(i, npb - 1), 0))
    sample = pl.BlockSpec((tm, d), lambda i: (jnp.clip(i - npb, 0, nsb - 1), 0))
    return npb, prompt, sample


def _norm_cast_kernel(xp_ref, xs_ref, g_ref, o_ref, *, prompt_blocks):
    x = jnp.where(pl.program_id(0) < prompt_blocks, xp_ref[...], xs_ref[...])
    o_ref[...] = _rms(x, g_ref[...]).astype(o_ref.dtype)


def norm_cast(xp, xs, g, tm=512):
    d = xp.shape[1]
    m = xp.shape[0] + xs.shape[0]
    npb, prompt, sample = _group_specs(xp.shape[0], xs.shape[0], tm, d)
    return pl.pallas_call(
        functools.partial(_norm_cast_kernel, prompt_blocks=npb),
        out_shape=jax.ShapeDtypeStruct((m, d), BF16),
        grid=(m // tm,),
        in_specs=[prompt, sample, pl.BlockSpec((1, d), lambda i: (0, 0))],
        out_specs=pl.BlockSpec((tm, d), lambda i: (i, 0)),
        compiler_params=_params(("parallel",), VMEM_LIMIT_BYTES),
        name="norm_cast",
    )(xp, xs, g)


def _rope_hi(x, c, sa, sb):
    return x * c + pltpu.roll(x, 32, 1) * sa + pltpu.roll(x, 96, 1) * sb


def _proj_small_kernel(x_ref, w_ref, gq_ref, gkv_ref, c_ref, sa_ref, sb_ref,
                       cqn_ref, ckv_ref, ckvb_ref, kr_ref, krp_ref, *, ql, kl, rope):
    acc = jnp.dot(x_ref[...], w_ref[...], preferred_element_type=F32)
    cqn_ref[...] = _rms(acc[:, :ql], gq_ref[...]).astype(cqn_ref.dtype)
    ckv = _rms(acc[:, ql:ql + kl], gkv_ref[...])
    ckv_ref[...] = ckv
    ckvb_ref[...] = ckv.astype(ckvb_ref.dtype)
    kr = _rope_hi(acc[:, ql + kl:ql + kl + LANES], c_ref[...], sa_ref[...], sb_ref[...])
    kr_ref[...] = kr[:, :rope]
    krp_ref[...] = kr.astype(krp_ref.dtype)


def proj_small(xn, wa, gq, gkv, c, sa, sb, ql, kl, rope, tm=256):
    m, d = xn.shape
    n = wa.shape[1]
    row = lambda w: pl.BlockSpec((tm, w), lambda i: (i, 0))
    full = lambda a: pl.BlockSpec(a.shape, lambda i: (0, 0))
    return pl.pallas_call(
        functools.partial(_proj_small_kernel, ql=ql, kl=kl, rope=rope),
        out_shape=(jax.ShapeDtypeStruct((m, ql), BF16),
                   jax.ShapeDtypeStruct((m, kl), F32),
                   jax.ShapeDtypeStruct((m, kl), BF16),
                   jax.ShapeDtypeStruct((m, rope), F32),
                   jax.ShapeDtypeStruct((m, LANES), BF16)),
        grid=(m // tm,),
        in_specs=[row(d), full(wa), full(gq), full(gkv), row(LANES), row(LANES), row(LANES)],
        out_specs=(row(ql), row(kl), row(kl), row(rope), row(LANES)),
        compiler_params=_params(("parallel",), VMEM_LIMIT_BYTES),
        name="proj_small",
    )(xn, wa, gq, gkv, c, sa, sb)


def _proj_big_kernel(x_ref, w_ref, cos_ref, sin_ref, o_ref, *, n_rope, dk, k_scale):
    j = pl.program_id(0)
    acc = jnp.dot(x_ref[...], w_ref[...], preferred_element_type=F32)

    @pl.when(j >= n_rope)
    def _():
        o_ref[...] = acc.astype(o_ref.dtype)

    @pl.when(j < n_rope)
    def _():
        scale = jnp.where(j >= n_rope // 2, k_scale, 1.0).astype(F32)
        c = cos_ref[...] * scale
        s = sin_ref[...] * scale
        half = dk // 2
        for h in range(acc.shape[1] // dk):
            x1 = acc[:, h * dk:h * dk + half]
            x2 = acc[:, h * dk + half:(h + 1) * dk]
            o_ref[:, h * dk:h * dk + half] = (x1 * c - x2 * s).astype(o_ref.dtype)
            o_ref[:, h * dk + half:(h + 1) * dk] = (x2 * c + x1 * s).astype(o_ref.dtype)


def proj_big(xn, wb, cos_r, sin_r, rope_cols, dk, k_scale, tm=512, tn=1024):
    m, d = xn.shape
    n = wb.shape[1]
    return pl.pallas_call(
        functools.partial(_proj_big_kernel, n_rope=rope_cols // tn, dk=dk, k_scale=k_scale),
        out_shape=jax.ShapeDtypeStruct((m, n), BF16),
        grid=(n // tn, m // tm),
        in_specs=[pl.BlockSpec((tm, d), lambda j, i: (i, 0)),
                  pl.BlockSpec((d, tn), lambda j, i: (0, j)),
                  pl.BlockSpec((tm, dk // 2), lambda j, i: (i, 0)),
                  pl.BlockSpec((tm, dk // 2), lambda j, i: (i, 0))],
        out_specs=pl.BlockSpec((tm, tn), lambda j, i: (i, j)),
        compiler_params=_params(("parallel", "parallel"), VMEM_LIMIT_BYTES),
        name="proj_big",
    )(xn, wb, cos_r, sin_r)


def _q_proj_kernel(x_ref, w_ref, c_ref, sa_ref, sb_ref, o_ref, *, heads, scale):
    acc = jnp.dot(x_ref[...], w_ref[...], preferred_element_type=F32)
    c = c_ref[...] * scale
    sa = sa_ref[...] * scale
    sb = sb_ref[...] * scale
    for h in range(heads):
        lo = 2 * h * LANES
        o_ref[:, lo:lo + LANES] = (acc[:, lo:lo + LANES] * scale).astype(o_ref.dtype)
        o_ref[:, lo + LANES:lo + 2 * LANES] = _rope_hi(
            acc[:, lo + LANES:lo + 2 * LANES], c, sa, sb).astype(o_ref.dtype)


def q_proj(cqn, wq, c, sa, sb, heads, scale, tm=256):
    m, k = cqn.shape
    n = wq.shape[1]
    row = lambda w: pl.BlockSpec((tm, w), lambda i: (i, 0))
    return pl.pallas_call(
        functools.partial(_q_proj_kernel, heads=heads, scale=scale),
        out_shape=jax.ShapeDtypeStruct((m, n), BF16),
        grid=(m // tm,),
        in_specs=[row(k), pl.BlockSpec((k, n), lambda i: (0, 0)), row(LANES), row(LANES), row(LANES)],
        out_specs=row(n),
        compiler_params=_params(("parallel",), VMEM_LIMIT_BYTES),
        name="q_proj",
    )(cqn, wq, c, sa, sb)


def _kv_expand_kernel(c_ref, kr_ref, wk_ref, wv_ref, k_ref, v_ref, *, heads, transpose_v):
    c = c_ref[...]
    kn = jnp.dot(c, wk_ref[...], preferred_element_type=F32)
    if transpose_v:
        v = lax.dot_general(wv_ref[...], c, (((1,), (1,)), ((), ())), preferred_element_type=F32)
    else:
        v = jnp.dot(c, wv_ref[...], preferred_element_type=F32)
    v_ref[...] = v.astype(v_ref.dtype)
    kr = kr_ref[...]
    for h in range(heads):
        k_ref[:, 2 * h * LANES:(2 * h + 1) * LANES] = kn[:, h * LANES:(h + 1) * LANES].astype(k_ref.dtype)
        k_ref[:, (2 * h + 1) * LANES:(2 * h + 2) * LANES] = kr


def kv_expand(c_bf, kr_pad, wk, wv, rows, heads, transpose_v, tm=256):
    kl = c_bf.shape[1]
    dv = heads * LANES
    row = lambda w: pl.BlockSpec((tm, w), lambda i: (i, 0))
    full = lambda a: pl.BlockSpec(a.shape, lambda i: (0, 0))
    v_shape, v_spec = ((dv, rows), pl.BlockSpec((dv, tm), lambda i: (0, i))) if transpose_v else ((rows, dv), row(dv))
    return pl.pallas_call(
        functools.partial(_kv_expand_kernel, heads=heads, transpose_v=transpose_v),
        out_shape=(jax.ShapeDtypeStruct((rows, 2 * heads * LANES), BF16),
                   jax.ShapeDtypeStruct(v_shape, BF16)),
        grid=(rows // tm,),
        in_specs=[row(kl), row(LANES), full(wk), full(wv)],
        out_specs=(row(2 * heads * LANES), v_spec),
        compiler_params=_params(("parallel",), VMEM_LIMIT_BYTES),
        name="kv_expand",
    )(c_bf, kr_pad, wk, wv)


def _attn_kernel(q_ref, k_ref, v_ref, o_ref, *, tq, tk, n_kv, causal):
    i = pl.program_id(2)
    q = q_ref[...]

    def step(j, carry, masked):
        m, l, acc = carry
        off = pl.multiple_of(j * tk, tk)
        k = k_ref[pl.ds(off, tk), :]
        v = v_ref[pl.ds(off, tk), :]
        s = lax.dot_general(q, k, (((1,), (1,)), ((), ())), preferred_element_type=F32)
        if masked:
            shift = CHUNK.bit_length() - 1
            qc = lax.shift_right_logical(i * tq + lax.broadcasted_iota(jnp.int32, s.shape, 0), shift)
            kc = lax.shift_right_logical(j * tk + lax.broadcasted_iota(jnp.int32, s.shape, 1), shift)
            s = jnp.where(kc <= qc, s, -1e30)
        m_new = jnp.maximum(m, jnp.max(s, axis=-1, keepdims=True))
        alpha = jnp.exp2(m - m_new)
        p = jnp.exp2(s - m_new)
        l = alpha * l + jnp.sum(p, axis=-1, keepdims=True)
        acc = alpha * acc + jnp.dot(p.astype(BF16), v, preferred_element_type=F32)
        return m_new, l, acc

    carry = (jnp.full((tq, 1), -jnp.inf, F32), jnp.zeros((tq, 1), F32),
             jnp.zeros((tq, v_ref.shape[1]), F32))
    if causal:
        carry = lax.fori_loop(0, i, functools.partial(step, masked=False), carry)
        carry = step(i, carry, True)
    else:
        carry = lax.fori_loop(0, n_kv, functools.partial(step, masked=False), carry)
    _, l, acc = carry
    o_ref[...] = (acc / l).astype(o_ref.dtype)


def attention(q, k, v, heads, *, batch, tq_total, q_row0, tk_total, tq, tk, causal):
    assert CHUNK & (CHUNK - 1) == 0 and tq % CHUNK == 0
    if causal:
        assert tq == tk
    nq = tq_total // tq
    qb0 = q_row0 // tq
    dq = 2 * LANES
    return pl.pallas_call(
        functools.partial(_attn_kernel, tq=tq, tk=tk, n_kv=tk_total // tk, causal=causal),
        out_shape=jax.ShapeDtypeStruct((batch * tq_total, heads * LANES), BF16),
        grid=(batch, heads, nq),
        in_specs=[pl.BlockSpec((tq, dq), lambda b, h, i: (qb0 + b * nq + i, h)),
                  pl.BlockSpec((tk_total, dq), lambda b, h, i: (b, h)),
                  pl.BlockSpec((tk_total, LANES), lambda b, h, i: (b, h))],
        out_specs=pl.BlockSpec((tq, LANES), lambda b, h, i: (b * nq + i, h)),
        compiler_params=_params(("parallel", "parallel", "arbitrary"), VMEM_LIMIT_BYTES),
        name="attention",
    )(q, k, v)


def _attn_t_kernel(q_ref, k_ref, vt_ref, o_ref, sa_ref, sb_ref, *, tq, tk):
    i = pl.program_id(1)
    dq, dv = 2 * LANES, LANES
    n_heads = q_ref.shape[1] // dq

    def scores(j, s_ref):
        off = pl.multiple_of(j * tk, tk)
        for h in range(n_heads):
            q = q_ref[:, h * dq:(h + 1) * dq]
            k = k_ref[pl.ds(off, tk), h * dq:(h + 1) * dq]
            s_ref[h] = lax.dot_general(k, q, (((1,), (1,)), ((), ())), preferred_element_type=F32)

    def softmax_pv(j, s_ref, carries, masked):
        off = pl.multiple_of(j * tk, tk)
        out = []
        for h in range(n_heads):
            m, l, acc = carries[h]
            s = s_ref[h]
            if masked:
                shift = CHUNK.bit_length() - 1
                kc = lax.shift_right_logical(j * tk + lax.broadcasted_iota(jnp.int32, s.shape, 0), shift)
                qc = lax.shift_right_logical(i * tq + lax.broadcasted_iota(jnp.int32, s.shape, 1), shift)
                s = jnp.where(kc <= qc, s, -1e30)
            m_new = jnp.maximum(m, jnp.max(s, axis=0, keepdims=True))
            alpha = jnp.exp2(m - m_new)
            p = jnp.exp2(s - m_new)
            l = alpha * l + jnp.sum(p, axis=0, keepdims=True)
            vt = vt_ref[h * dv:(h + 1) * dv, pl.ds(off, tk)]
            acc = alpha * acc + jnp.dot(vt, p.astype(BF16), preferred_element_type=F32)
            out.append((m_new, l, acc))
        return tuple(out)

    def finish(carries):
        for h, (_, l, acc) in enumerate(carries):
            o_ref[:, h * dv:(h + 1) * dv] = (acc / l).T.astype(o_ref.dtype)

    scores(0, sa_ref)

    def pair(t, carries):
        scores(2 * t + 1, sb_ref)
        carries = softmax_pv(2 * t, sa_ref, carries, False)
        scores(2 * t + 2, sa_ref)
        return softmax_pv(2 * t + 1, sb_ref, carries, False)

    init = (jnp.full((1, tq), -jnp.inf, F32), jnp.zeros((1, tq), F32), jnp.zeros((dv, tq), F32))
    carries = lax.fori_loop(0, i // 2, pair, (init,) * n_heads)

    @pl.when(i % 2 == 0)
    def _():
        finish(softmax_pv(i, sa_ref, carries, True))

    @pl.when(i % 2 == 1)
    def _():
        scores(i, sb_ref)
        finish(softmax_pv(i, sb_ref, softmax_pv(i - 1, sa_ref, carries, False), True))


def attention_causal(q, k, vt, heads, *, t, tq, hp=2):
    assert CHUNK & (CHUNK - 1) == 0 and tq % CHUNK == 0 and heads % hp == 0
    dq = 2 * LANES
    return pl.pallas_call(
        functools.partial(_attn_t_kernel, tq=tq, tk=tq),
        out_shape=jax.ShapeDtypeStruct((t, heads * LANES), BF16),
        grid=(heads // hp, t // tq),
        in_specs=[pl.BlockSpec((tq, hp * dq), lambda h, i: (i, h)),
                  pl.BlockSpec((t, hp * dq), lambda h, i: (0, h)),
                  pl.BlockSpec((hp * LANES, t), lambda h, i: (h, 0))],
        out_specs=pl.BlockSpec((tq, hp * LANES), lambda h, i: (i, h)),
        scratch_shapes=[pltpu.VMEM((hp, tq, tq), F32), pltpu.VMEM((hp, tq, tq), F32)],
        compiler_params=_params(("parallel", "arbitrary"), VMEM_LIMIT_BYTES),
        name="attention_causal",
    )(q, k, vt)


def _retention_kernel(q_ref, k_ref, v_ref, s0_ref, d_ref, xi_ref, zeta_ref, gc_ref,
                      o_ref, sout_ref, s_scr, *, n_chunks):
    c = pl.program_id(2)

    @pl.when(c == 0)
    def _():
        s_scr[...] = s0_ref[...]

    q = q_ref[...]
    k = k_ref[...]
    v = v_ref[...]
    s = s_scr[...]
    a = lax.dot_general(q, k, (((1,), (1,)), ((), ())), preferred_element_type=F32) * d_ref[...]
    o = jnp.dot(a.astype(BF16), v, preferred_element_type=F32)
    o = o + jnp.dot(q, s.astype(BF16), preferred_element_type=F32) * xi_ref[...]
    kz = (k.astype(F32) * zeta_ref[...]).astype(BF16)
    s_new = s * gc_ref[...] + lax.dot_general(kz, v, (((0,), (0,)), ((), ())),
                                              preferred_element_type=F32)
    s_scr[...] = s_new
    o_ref[...] = (o * lax.rsqrt(jnp.mean(o * o, axis=-1, keepdims=True) + NORM_EPS)).astype(o_ref.dtype)

    @pl.when(c == n_chunks - 1)
    def _():
        sout_ref[...] = s_new


def retention(big, s0, tabs, heads, dk, *, batch, t_total, row0, chunk):
    d_tab, xi_tab, zeta_tab, gc_tab = tabs
    nc = t_total // chunk
    rb0 = row0 // chunk
    rowblk = lambda col0: pl.BlockSpec((chunk, dk), lambda b, h, c: (rb0 + b * nc + c, col0 + h))
    per_head = lambda a: pl.BlockSpec((None,) + a.shape[1:], lambda b, h, c: (h, 0, 0))
    st_spec = pl.BlockSpec((None, None, dk, dk), lambda b, h, c: (b, h, 0, 0))
    return pl.pallas_call(
        functools.partial(_retention_kernel, n_chunks=nc),
        out_shape=(jax.ShapeDtypeStruct((batch * t_total, heads * dk), BF16),
                   jax.ShapeDtypeStruct((batch, heads, dk, dk), F32)),
        grid=(batch, heads, nc),
        in_specs=[rowblk(0), rowblk(heads), rowblk(2 * heads), st_spec,
                  per_head(d_tab), per_head(xi_tab), per_head(zeta_tab), per_head(gc_tab)],
        out_specs=(pl.BlockSpec((chunk, dk), lambda b, h, c: (b * nc + c, h)), st_spec),
        scratch_shapes=[pltpu.VMEM((dk, dk), F32)],
        compiler_params=_params(("parallel", "parallel", "arbitrary"), VMEM_LIMIT_BYTES),
        name="retention",
    )(big, big, big, s0, d_tab, xi_tab, zeta_tab, gc_tab)


def retention_tables(length, heads, dk):
    lg = jnp.log1p(-jnp.exp2(-5.0 - jnp.arange(heads, dtype=F32)))
    idx = jnp.arange(length, dtype=F32)
    diff = idx[:, None] - idx[None, :]
    decay = jnp.where(diff >= 0, jnp.exp(jnp.maximum(diff, 0.0)[None] * lg[:, None, None]), 0.0)
    xi = jnp.exp((idx + 1.0)[None, :] * lg[:, None])
    zeta = jnp.exp((length - 1.0 - idx)[None, :] * lg[:, None])
    gc = jnp.exp(length * lg)
    bc = lambda a: jnp.broadcast_to(a[:, :, None], (heads, a.shape[1], dk))
    return decay, bc(xi), bc(zeta), jnp.broadcast_to(gc[:, None, None], (heads, 1, dk))


def _mixer_out_kernel(oap_ref, oas_ref, orp_ref, ors_ref, rg_ref, ga_ref, gb_ref, xp_ref, xs_ref, wo_ref,
                      g_ref, wr_ref, br_ref, h_ref, hn_ref, lg_ref, *, prompt_blocks):
    is_prompt = pl.program_id(0) < prompt_blocks
    oa = jnp.where(is_prompt, oap_ref[...], oas_ref[...]).astype(F32)
    orn = jnp.where(is_prompt, orp_ref[...], ors_ref[...]).astype(F32)
    x = jnp.where(is_prompt, xp_ref[...], xs_ref[...])
    rg = rg_ref[...].astype(F32)
    yb = rg * _sigmoid(rg) * orn
    merged = _sigmoid(ga_ref[...].astype(F32)) * oa + _sigmoid(gb_ref[...].astype(F32)) * yb
    h = x + jnp.dot(merged.astype(BF16), wo_ref[...], preferred_element_type=F32)
    h_ref[...] = h
    hn = _rms(h, g_ref[...])
    hn_ref[...] = hn
    lg_ref[...] = jnp.dot(hn, wr_ref[...], preferred_element_type=F32,
                          precision=lax.Precision.HIGHEST) + br_ref[...]


def mixer_out(oa_p, oa_s, or_p, or_s, big, gate_blk0, xp, xs, wo, g, wr, br, tm=256):
    d = xp.shape[1]
    m = xp.shape[0] + xs.shape[0]
    e = wr.shape[1]
    npb, prompt, sample = _group_specs(xp.shape[0], xs.shape[0], tm, d)
    row = lambda col: pl.BlockSpec((tm, d), lambda i: (i, col))
    full = lambda a: pl.BlockSpec(a.shape, lambda i: (0, 0))
    return pl.pallas_call(
        functools.partial(_mixer_out_kernel, prompt_blocks=npb),
        out_shape=(jax.ShapeDtypeStruct((m, d), F32),
                   jax.ShapeDtypeStruct((m, d), F32),
                   jax.ShapeDtypeStruct((m, e), F32)),
        grid=(m // tm,),
        in_specs=[prompt, sample, prompt, sample, row(gate_blk0), row(gate_blk0 + 1), row(gate_blk0 + 2),
                  prompt, sample, full(wo), full(g), full(wr), full(br)],
        out_specs=(row(0), row(0), pl.BlockSpec((tm, e), lambda i: (i, 0))),
        compiler_params=_params(("parallel",), VMEM_LIMIT_BYTES),
        name="mixer_out",
    )(oa_p, oa_s, or_p, or_s, big, big, big, xp, xs, wo, g, wr, br)


def _expert_kernel(tok_ref, ie_ref, ib_ref, inb_ref, nv_ref, hn_ref, wg_ref, wu_ref, bg_ref, bu_ref, wd_ref,
                   bd_ref, ye_ref, x32, y32, wgb, wub, wdb, sem_x, sem_out, *, rows, group, n_f, n_items,
                   n_blocks):
    k = pl.program_id(0)
    f = pl.program_id(1)
    nb = inb_ref[k]
    blk0 = ib_ref[k]
    n_valid = nv_ref[0]
    valid = k < n_valid
    has_next = k + 1 < n_valid
    last_f = f == n_f - 1
    cur = k % 2
    rs = rows * group
    d = y32.shape[1]
    per_block = rs // (n_f * group)

    def gather_copy(slot, buf, row):
        return pltpu.make_async_copy(hn_ref.at[pl.ds(tok_ref[slot], 1), :],
                                     x32.at[buf, pl.ds(row, 1), :], sem_x.at[buf])

    def out_copy(b):
        return pltpu.make_async_copy(y32.at[pl.ds(b * rows, rows), :],
                                     ye_ref.at[pl.ds((blk0 + b) * rows, rows), :], sem_out)

    @pl.when((k == 0) & (f == 0))
    def _():
        def body(r, _):
            gather_copy(blk0 * rows + r, 0, r).start()
            return 0
        lax.fori_loop(0, rs, body, 0, unroll=8)

    @pl.when(valid & (f == 0))
    def _():
        def drain(r, _):
            gather_copy(0, cur, 0).wait()
            return 0
        lax.fori_loop(0, rs, drain, 0, unroll=8)

        def init(b, _):
            y32[pl.ds(pl.multiple_of(b * rows, rows), rows), :] = jnp.broadcast_to(bd_ref[...], (rows, d))
            return 0
        lax.fori_loop(0, nb, init, 0)

    @pl.when(valid)
    def _():
        wgb[...] = wg_ref[...].astype(BF16)
        wub[...] = wu_ref[...].astype(BF16)
        wdb[...] = wd_ref[...].astype(BF16)

    def prefetch(b, next_slot0):
        row0 = (f * group + b) * per_block
        for r in range(per_block):
            gather_copy(next_slot0 + row0 + r, 1 - cur, row0 + r).start()

    def compute(b, next_slot0):
        r = pl.ds(pl.multiple_of(b * rows, rows), rows)
        x = x32[cur, r, :].astype(BF16)
        gate = jnp.minimum(jnp.dot(x, wgb[...], preferred_element_type=F32) + bg_ref[...], SWIGLU_LIMIT)
        up = jnp.clip(jnp.dot(x, wub[...], preferred_element_type=F32) + bu_ref[...],
                      -SWIGLU_LIMIT, SWIGLU_LIMIT)
        h = (up + 1.0) * (gate * _sigmoid(gate * SWIGLU_ALPHA))
        y32[r, :] += jnp.dot(h.astype(BF16), wdb[...], preferred_element_type=F32)
        if next_slot0 is not None:
            prefetch(b, next_slot0)

        @pl.when(last_f)
        def _():
            out_copy(b).start()

    @pl.when(has_next)
    def _():
        next_slot0 = ib_ref[k + 1] * rows

        def body(b, _):
            compute(b, next_slot0)
            return 0
        lax.fori_loop(0, nb, body, 0)

        def rest(b, _):
            prefetch(b, next_slot0)
            return 0
        lax.fori_loop(nb, group, rest, 0)

    @pl.when(valid & jnp.logical_not(has_next))
    def _():
        def body(b, _):
            compute(b, None)
            return 0
        lax.fori_loop(0, nb, body, 0)

    @pl.when(valid & last_f)
    def _():
        def body(b, _):
            out_copy(b).wait()
            return 0
        lax.fori_loop(0, nb, body, 0)

    @pl.when((k == n_items - 1) & last_f)
    def _():
        used = nv_ref[1]
        y32[0:rows, :] = jnp.zeros((rows, d), F32)

        def fill(b):
            return pltpu.make_async_copy(y32.at[pl.ds(0, rows), :], ye_ref.at[pl.ds(b * rows, rows), :], sem_out)

        def start(b, _):
            fill(b).start()
            return 0

        def wait(b, _):
            fill(b).wait()
            return 0
        lax.fori_loop(used, n_blocks, start, 0)
        lax.fori_loop(used, n_blocks, wait, 0)


def expert_mlp(slot_tok, items, hn, w_gate_up, b_gate_up, w_down, b_down, n_items, n_blocks):
    item_e, item_b0, item_nb, counts = items
    n_exp, d, two_f = w_gate_up.shape
    d_ff = two_f // 2
    tf = MOE_FTILE
    n_f = d_ff // tf
    rows, group = MOE_ROWS, MOE_GROUP
    rs = rows * group
    assert rs % (n_f * group) == 0

    def fi(k, f, nv):
        return jnp.where(k < nv[0], f, n_f - 1)

    in_specs = [
        pl.BlockSpec(memory_space=pl.ANY),
        pl.BlockSpec((None, d, tf), lambda k, f, tok, ie, ib, inb, nv: (ie[k], 0, fi(k, f, nv))),
        pl.BlockSpec((None, d, tf), lambda k, f, tok, ie, ib, inb, nv: (ie[k], 0, n_f + fi(k, f, nv))),
        pl.BlockSpec((None, 1, tf), lambda k, f, tok, ie, ib, inb, nv: (ie[k], 0, fi(k, f, nv))),
        pl.BlockSpec((None, 1, tf), lambda k, f, tok, ie, ib, inb, nv: (ie[k], 0, n_f + fi(k, f, nv))),
        pl.BlockSpec((None, tf, d), lambda k, f, tok, ie, ib, inb, nv: (ie[k], fi(k, f, nv), 0)),
        pl.BlockSpec((None, 1, d), lambda k, f, tok, ie, ib, inb, nv: (ie[k], 0, 0)),
    ]
    return pl.pallas_call(
        functools.partial(_expert_kernel, rows=rows, group=group, n_f=n_f, n_items=n_items, n_blocks=n_blocks),
        out_shape=jax.ShapeDtypeStruct((n_blocks * rows, d), F32),
        grid_spec=pltpu.PrefetchScalarGridSpec(
            num_scalar_prefetch=5, grid=(n_items, n_f),
            in_specs=in_specs,
            out_specs=pl.BlockSpec(memory_space=pl.ANY),
            scratch_shapes=[pltpu.VMEM((2, rs, d), F32), pltpu.VMEM((rs, d), F32),
                            pltpu.VMEM((d, tf), BF16), pltpu.VMEM((d, tf), BF16), pltpu.VMEM((tf, d), BF16),
                            pltpu.SemaphoreType.DMA((2,)), pltpu.SemaphoreType.DMA]),
        compiler_params=_params(("arbitrary", "arbitrary"), VMEM_LIMIT_BYTES),
        name="moe_experts",
    )(slot_tok, item_e, item_b0, item_nb, counts, hn, w_gate_up, w_gate_up,
      b_gate_up.reshape(n_exp, 1, two_f), b_gate_up.reshape(n_exp, 1, two_f),
      w_down, b_down.reshape(n_exp, 1, d))


def _combine_kernel(pos_ref, ye_ref, w_ref, h_ref, g_ref, o_ref, buf, sems, *, tc, blk0, n_steps, final_norm):
    i = pl.program_id(0)
    n_rows = TOP_K * tc

    def copy(r, src_row, par):
        return pltpu.make_async_copy(ye_ref.at[pl.ds(src_row, 1), :], buf.at[par, pl.ds(r, 1), :],
                                     sems.at[par])

    def issue(step, par):
        def body(r, _):
            copy(r, pos_ref[(blk0 + step) * n_rows + r], par).start()
            return 0
        lax.fori_loop(0, n_rows, body, 0, unroll=8)

    @pl.when(i == 0)
    def _():
        issue(0, 0)

    @pl.when(i + 1 < n_steps)
    def _():
        issue(i + 1, (i + 1) % 2)

    par = i % 2

    def drain(r, _):
        copy(0, 0, par).wait()
        return 0
    lax.fori_loop(0, n_rows, drain, 0, unroll=8)

    w = w_ref[...]
    y = h_ref[...]
    for k in range(TOP_K):
        y = y + w[:, k:k + 1] * buf[par, k * tc:(k + 1) * tc, :]
    o_ref[...] = _rms(y, g_ref[...]) if final_norm else y


def combine(pos, ye, top_w, h, g, row0, rows, final_norm, tc=COMBINE_ROWS):
    d = h.shape[1]
    blk0 = row0 // tc
    n_steps = rows // tc
    return pl.pallas_call(
        functools.partial(_combine_kernel, tc=tc, blk0=blk0, n_steps=n_steps, final_norm=final_norm),
        out_shape=jax.ShapeDtypeStruct((rows, d), F32),
        grid_spec=pltpu.PrefetchScalarGridSpec(
            num_scalar_prefetch=1, grid=(n_steps,),
            in_specs=[pl.BlockSpec(memory_space=pl.ANY),
                      pl.BlockSpec((tc, TOP_K), lambda i, p: (blk0 + i, 0)),
                      pl.BlockSpec((tc, d), lambda i, p: (blk0 + i, 0)),
                      pl.BlockSpec((1, d), lambda i, p: (0, 0))],
            out_specs=pl.BlockSpec((tc, d), lambda i, p: (i, 0)),
            scratch_shapes=[pltpu.VMEM((2, TOP_K * tc, d), F32), pltpu.SemaphoreType.DMA((2,))]),
        compiler_params=_params(("arbitrary",), VMEM_LIMIT_BYTES),
        name="moe_combine",
    )(pos, ye, top_w, h, g)


def route(logits, tc):
    m, n_exp = logits.shape
    a = m * TOP_K
    rows, group = MOE_ROWS, MOE_GROUP
    top_logit, top_idx = lax.top_k(logits, TOP_K)
    top_w = jax.nn.softmax(top_logit, axis=-1)
    flat_e = top_idx.reshape(a)
    onehot = (flat_e[:, None] == jnp.arange(n_exp, dtype=jnp.int32)[None, :]).astype(jnp.int32)
    csum = jnp.cumsum(onehot, axis=0)
    counts = csum[-1]
    rank = jnp.take_along_axis(csum, flat_e[:, None], axis=1)[:, 0] - 1
    nb_e = (counts + rows - 1) // rows
    blk_start = jnp.cumsum(nb_e) - nb_e
    dest = blk_start[flat_e] * rows + rank
    n_blocks = a // rows + n_exp
    slot_tok = jnp.zeros(((n_blocks + group) * rows,), jnp.int32).at[dest].set(
        jnp.arange(a, dtype=jnp.int32) // TOP_K)
    ni_e = (nb_e + group - 1) // group
    item_end = jnp.cumsum(ni_e)
    n_valid = item_end[-1]
    n_items = (n_blocks + n_exp * (group - 1)) // group
    kk = jnp.arange(n_items, dtype=jnp.int32)
    item_e = jnp.minimum(jnp.sum(item_end[None, :] <= kk[:, None], axis=1), n_exp - 1).astype(jnp.int32)
    local = kk - (item_end - ni_e)[item_e]
    valid = kk < n_valid
    item_e = jnp.where(valid, item_e, item_e[jnp.maximum(n_valid - 1, 0)])
    item_b0 = jnp.where(valid, blk_start[item_e] + local * group, 0).astype(jnp.int32)
    item_nb = jnp.where(valid, jnp.clip(nb_e[item_e] - local * group, 0, group), 0).astype(jnp.int32)
    pos = dest.reshape(m // tc, tc, TOP_K).transpose(0, 2, 1).reshape(a)
    counts = jnp.stack([n_valid, jnp.sum(nb_e)]).astype(jnp.int32)
    items = (item_e, item_b0, item_nb, counts)
    return top_w, slot_tok, pos.astype(jnp.int32), items, n_items, n_blocks


def _rope_tables(pos, half):
    inv = jnp.power(ROPE_THETA, -jnp.arange(half, dtype=F32) / half)
    ang = pos.astype(F32)[:, None] * inv[None, :]
    return jnp.cos(ang), jnp.sin(ang)


def kernel(x_prompt, x_sample, cache_latent, cache_k_rope, state_retention, g_mix_norm, w_in, g_q_norm, w_uq, g_kv_norm, w_uk, w_uv, w_o, g_ffn_norm, w_router, b_router, w_gate_up, b_gate_up, w_down, b_down, g_final):
    bp, t, d = x_prompt.shape
    db, ts, _ = x_sample.shape
    depth = w_in.shape[0]
    past = cache_latent.shape[2]
    ql, ha, qk = w_uq.shape[1:]
    kl = w_uk.shape[1]
    rope = cache_k_rope.shape[-1]
    nope = qk - rope
    hr, dk = state_retention.shape[2:4]
    assert bp == 1 and nope == LANES and w_uv.shape[-1] == LANES and 2 * rope == LANES
    mp, ms = bp * t, db * ts
    m = mp + ms
    tk_s = past + ts
    mla_scale = float(qk) ** -0.5 * 1.4426950408889634

    pos = jnp.concatenate([jnp.arange(t, dtype=jnp.int32),
                           jnp.tile(past + jnp.arange(ts, dtype=jnp.int32), db)])
    cos_a, sin_a = _rope_tables(pos, rope // 2)
    z = lambda w: jnp.zeros((m, w), F32)
    rope_c = jnp.concatenate([cos_a, cos_a, z(LANES - rope)], axis=1)
    rope_sa = jnp.concatenate([z(rope // 2), sin_a, z(LANES - rope)], axis=1)
    rope_sb = jnp.concatenate([-sin_a, z(LANES - rope // 2)], axis=1)
    cos_r, sin_r = _rope_tables(pos, dk // 2)
    tabs_p = retention_tables(RET_CHUNK, hr, dk)
    tabs_s = retention_tables(ts, hr, dk)

    xp, xs = x_prompt.reshape(mp, d), x_sample.reshape(ms, d)
    lat_p, kr_p, st_p, lat_s, kr_s, st_s = [], [], [], [], [], []
    small = ql + kl + rope
    for l in range(depth):
        wa = jnp.pad(w_in[l][:, :small], ((0, 0), (0, LANES - rope))).astype(BF16)
        wb = w_in[l][:, small:].astype(BF16)
        wq = jnp.pad(w_uq[l], ((0, 0), (0, 0), (0, 2 * LANES - qk))).reshape(ql, ha * 2 * LANES).astype(BF16)
        wk = w_uk[l].reshape(kl, ha * nope).astype(BF16)
        wv = w_uv[l].reshape(kl, ha * LANES).astype(BF16)
        wo = w_o[l].astype(BF16)

        xn = norm_cast(xp, xs, g_mix_norm[l].reshape(1, d))
        cqn, ckv, ckv_b, kr, kr_pad = proj_small(xn, wa, g_q_norm[l].reshape(1, ql), g_kv_norm[l].reshape(1, kl),
                                                 rope_c, rope_sa, rope_sb, ql, kl, rope)
        big = proj_big(xn, wb, cos_r, sin_r, 2 * hr * dk, dk, float(dk) ** -0.5)
        qcat = q_proj(cqn, wq, rope_c, rope_sa, rope_sb, ha, mla_scale)

        k_p, vt_p = kv_expand(ckv_b, kr_pad, wk, wv.T, mp, ha, True)
        oa_p = attention_causal(qcat, k_p, vt_p, ha, t=mp, tq=512)
        c_all = jnp.concatenate([cache_latent[l], ckv[mp:].reshape(db, ts, kl)], axis=1)
        kr_all = jnp.concatenate([cache_k_rope[l], kr[mp:].reshape(db, ts, rope)], axis=1)
        kr_all = jnp.pad(kr_all, ((0, 0), (0, 0), (0, LANES - rope)))
        k_s, v_s = kv_expand(c_all.reshape(db * tk_s, kl).astype(BF16),
                             kr_all.reshape(db * tk_s, LANES).astype(BF16), wk, wv, db * tk_s, ha, False)
        oa_s = attention(qcat, k_s, v_s, ha, batch=db, tq_total=ts, q_row0=mp, tk_total=tk_s,
                         tq=ts, tk=tk_s, causal=False)

        or_p, s_p = retention(big, jnp.zeros((bp, hr, dk, dk), F32), tabs_p, hr, dk,
                              batch=bp, t_total=t, row0=0, chunk=RET_CHUNK)
        or_s, s_s = retention(big, state_retention[l].astype(F32), tabs_s, hr, dk,
                              batch=db, t_total=ts, row0=mp, chunk=ts)

        h1, hn, logits = mixer_out(oa_p, oa_s, or_p, or_s, big, 3, xp, xs, wo, g_ffn_norm[l].reshape(1, d),
                                   w_router[l], b_router[l].reshape(1, -1))

        top_w, slot_tok, pos_flat, items, n_items, n_blocks = route(logits, COMBINE_ROWS)
        ye = expert_mlp(slot_tok, items, hn, w_gate_up[l], b_gate_up[l], w_down[l], b_down[l], n_items, n_blocks)
        last = l == depth - 1
        g_out = g_final.reshape(1, d)
        y_p = combine(pos_flat, ye, top_w, h1, g_out, 0, mp, last)
        y_s = combine(pos_flat, ye, top_w, h1, g_out, mp, ms, last)
        xp, xs = y_p, y_s

        lat_p.append(ckv[:mp].reshape(bp, t, kl))
        kr_p.append(kr[:mp].reshape(bp, t, rope))
        st_p.append(s_p)
        lat_s.append(ckv[mp:].reshape(db, ts, kl))
        kr_s.append(kr[mp:].reshape(db, ts, rope))
        st_s.append(s_s)

    return (y_p.reshape(bp, t, d), y_s.reshape(db, ts, d),
            jnp.stack(lat_p), jnp.stack(kr_p), jnp.stack(st_p),
            jnp.stack(lat_s), jnp.stack(kr_s), jnp.stack(st_s))
```

```python
import functools

import jax
import jax.numpy as jnp
from jax import lax
from jax.experimental import pallas as pl
from jax.experimental.pallas import tpu as pltpu

F32 = jnp.float32
BF16 = jnp.bfloat16

NORM_EPS = 1e-6
ROPE_THETA = 10000.0
CHUNK = 64
TOP_K = 4
SWIGLU_LIMIT = 7.0
SWIGLU_ALPHA = 1.702

LANES = 128
VMEM_LIMIT_BYTES = 56 * 1024 * 1024

RET_CHUNK = 256
MOE_ROWS = 256
MOE_GROUP = 5
MOE_FTILE = 256
COMBINE_ROWS = 128


def _params(sem, vmem=None):
    return pltpu.CompilerParams(dimension_semantics=sem, vmem_limit_bytes=vmem)


def _rms(x, g):
    return x * lax.rsqrt(jnp.mean(x * x, axis=-1, keepdims=True) + NORM_EPS) * g


def _sigmoid(x):
    return 1.0 / (1.0 + jnp.exp(-x))


def _group_specs(rows_p, rows_s, tm, d):
    npb, nsb = rows_p // tm, rows_s // tm
    prompt = pl.BlockSpec((tm, d), lambda i: (jnp.minimum(i, npb - 1), 0))
    sample = pl.BlockSpec((tm, d), lambda i: (jnp.clip(i - npb, 0, nsb - 1), 0))
    return npb, prompt, sample


def _norm_cast_kernel(xp_ref, xs_ref, g_ref, o_ref, *, prompt_blocks):
    x = jnp.where(pl.program_id(0) < prompt_blocks, xp_ref[...], xs_ref[...])
    o_ref[...] = _rms(x, g_ref[...]).astype(o_ref.dtype)


def norm_cast(xp, xs, g, tm=512):
    d = xp.shape[1]
    m = xp.shape[0] + xs.shape[0]
    npb, prompt, sample = _group_specs(xp.shape[0], xs.shape[0], tm, d)
    return pl.pallas_call(
        functools.partial(_norm_cast_kernel, prompt_blocks=npb),
        out_shape=jax.ShapeDtypeStruct((m, d), BF16),
        grid=(m // tm,),
        in_specs=[prompt, sample, pl.BlockSpec((1, d), lambda i: (0, 0))],
        out_specs=pl.BlockSpec((tm, d), lambda i: (i, 0)),
        compiler_params=_params(("parallel",), VMEM_LIMIT_BYTES),
        name="norm_cast",
    )(xp, xs, g)


def _rope_hi(x, c, sa, sb):
    return x * c + pltpu.roll(x, 32, 1) * sa + pltpu.roll(x, 96, 1) * sb


def _proj_small_kernel(x_ref, w_ref, gq_ref, gkv_ref, c_ref, sa_ref, sb_ref,
                       cqn_ref, ckv_ref, ckvb_ref, kr_ref, krp_ref, *, ql, kl, rope):
    acc = jnp.dot(x_ref[...], w_ref[...], preferred_element_type=F32)
    cqn_ref[...] = _rms(acc[:, :ql], gq_ref[...]).astype(cqn_ref.dtype)
    ckv = _rms(acc[:, ql:ql + kl], gkv_ref[...])
    ckv_ref[...] = ckv
    ckvb_ref[...] = ckv.astype(ckvb_ref.dtype)
    kr = _rope_hi(acc[:, ql + kl:ql + kl + LANES], c_ref[...], sa_ref[...], sb_ref[...])
    kr_ref[...] = kr[:, :rope]
    krp_ref[...] = kr.astype(krp_ref.dtype)


def proj_small(xn, wa, gq, gkv, c, sa, sb, ql, kl, rope, tm=256):
    m, d = xn.shape
    n = wa.shape[1]
    row = lambda w: pl.BlockSpec((tm, w), lambda i: (i, 0))
    full = lambda a: pl.BlockSpec(a.shape, lambda i: (0, 0))
    return pl.pallas_call(
        functools.partial(_proj_small_kernel, ql=ql, kl=kl, rope=rope),
        out_shape=(jax.ShapeDtypeStruct((m, ql), BF16),
                   jax.ShapeDtypeStruct((m, kl), F32),
                   jax.ShapeDtypeStruct((m, kl), BF16),
                   jax.ShapeDtypeStruct((m, rope), F32),
                   jax.ShapeDtypeStruct((m, LANES), BF16)),
        grid=(m // tm,),
        in_specs=[row(d), full(wa), full(gq), full(gkv), row(LANES), row(LANES), row(LANES)],
        out_specs=(row(ql), row(kl), row(kl), row(rope), row(LANES)),
        compiler_params=_params(("parallel",), VMEM_LIMIT_BYTES),
        name="proj_small",
    )(xn, wa, gq, gkv, c, sa, sb)


def _proj_big_kernel(x_ref, w_ref, cos_ref, sin_ref, o_ref, *, n_rope, dk, k_scale):
    j = pl.program_id(0)
    acc = jnp.dot(x_ref[...], w_ref[...], preferred_element_type=F32)

    @pl.when(j >= n_rope)
    def _():
        o_ref[...] = acc.astype(o_ref.dtype)

    @pl.when(j < n_rope)
    def _():
        scale = jnp.where(j >= n_rope // 2, k_scale, 1.0).astype(F32)
        c = cos_ref[...] * scale
        s = sin_ref[...] * scale
        half = dk // 2
        for h in range(acc.shape[1] // dk):
            x1 = acc[:, h * dk:h * dk + half]
            x2 = acc[:, h * dk + half:(h + 1) * dk]
            o_ref[:, h * dk:h * dk + half] = (x1 * c - x2 * s).astype(o_ref.dtype)
            o_ref[:, h * dk + half:(h + 1) * dk] = (x2 * c + x1 * s).astype(o_ref.dtype)


def proj_big(xn, wb, cos_r, sin_r, rope_cols, dk, k_scale, tm=512, tn=1024):
    m, d = xn.shape
    n = wb.shape[1]
    return pl.pallas_call(
        functools.partial(_proj_big_kernel, n_rope=rope_cols // tn, dk=dk, k_scale=k_scale),
        out_shape=jax.ShapeDtypeStruct((m, n), BF16),
        grid=(n // tn, m // tm),
        in_specs=[pl.BlockSpec((tm, d), lambda j, i: (i, 0)),
                  pl.BlockSpec((d, tn), lambda j, i: (0, j)),
                  pl.BlockSpec((tm, dk // 2), lambda j, i: (i, 0)),
                  pl.BlockSpec((tm, dk // 2), lambda j, i: (i, 0))],
        out_specs=pl.BlockSpec((tm, tn), lambda j, i: (i, j)),
        compiler_params=_params(("parallel", "parallel"), VMEM_LIMIT_BYTES),
        name="proj_big",
    )(xn, wb, cos_r, sin_r)


def _q_proj_kernel(x_ref, w_ref, c_ref, sa_ref, sb_ref, o_ref, *, heads, scale):
    acc = jnp.dot(x_ref[...], w_ref[...], preferred_element_type=F32)
    c = c_ref[...] * scale
    sa = sa_ref[...] * scale
    sb = sb_ref[...] * scale
    for h in range(heads):
        lo = 2 * h * LANES
        o_ref[:, lo:lo + LANES] = (acc[:, lo:lo + LANES] * scale).astype(o_ref.dtype)
        o_ref[:, lo + LANES:lo + 2 * LANES] = _rope_hi(
            acc[:, lo + LANES:lo + 2 * LANES], c, sa, sb).astype(o_ref.dtype)


def q_proj(cqn, wq, c, sa, sb, heads, scale, tm=256):
    m, k = cqn.shape
    n = wq.shape[1]
    row = lambda w: pl.BlockSpec((tm, w), lambda i: (i, 0))
    return pl.pallas_call(
        functools.partial(_q_proj_kernel, heads=heads, scale=scale),
        out_shape=jax.ShapeDtypeStruct((m, n), BF16),
        grid=(m // tm,),
        in_specs=[row(k), pl.BlockSpec((k, n), lambda i: (0, 0)), row(LANES), row(LANES), row(LANES)],
        out_specs=row(n),
        compiler_params=_params(("parallel",), VMEM_LIMIT_BYTES),
        name="q_proj",
    )(cqn, wq, c, sa, sb)


def _kv_expand_kernel(c_ref, kr_ref, wk_ref, wv_ref, k_ref, v_ref, *, heads, transpose_v):
    c = c_ref[...]
    kn = jnp.dot(c, wk_ref[...], preferred_element_type=F32)
    if transpose_v:
        v = lax.dot_general(wv_ref[...], c, (((1,), (1,)), ((), ())), preferred_element_type=F32)
    else:
        v = jnp.dot(c, wv_ref[...], preferred_element_type=F32)
    v_ref[...] = v.astype(v_ref.dtype)
    kr = kr_ref[...]
    for h in range(heads):
        k_ref[:, 2 * h * LANES:(2 * h + 1) * LANES] = kn[:, h * LANES:(h + 1) * LANES].astype(k_ref.dtype)
        k_ref[:, (2 * h + 1) * LANES:(2 * h + 2) * LANES] = kr


def kv_expand(c_bf, kr_pad, wk, wv, rows, heads, transpose_v, tm=256):
    kl = c_bf.shape[1]
    dv = heads * LANES
    row = lambda w: pl.BlockSpec((tm, w), lambda i: (i, 0))
    full = lambda a: pl.BlockSpec(a.shape, lambda i: (0, 0))
    v_shape, v_spec = ((dv, rows), pl.BlockSpec((dv, tm), lambda i: (0, i))) if transpose_v else ((rows, dv), row(dv))
    return pl.pallas_call(
        functools.partial(_kv_expand_kernel, heads=heads, transpose_v=transpose_v),
        out_shape=(jax.ShapeDtypeStruct((rows, 2 * heads * LANES), BF16),
                   jax.ShapeDtypeStruct(v_shape, BF16)),
        grid=(rows // tm,),
        in_specs=[row(kl), row(LANES), full(wk), full(wv)],
        out_specs=(row(2 * heads * LANES), v_spec),
        compiler_params=_params(("parallel",), VMEM_LIMIT_BYTES),
        name="kv_expand",
    )(c_bf, kr_pad, wk, wv)


def _attn_kernel(q_ref, k_ref, v_ref, o_ref, *, tq, tk, n_kv, causal):
    i = pl.program_id(2)
    q = q_ref[...]

    def step(j, carry, masked):
        m, l, acc = carry
        off = pl.multiple_of(j * tk, tk)
        k = k_ref[pl.ds(off, tk), :]
        v = v_ref[pl.ds(off, tk), :]
        s = lax.dot_general(q, k, (((1,), (1,)), ((), ())), preferred_element_type=F32)
        if masked:
            shift = CHUNK.bit_length() - 1
            qc = lax.shift_right_logical(i * tq + lax.broadcasted_iota(jnp.int32, s.shape, 0), shift)
            kc = lax.shift_right_logical(j * tk + lax.broadcasted_iota(jnp.int32, s.shape, 1), shift)
            s = jnp.where(kc <= qc, s, -1e30)
        m_new = jnp.maximum(m, jnp.max(s, axis=-1, keepdims=True))
        alpha = jnp.exp2(m - m_new)
        p = jnp.exp2(s - m_new)
        l = alpha * l + jnp.sum(p, axis=-1, keepdims=True)
        acc = alpha * acc + jnp.dot(p.astype(BF16), v, preferred_element_type=F32)
        return m_new, l, acc

    carry = (jnp.full((tq, 1), -jnp.inf, F32), jnp.zeros((tq, 1), F32),
             jnp.zeros((tq, v_ref.shape[1]), F32))
    if causal:
        carry = lax.fori_loop(0, i, functools.partial(step, masked=False), carry)
        carry = step(i, carry, True)
    else:
        carry = lax.fori_loop(0, n_kv, functools.partial(step, masked=False), carry)
    _, l, acc = carry
    o_ref[...] = (acc / l).astype(o_ref.dtype)


def attention(q, k, v, heads, *, batch, tq_total, q_row0, tk_total, tq, tk, causal):
    assert CHUNK & (CHUNK - 1) == 0 and tq % CHUNK == 0
    if causal:
        assert tq == tk
    nq = tq_total // tq
    qb0 = q_row0 // tq
    dq = 2 * LANES
    return pl.pallas_call(
        functools.partial(_attn_kernel, tq=tq, tk=tk, n_kv=tk_total // tk, causal=causal),
        out_shape=jax.ShapeDtypeStruct((batch * tq_total, heads * LANES), BF16),
        grid=(batch, heads, nq),
        in_specs=[pl.BlockSpec((tq, dq), lambda b, h, i: (qb0 + b * nq + i, h)),
                  pl.BlockSpec((tk_total, dq), lambda b, h, i: (b, h)),
                  pl.BlockSpec((tk_total, LANES), lambda b, h, i: (b, h))],
        out_specs=pl.BlockSpec((tq, LANES), lambda b, h, i: (b * nq + i, h)),
        compiler_params=_params(("parallel", "parallel", "arbitrary"), VMEM_LIMIT_BYTES),
        name="attention",
    )(q, k, v)


def _attn_t_kernel(q_ref, k_ref, vt_ref, o_ref, sa_ref, sb_ref, *, tq, tk):
    i = pl.program_id(1)
    dq, dv = 2 * LANES, LANES
    n_heads = q_ref.shape[1] // dq

    def scores(j, s_ref):
        off = pl.multiple_of(j * tk, tk)
        for h in range(n_heads):
            q = q_ref[:, h * dq:(h + 1) * dq]
            k = k_ref[pl.ds(off, tk), h * dq:(h + 1) * dq]
            s_ref[h] = lax.dot_general(k, q, (((1,), (1,)), ((), ())), preferred_element_type=F32)

    def softmax_pv(j, s_ref, carries, masked):
        off = pl.multiple_of(j * tk, tk)
        out = []
        for h in range(n_heads):
            m, l, acc = carries[h]
            s = s_ref[h]
            if masked:
                shift = CHUNK.bit_length() - 1
                kc = lax.shift_right_logical(j * tk + lax.broadcasted_iota(jnp.int32, s.shape, 0), shift)
                qc = lax.shift_right_logical(i * tq + lax.broadcasted_iota(jnp.int32, s.shape, 1), shift)
                s = jnp.where(kc <= qc, s, -1e30)
            m_new = jnp.maximum(m, jnp.max(s, axis=0, keepdims=True))
            alpha = jnp.exp2(m - m_new)
            p = jnp.exp2(s - m_new)
            l = alpha * l + jnp.sum(p, axis=0, keepdims=True)
            vt = vt_ref[h * dv:(h + 1) * dv, pl.ds(off, tk)]
            acc = alpha * acc + jnp.dot(vt, p.astype(BF16), preferred_element_type=F32)
            out.append((m_new, l, acc))
        return tuple(out)

    def finish(carries):
        for h, (_, l, acc) in enumerate(carries):
            o_ref[:, h * dv:(h + 1) * dv] = (acc / l).T.astype(o_ref.dtype)

    scores(0, sa_ref)

    def pair(t, carries):
        scores(2 * t + 1, sb_ref)
        carries = softmax_pv(2 * t, sa_ref, carries, False)
        scores(2 * t + 2, sa_ref)
        return softmax_pv(2 * t + 1, sb_ref, carries, False)

    init = (jnp.full((1, tq), -jnp.inf, F32), jnp.zeros((1, tq), F32), jnp.zeros((dv, tq), F32))
    carries = lax.fori_loop(0, i // 2, pair, (init,) * n_heads)

    @pl.when(i % 2 == 0)
    def _():
        finish(softmax_pv(i, sa_ref, carries, True))

    @pl.when(i % 2 == 1)
    def _():
        scores(i, sb_ref)
        finish(softmax_pv(i, sb_ref, softmax_pv(i - 1, sa_ref, carries, False), True))


def attention_causal(q, k, vt, heads, *, t, tq, hp=2):
    assert CHUNK & (CHUNK - 1) == 0 and tq % CHUNK == 0 and heads % hp == 0
    dq = 2 * LANES
    return pl.pallas_call(
        functools.partial(_attn_t_kernel, tq=tq, tk=tq),
        out_shape=jax.ShapeDtypeStruct((t, heads * LANES), BF16),
        grid=(heads // hp, t // tq),
        in_specs=[pl.BlockSpec((tq, hp * dq), lambda h, i: (i, h)),
                  pl.BlockSpec((t, hp * dq), lambda h, i: (0, h)),
                  pl.BlockSpec((hp * LANES, t), lambda h, i: (h, 0))],
        out_specs=pl.BlockSpec((tq, hp * LANES), lambda h, i: (i, h)),
        scratch_shapes=[pltpu.VMEM((hp, tq, tq), F32), pltpu.VMEM((hp, tq, tq), F32)],
        compiler_params=_params(("parallel", "arbitrary"), VMEM_LIMIT_BYTES),
        name="attention_causal",
    )(q, k, vt)


def _retention_kernel(q_ref, k_ref, v_ref, s0_ref, d_ref, xi_ref, zeta_ref, gc_ref,
                      o_ref, sout_ref, s_scr, *, n_chunks):
    c = pl.program_id(2)
    dk = s_scr.shape[-1]

    @pl.when(c == 0)
    def _():
        s_scr[...] = s0_ref[...]

    for h in range(s_scr.shape[0]):
        cols = slice(h * dk, (h + 1) * dk)
        q = q_ref[:, cols]
        k = k_ref[:, cols]
        v = v_ref[:, cols]
        s = s_scr[h]
        a = lax.dot_general(q, k, (((1,), (1,)), ((), ())), preferred_element_type=F32) * d_ref[h]
        o = jnp.dot(a.astype(BF16), v, preferred_element_type=F32)
        o = o + jnp.dot(q, s.astype(BF16), preferred_element_type=F32) * xi_ref[h]
        kz = (k.astype(F32) * zeta_ref[h]).astype(BF16)
        s_new = s * gc_ref[h] + lax.dot_general(kz, v, (((0,), (0,)), ((), ())),
                                                preferred_element_type=F32)
        s_scr[h] = s_new
        o_ref[:, cols] = (o * lax.rsqrt(jnp.mean(o * o, axis=-1, keepdims=True) + NORM_EPS)).astype(o_ref.dtype)

    @pl.when(c == n_chunks - 1)
    def _():
        sout_ref[...] = s_scr[...]


def retention(big, s0, tabs, heads, dk, *, batch, t_total, row0, chunk, hp=4):
    d_tab, xi_tab, zeta_tab, gc_tab = tabs
    assert heads % hp == 0
    nc = t_total // chunk
    rb0 = row0 // chunk
    hg = heads // hp
    rowblk = lambda g0: pl.BlockSpec((chunk, hp * dk), lambda b, h, c: (rb0 + b * nc + c, g0 + h))
    per_head = lambda a: pl.BlockSpec((hp,) + a.shape[1:], lambda b, h, c: (h, 0, 0))
    st_spec = pl.BlockSpec((None, hp, dk, dk), lambda b, h, c: (b, h, 0, 0))
    return pl.pallas_call(
        functools.partial(_retention_kernel, n_chunks=nc),
        out_shape=(jax.ShapeDtypeStruct((batch * t_total, heads * dk), BF16),
                   jax.ShapeDtypeStruct((batch, heads, dk, dk), F32)),
        grid=(batch, hg, nc),
        in_specs=[rowblk(0), rowblk(hg), rowblk(2 * hg), st_spec,
                  per_head(d_tab), per_head(xi_tab), per_head(zeta_tab), per_head(gc_tab)],
        out_specs=(pl.BlockSpec((chunk, hp * dk), lambda b, h, c: (b * nc + c, h)), st_spec),
        scratch_shapes=[pltpu.VMEM((hp, dk, dk), F32)],
        compiler_params=_params(("parallel", "parallel", "arbitrary"), VMEM_LIMIT_BYTES),
        name="retention",
    )(big, big, big, s0, d_tab, xi_tab, zeta_tab, gc_tab)


def retention_tables(length, heads, dk):
    lg = jnp.log1p(-jnp.exp2(-5.0 - jnp.arange(heads, dtype=F32)))
    idx = jnp.arange(length, dtype=F32)
    diff = idx[:, None] - idx[None, :]
    decay = jnp.where(diff >= 0, jnp.exp(jnp.maximum(diff, 0.0)[None] * lg[:, None, None]), 0.0)
    xi = jnp.exp((idx + 1.0)[None, :] * lg[:, None])
    zeta = jnp.exp((length - 1.0 - idx)[None, :] * lg[:, None])
    gc = jnp.exp(length * lg)
    bc = lambda a: jnp.broadcast_to(a[:, :, None], (heads, a.shape[1], dk))
    return decay, bc(xi), bc(zeta), jnp.broadcast_to(gc[:, None, None], (heads, 1, dk))


def _mixer_out_kernel(oap_ref, oas_ref, orp_ref, ors_ref, rg_ref, ga_ref, gb_ref, xp_ref, xs_ref, wo_ref,
                      g_ref, wr_ref, br_ref, h_ref, hn_ref, lg_ref, *, prompt_blocks):
    is_prompt = pl.program_id(0) < prompt_blocks
    oa = jnp.where(is_prompt, oap_ref[...], oas_ref[...]).astype(F32)
    orn = jnp.where(is_prompt, orp_ref[...], ors_ref[...]).astype(F32)
    x = jnp.where(is_prompt, xp_ref[...], xs_ref[...])
    rg = rg_ref[...].astype(F32)
    yb = rg * _sigmoid(rg) * orn
    merged = _sigmoid(ga_ref[...].astype(F32)) * oa + _sigmoid(gb_ref[...].astype(F32)) * yb
    h = x + jnp.dot(merged.astype(BF16), wo_ref[...], preferred_element_type=F32)
    h_ref[...] = h
    hn = _rms(h, g_ref[...])
    hn_ref[...] = hn
    lg_ref[...] = jnp.dot(hn, wr_ref[...], preferred_element_type=F32,
                          precision=lax.Precision.HIGHEST) + br_ref[...]


def mixer_out(oa_p, oa_s, or_p, or_s, big, gate_blk0, xp, xs, wo, g, wr, br, tm=256):
    d = xp.shape[1]
    m = xp.shape[0] + xs.shape[0]
    e = wr.shape[1]
    npb, prompt, sample = _group_specs(xp.shape[0], xs.shape[0], tm, d)
    row = lambda col: pl.BlockSpec((tm, d), lambda i: (i, col))
    full = lambda a: pl.BlockSpec(a.shape, lambda i: (0, 0))
    return pl.pallas_call(
        functools.partial(_mixer_out_kernel, prompt_blocks=npb),
        out_shape=(jax.ShapeDtypeStruct((m, d), F32),
                   jax.ShapeDtypeStruct((m, d), F32),
                   jax.ShapeDtypeStruct((m, e), F32)),
        grid=(m // tm,),
        in_specs=[prompt, sample, prompt, sample, row(gate_blk0), row(gate_blk0 + 1), row(gate_blk0 + 2),
                  prompt, sample, full(wo), full(g), full(wr), full(br)],
        out_specs=(row(0), row(0), pl.BlockSpec((tm, e), lambda i: (i, 0))),
        compiler_params=_params(("parallel",), VMEM_LIMIT_BYTES),
        name="mixer_out",
    )(oa_p, oa_s, or_p, or_s, big, big, big, xp, xs, wo, g, wr, br)


def _expert_kernel(tok_ref, ie_ref, ib_ref, inb_ref, nv_ref, hn_ref, wg_ref, wu_ref, bg_ref, bu_ref, wd_ref,
                   bd_ref, ye_ref, x32, y32, wgb, wub, wdb, sem_x, sem_out, *, rows, group, n_f, n_items,
                   n_blocks):
    k = pl.program_id(0)
    f = pl.program_id(1)
    nb = inb_ref[k]
    blk0 = ib_ref[k]
    n_valid = nv_ref[0]
    valid = k < n_valid
    has_next = k + 1 < n_valid
    last_f = f == n_f - 1
    cur = k % 2
    rs = rows * group
    d = y32.shape[1]
    per_block = rs // (n_f * group)

    def gather_copy(slot, buf, row):
        return pltpu.make_async_copy(hn_ref.at[pl.ds(tok_ref[slot], 1), :],
                                     x32.at[buf, pl.ds(row, 1), :], sem_x.at[buf])

    def out_copy(b):
        return pltpu.make_async_copy(y32.at[pl.ds(b * rows, rows), :],
                                     ye_ref.at[pl.ds((blk0 + b) * rows, rows), :], sem_out)

    @pl.when((k == 0) & (f == 0))
    def _():
        def body(r, _):
            gather_copy(blk0 * rows + r, 0, r).start()
            return 0
        lax.fori_loop(0, rs, body, 0, unroll=8)

    @pl.when(valid & (f == 0))
    def _():
        def drain(r, _):
            gather_copy(0, cur, 0).wait()
            return 0
        lax.fori_loop(0, rs, drain, 0, unroll=8)

        def init(b, _):
            y32[pl.ds(pl.multiple_of(b * rows, rows), rows), :] = jnp.broadcast_to(bd_ref[...], (rows, d))
            return 0
        lax.fori_loop(0, nb, init, 0)

    @pl.when(valid)
    def _():
        wgb[...] = wg_ref[...].astype(BF16)
        wub[...] = wu_ref[...].astype(BF16)
        wdb[...] = wd_ref[...].astype(BF16)

    def prefetch(b, next_slot0):
        row0 = (f * group + b) * per_block
        for r in range(per_block):
            gather_copy(next_slot0 + row0 + r, 1 - cur, row0 + r).start()

    def compute_blocks(b0, n, next_slot0):
        r = pl.ds(pl.multiple_of(b0 * rows, rows), n * rows)
        x = x32[cur, r, :].astype(BF16)
        gate = jnp.minimum(jnp.dot(x, wgb[...], preferred_element_type=F32) + bg_ref[...], SWIGLU_LIMIT)
        up = jnp.clip(jnp.dot(x, wub[...], preferred_element_type=F32) + bu_ref[...],
                      -SWIGLU_LIMIT, SWIGLU_LIMIT)
        h = (up + 1.0) * (gate * _sigmoid(gate * SWIGLU_ALPHA))
        y32[r, :] += jnp.dot(h.astype(BF16), wdb[...], preferred_element_type=F32)
        if next_slot0 is not None:
            for i in range(n):
                prefetch(b0 + i, next_slot0)

        @pl.when(last_f)
        def _():
            for i in range(n):
                out_copy(b0 + i).start()

    def run(next_slot0):
        def pair(p, _):
            compute_blocks(2 * p, 2, next_slot0)
            return 0
        lax.fori_loop(0, nb // 2, pair, 0)

        @pl.when(nb % 2 == 1)
        def _():
            compute_blocks(nb - 1, 1, next_slot0)

    @pl.when(has_next)
    def _():
        next_slot0 = ib_ref[k + 1] * rows
        run(next_slot0)

        def rest(b, _):
            prefetch(b, next_slot0)
            return 0
        lax.fori_loop(nb, group, rest, 0)

    @pl.when(valid & jnp.logical_not(has_next))
    def _():
        run(None)

    @pl.when(valid & last_f)
    def _():
        def body(b, _):
            out_copy(b).wait()
            return 0
        lax.fori_loop(0, nb, body, 0)

    @pl.when((k == n_items - 1) & last_f)
    def _():
        used = nv_ref[1]
        y32[0:rows, :] = jnp.zeros((rows, d), F32)

        def fill(b):
            return pltpu.make_async_copy(y32.at[pl.ds(0, rows), :], ye_ref.at[pl.ds(b * rows, rows), :], sem_out)

        def start(b, _):
            fill(b).start()
            return 0

        def wait(b, _):
            fill(b).wait()
            return 0
        lax.fori_loop(used, n_blocks, start, 0)
        lax.fori_loop(used, n_blocks, wait, 0)


def expert_mlp(slot_tok, items, hn, w_gate_up, b_gate_up, w_down, b_down, n_items, n_blocks):
    item_e, item_b0, item_nb, counts = items
    n_exp, d, two_f = w_gate_up.shape
    d_ff = two_f // 2
    tf = MOE_FTILE
    n_f = d_ff // tf
    rows, group = MOE_ROWS, MOE_GROUP
    rs = rows * group
    assert rs % (n_f * group) == 0

    def fi(k, f, nv):
        return jnp.where(k < nv[0], f, n_f - 1)

    in_specs = [
        pl.BlockSpec(memory_space=pl.ANY),
        pl.BlockSpec((None, d, tf), lambda k, f, tok, ie, ib, inb, nv: (ie[k], 0, fi(k, f, nv))),
        pl.BlockSpec((None, d, tf), lambda k, f, tok, ie, ib, inb, nv: (ie[k], 0, n_f + fi(k, f, nv))),
        pl.BlockSpec((None, 1, tf), lambda k, f, tok, ie, ib, inb, nv: (ie[k], 0, fi(k, f, nv))),
        pl.BlockSpec((None, 1, tf), lambda k, f, tok, ie, ib, inb, nv: (ie[k], 0, n_f + fi(k, f, nv))),
        pl.BlockSpec((None, tf, d), lambda k, f, tok, ie, ib, inb, nv: (ie[k], fi(k, f, nv), 0)),
        pl.BlockSpec((None, 1, d), lambda k, f, tok, ie, ib, inb, nv: (ie[k], 0, 0)),
    ]
    return pl.pallas_call(
        functools.partial(_expert_kernel, rows=rows, group=group, n_f=n_f, n_items=n_items, n_blocks=n_blocks),
        out_shape=jax.ShapeDtypeStruct((n_blocks * rows, d), F32),
        grid_spec=pltpu.PrefetchScalarGridSpec(
            num_scalar_prefetch=5, grid=(n_items, n_f),
            in_specs=in_specs,
            out_specs=pl.BlockSpec(memory_space=pl.ANY),
            scratch_shapes=[pltpu.VMEM((2, rs, d), F32), pltpu.VMEM((rs, d), F32),
                            pltpu.VMEM((d, tf), BF16), pltpu.VMEM((d, tf), BF16), pltpu.VMEM((tf, d), BF16),
                            pltpu.SemaphoreType.DMA((2,)), pltpu.SemaphoreType.DMA]),
        compiler_params=_params(("arbitrary", "arbitrary"), VMEM_LIMIT_BYTES),
        name="moe_experts",
    )(slot_tok, item_e, item_b0, item_nb, counts, hn, w_gate_up, w_gate_up,
      b_gate_up.reshape(n_exp, 1, two_f), b_gate_up.reshape(n_exp, 1, two_f),
      w_down, b_down.reshape(n_exp, 1, d))


def _combine_kernel(pos_ref, ye_ref, w_ref, h_ref, g_ref, o_ref, buf, sems, *, tc, blk0, n_steps, final_norm):
    i = pl.program_id(0)
    n_rows = TOP_K * tc

    def copy(r, src_row, par):
        return pltpu.make_async_copy(ye_ref.at[pl.ds(src_row, 1), :], buf.at[par, pl.ds(r, 1), :],
                                     sems.at[par])

    def issue(step, par):
        def body(r, _):
            copy(r, pos_ref[(blk0 + step) * n_rows + r], par).start()
            return 0
        lax.fori_loop(0, n_rows, body, 0, unroll=8)

    @pl.when(i == 0)
    def _():
        issue(0, 0)

    @pl.when(i + 1 < n_steps)
    def _():
        issue(i + 1, (i + 1) % 2)

    par = i % 2

    def drain(r, _):
        copy(0, 0, par).wait()
        return 0
    lax.fori_loop(0, n_rows, drain, 0, unroll=8)

    w = w_ref[...]
    y = h_ref[...]
    for k in range(TOP_K):
        y = y + w[:, k:k + 1] * buf[par, k * tc:(k + 1) * tc, :]
    o_ref[...] = _rms(y, g_ref[...]) if final_norm else y


def combine(pos, ye, top_w, h, g, row0, rows, final_norm, tc=COMBINE_ROWS):
    d = h.shape[1]
    blk0 = row0 // tc
    n_steps = rows // tc
    return pl.pallas_call(
        functools.partial(_combine_kernel, tc=tc, blk0=blk0, n_steps=n_steps, final_norm=final_norm),
        out_shape=jax.ShapeDtypeStruct((rows, d), F32),
        grid_spec=pltpu.PrefetchScalarGridSpec(
            num_scalar_prefetch=1, grid=(n_steps,),
            in_specs=[pl.BlockSpec(memory_space=pl.ANY),
                      pl.BlockSpec((tc, TOP_K), lambda i, p: (blk0 + i, 0)),
                      pl.BlockSpec((tc, d), lambda i, p: (blk0 + i, 0)),
                      pl.BlockSpec((1, d), lambda i, p: (0, 0))],
            out_specs=pl.BlockSpec((tc, d), lambda i, p: (i, 0)),
            scratch_shapes=[pltpu.VMEM((2, TOP_K * tc, d), F32), pltpu.SemaphoreType.DMA((2,))]),
        compiler_params=_params(("arbitrary",), VMEM_LIMIT_BYTES),
        name="moe_combine",
    )(pos, ye, top_w, h, g)


def route(logits, tc):
    m, n_exp = logits.shape
    a = m * TOP_K
    rows, group = MOE_ROWS, MOE_GROUP
    top_logit, top_idx = lax.top_k(logits, TOP_K)
    top_w = jax.nn.softmax(top_logit, axis=-1)
    flat_e = top_idx.reshape(a)
    onehot = (flat_e[:, None] == jnp.arange(n_exp, dtype=jnp.int32)[None, :]).astype(jnp.int32)
    csum = jnp.cumsum(onehot, axis=0)
    counts = csum[-1]
    rank = jnp.take_along_axis(csum, flat_e[:, None], axis=1)[:, 0] - 1
    nb_e = (counts + rows - 1) // rows
    blk_start = jnp.cumsum(nb_e) - nb_e
    dest = blk_start[flat_e] * rows + rank
    n_blocks = a // rows + n_exp
    slot_tok = jnp.zeros(((n_blocks + group) * rows,), jnp.int32).at[dest].set(
        jnp.arange(a, dtype=jnp.int32) // TOP_K)
    ni_e = (nb_e + group - 1) // group
    item_end = jnp.cumsum(ni_e)
    n_valid = item_end[-1]
    n_items = (n_blocks + n_exp * (group - 1)) // group
    kk = jnp.arange(n_items, dtype=jnp.int32)
    item_e = jnp.minimum(jnp.sum(item_end[None, :] <= kk[:, None], axis=1), n_exp - 1).astype(jnp.int32)
    local = kk - (item_end - ni_e)[item_e]
    valid = kk < n_valid
    item_e = jnp.where(valid, item_e, item_e[jnp.maximum(n_valid - 1, 0)])
    item_b0 = jnp.where(valid, blk_start[item_e] + local * group, 0).astype(jnp.int32)
    item_nb = jnp.where(valid, jnp.clip(nb_e[item_e] - local * group, 0, group), 0).astype(jnp.int32)
    pos = dest.reshape(m // tc, tc, TOP_K).transpose(0, 2, 1).reshape(a)
    counts = jnp.stack([n_valid, jnp.sum(nb_e)]).astype(jnp.int32)
    items = (item_e, item_b0, item_nb, counts)
    return top_w, slot_tok, pos.astype(jnp.int32), items, n_items, n_blocks


def _rope_tables(pos, half):
    inv = jnp.power(ROPE_THETA, -jnp.arange(half, dtype=F32) / half)
    ang = pos.astype(F32)[:, None] * inv[None, :]
    return jnp.cos(ang), jnp.sin(ang)


def kernel(x_prompt, x_sample, cache_latent, cache_k_rope, state_retention, g_mix_norm, w_in, g_q_norm, w_uq, g_kv_norm, w_uk, w_uv, w_o, g_ffn_norm, w_router, b_router, w_gate_up, b_gate_up, w_down, b_down, g_final):
    bp, t, d = x_prompt.shape
    db, ts, _ = x_sample.shape
    depth = w_in.shape[0]
    past = cache_latent.shape[2]
    ql, ha, qk = w_uq.shape[1:]
    kl = w_uk.shape[1]
    rope = cache_k_rope.shape[-1]
    nope = qk - rope
    hr, dk = state_retention.shape[2:4]
    assert bp == 1 and nope == LANES and w_uv.shape[-1] == LANES and 2 * rope == LANES
    mp, ms = bp * t, db * ts
    m = mp + ms
    tk_s = past + ts
    mla_scale = float(qk) ** -0.5 * 1.4426950408889634

    pos = jnp.concatenate([jnp.arange(t, dtype=jnp.int32),
                           jnp.tile(past + jnp.arange(ts, dtype=jnp.int32), db)])
    cos_a, sin_a = _rope_tables(pos, rope // 2)
    z = lambda w: jnp.zeros((m, w), F32)
    rope_c = jnp.concatenate([cos_a, cos_a, z(LANES - rope)], axis=1)
    rope_sa = jnp.concatenate([z(rope // 2), sin_a, z(LANES - rope)], axis=1)
    rope_sb = jnp.concatenate([-sin_a, z(LANES - rope // 2)], axis=1)
    cos_r, sin_r = _rope_tables(pos, dk // 2)
    tabs_p = retention_tables(RET_CHUNK, hr, dk)
    tabs_s = retention_tables(ts, hr, dk)

    xp, xs = x_prompt.reshape(mp, d), x_sample.reshape(ms, d)
    lat_p, kr_p, st_p, lat_s, kr_s, st_s = [], [], [], [], [], []
    small = ql + kl + rope
    for l in range(depth):
        wa = jnp.pad(w_in[l][:, :small], ((0, 0), (0, LANES - rope))).astype(BF16)
        wb = w_in[l][:, small:].astype(BF16)
        wq = jnp.pad(w_uq[l], ((0, 0), (0, 0), (0, 2 * LANES - qk))).reshape(ql, ha * 2 * LANES).astype(BF16)
        wk = w_uk[l].reshape(kl, ha * nope).astype(BF16)
        wv = w_uv[l].reshape(kl, ha * LANES).astype(BF16)
        wo = w_o[l].astype(BF16)

        xn = norm_cast(xp, xs, g_mix_norm[l].reshape(1, d))
        cqn, ckv, ckv_b, kr, kr_pad = proj_small(xn, wa, g_q_norm[l].reshape(1, ql), g_kv_norm[l].reshape(1, kl),
                                                 rope_c, rope_sa, rope_sb, ql, kl, rope)
        big = proj_big(xn, wb, cos_r, sin_r, 2 * hr * dk, dk, float(dk) ** -0.5)
        qcat = q_proj(cqn, wq, rope_c, rope_sa, rope_sb, ha, mla_scale)

        k_p, vt_p = kv_expand(ckv_b, kr_pad, wk, wv.T, mp, ha, True)
        oa_p = attention_causal(qcat, k_p, vt_p, ha, t=mp, tq=512)
        c_all = jnp.concatenate([cache_latent[l], ckv[mp:].reshape(db, ts, kl)], axis=1)
        kr_all = jnp.concatenate([cache_k_rope[l], kr[mp:].reshape(db, ts, rope)], axis=1)
        kr_all = jnp.pad(kr_all, ((0, 0), (0, 0), (0, LANES - rope)))
        k_s, v_s = kv_expand(c_all.reshape(db * tk_s, kl).astype(BF16),
                             kr_all.reshape(db * tk_s, LANES).astype(BF16), wk, wv, db * tk_s, ha, False)
        oa_s = attention(qcat, k_s, v_s, ha, batch=db, tq_total=ts, q_row0=mp, tk_total=tk_s,
                         tq=ts, tk=tk_s, causal=False)

        or_p, s_p = retention(big, jnp.zeros((bp, hr, dk, dk), F32), tabs_p, hr, dk,
                              batch=bp, t_total=t, row0=0, chunk=RET_CHUNK)
        or_s, s_s = retention(big, state_retention[l].astype(F32), tabs_s, hr, dk,
                              batch=db, t_total=ts, row0=mp, chunk=ts)

        h1, hn, logits = mixer_out(oa_p, oa_s, or_p, or_s, big, 3, xp, xs, wo, g_ffn_norm[l].reshape(1, d),
                                   w_router[l], b_router[l].reshape(1, -1))

        top_w, slot_tok, pos_flat, items, n_items, n_blocks = route(logits, COMBINE_ROWS)
        ye = expert_mlp(slot_tok, items, hn, w_gate_up[l], b_gate_up[l], w_down[l], b_down[l], n_items, n_blocks)
        last = l == depth - 1
        g_out = g_final.reshape(1, d)
        y_p = combine(pos_flat, ye, top_w, h1, g_out, 0, mp, last)
        y_s = combine(pos_flat, ye, top_w, h1, g_out, mp, ms, last)
        xp, xs = y_p, y_s

        lat_p.append(ckv[:mp].reshape(bp, t, kl))
        kr_p.append(kr[:mp].reshape(bp, t, rope))
        st_p.append(s_p)
        lat_s.append(ckv[mp:].reshape(db, ts, kl))
        kr_s.append(kr[mp:].reshape(db, ts, rope))
        st_s.append(s_s)

    return (y_p.reshape(bp, t, d), y_s.reshape(db, ts, d),
            jnp.stack(lat_p), jnp.stack(kr_p), jnp.stack(st_p),
            jnp.stack(lat_s), jnp.stack(kr_s), jnp.stack(st_s))
```

```python
import functools

import jax
import jax.numpy as jnp
from jax import lax
from jax.experimental import pallas as pl
from jax.experimental.pallas import tpu as pltpu

F32 = jnp.float32
BF16 = jnp.bfloat16

NORM_EPS = 1e-6
ROPE_THETA = 10000.0
CHUNK = 64
TOP_K = 4
SWIGLU_LIMIT = 7.0
SWIGLU_ALPHA = 1.702

LANES = 128
VMEM_LIMIT_BYTES = 56 * 1024 * 1024

RET_CHUNK = 256
MOE_ROWS = 256
MOE_GROUP = 5
MOE_FTILE = 256
COMBINE_ROWS = 128
MIXER_SUBTILES = 2
ABSORBED_HEADS = 4


def _params(sem, vmem=None):
    return pltpu.CompilerParams(dimension_semantics=sem, vmem_limit_bytes=vmem)


def _rms(x, g):
    return x * lax.rsqrt(jnp.mean(x * x, axis=-1, keepdims=True) + NORM_EPS) * g


def _sigmoid(x):
    return 1.0 / (1.0 + jnp.exp(-x))


def _group_specs(rows_p, rows_s, tm, d):
    npb, nsb = rows_p // tm, rows_s // tm
    prompt = pl.BlockSpec((tm, d), lambda i: (jnp.minimum(i, npb - 1), 0))
    sample = pl.BlockSpec((tm, d), lambda i: (jnp.clip(i - npb, 0, nsb - 1), 0))
    return npb, prompt, sample


def _norm_cast_kernel(xp_ref, xs_ref, g_ref, o_ref, *, prompt_blocks):
    x = jnp.where(pl.program_id(0) < prompt_blocks, xp_ref[...], xs_ref[...])
    o_ref[...] = _rms(x, g_ref[...]).astype(o_ref.dtype)


def norm_cast(xp, xs, g, tm=512):
    d = xp.shape[1]
    m = xp.shape[0] + xs.shape[0]
    npb, prompt, sample = _group_specs(xp.shape[0], xs.shape[0], tm, d)
    return pl.pallas_call(
        functools.partial(_norm_cast_kernel, prompt_blocks=npb),
        out_shape=jax.ShapeDtypeStruct((m, d), BF16),
        grid=(m // tm,),
        in_specs=[prompt, sample, pl.BlockSpec((1, d), lambda i: (0, 0))],
        out_specs=pl.BlockSpec((tm, d), lambda i: (i, 0)),
        compiler_params=_params(("parallel",), VMEM_LIMIT_BYTES),
        name="norm_cast",
    )(xp, xs, g)


def _rope_hi(x, c, sa, sb):
    return x * c + pltpu.roll(x, 32, 1) * sa + pltpu.roll(x, 96, 1) * sb


def _proj_small_kernel(x_ref, w_ref, gq_ref, gkv_ref, c_ref, sa_ref, sb_ref,
                       cqn_ref, ckv_ref, ckvb_ref, kr_ref, krp_ref, *, ql, kl, rope):
    acc = jnp.dot(x_ref[...], w_ref[...], preferred_element_type=F32)
    cqn_ref[...] = _rms(acc[:, :ql], gq_ref[...]).astype(cqn_ref.dtype)
    ckv = _rms(acc[:, ql:ql + kl], gkv_ref[...])
    ckv_ref[...] = ckv
    ckvb_ref[...] = ckv.astype(ckvb_ref.dtype)
    kr = _rope_hi(acc[:, ql + kl:ql + kl + LANES], c_ref[...], sa_ref[...], sb_ref[...])
    kr_ref[...] = kr[:, :rope]
    krp_ref[...] = kr.astype(krp_ref.dtype)


def proj_small(xn, wa, gq, gkv, c, sa, sb, ql, kl, rope, tm=256):
    m, d = xn.shape
    n = wa.shape[1]
    row = lambda w: pl.BlockSpec((tm, w), lambda i: (i, 0))
    full = lambda a: pl.BlockSpec(a.shape, lambda i: (0, 0))
    return pl.pallas_call(
        functools.partial(_proj_small_kernel, ql=ql, kl=kl, rope=rope),
        out_shape=(jax.ShapeDtypeStruct((m, ql), BF16),
                   jax.ShapeDtypeStruct((m, kl), F32),
                   jax.ShapeDtypeStruct((m, kl), BF16),
                   jax.ShapeDtypeStruct((m, rope), F32),
                   jax.ShapeDtypeStruct((m, LANES), BF16)),
        grid=(m // tm,),
        in_specs=[row(d), full(wa), full(gq), full(gkv), row(LANES), row(LANES), row(LANES)],
        out_specs=(row(ql), row(kl), row(kl), row(rope), row(LANES)),
        compiler_params=_params(("parallel",), VMEM_LIMIT_BYTES),
        name="proj_small",
    )(xn, wa, gq, gkv, c, sa, sb)


def _proj_big_kernel(x_ref, w_ref, cos_ref, sin_ref, o_ref, *, n_rope, dk, k_scale):
    j = pl.program_id(0)
    acc = jnp.dot(x_ref[...], w_ref[...], preferred_element_type=F32)

    @pl.when(j >= n_rope)
    def _():
        o_ref[...] = acc.astype(o_ref.dtype)

    @pl.when(j < n_rope)
    def _():
        scale = jnp.where(j >= n_rope // 2, k_scale, 1.0).astype(F32)
        c = cos_ref[...] * scale
        s = sin_ref[...] * scale
        half = dk // 2
        for h in range(acc.shape[1] // dk):
            x1 = acc[:, h * dk:h * dk + half]
            x2 = acc[:, h * dk + half:(h + 1) * dk]
            o_ref[:, h * dk:h * dk + half] = (x1 * c - x2 * s).astype(o_ref.dtype)
            o_ref[:, h * dk + half:(h + 1) * dk] = (x2 * c + x1 * s).astype(o_ref.dtype)


def proj_big(xn, wb, cos_r, sin_r, rope_cols, dk, k_scale, tm=512, tn=1024):
    m, d = xn.shape
    n = wb.shape[1]
    return pl.pallas_call(
        functools.partial(_proj_big_kernel, n_rope=rope_cols // tn, dk=dk, k_scale=k_scale),
        out_shape=jax.ShapeDtypeStruct((m, n), BF16),
        grid=(n // tn, m // tm),
        in_specs=[pl.BlockSpec((tm, d), lambda j, i: (i, 0)),
                  pl.BlockSpec((d, tn), lambda j, i: (0, j)),
                  pl.BlockSpec((tm, dk // 2), lambda j, i: (i, 0)),
                  pl.BlockSpec((tm, dk // 2), lambda j, i: (i, 0))],
        out_specs=pl.BlockSpec((tm, tn), lambda j, i: (i, j)),
        compiler_params=_params(("parallel", "parallel"), VMEM_LIMIT_BYTES),
        name="proj_big",
    )(xn, wb, cos_r, sin_r)


def _q_proj_kernel(x_ref, w_ref, c_ref, sa_ref, sb_ref, o_ref, *, heads, scale):
    acc = jnp.dot(x_ref[...], w_ref[...], preferred_element_type=F32)
    c = c_ref[...] * scale
    sa = sa_ref[...] * scale
    sb = sb_ref[...] * scale
    for h in range(heads):
        lo = 2 * h * LANES
        o_ref[:, lo:lo + LANES] = (acc[:, lo:lo + LANES] * scale).astype(o_ref.dtype)
        o_ref[:, lo + LANES:lo + 2 * LANES] = _rope_hi(
            acc[:, lo + LANES:lo + 2 * LANES], c, sa, sb).astype(o_ref.dtype)


def q_proj(cqn, wq, c, sa, sb, heads, scale, tm=256):
    m, k = cqn.shape
    n = wq.shape[1]
    row = lambda w: pl.BlockSpec((tm, w), lambda i: (i, 0))
    return pl.pallas_call(
        functools.partial(_q_proj_kernel, heads=heads, scale=scale),
        out_shape=jax.ShapeDtypeStruct((m, n), BF16),
        grid=(m // tm,),
        in_specs=[row(k), pl.BlockSpec((k, n), lambda i: (0, 0)), row(LANES), row(LANES), row(LANES)],
        out_specs=row(n),
        compiler_params=_params(("parallel",), VMEM_LIMIT_BYTES),
        name="q_proj",
    )(cqn, wq, c, sa, sb)


def _kv_expand_kernel(c_ref, kr_ref, wk_ref, wvt_ref, k_ref, vt_ref, *, heads):
    c = c_ref[...]
    kn = jnp.dot(c, wk_ref[...], preferred_element_type=F32)
    vt_ref[...] = lax.dot_general(wvt_ref[...], c, (((1,), (1,)), ((), ())),
                                  preferred_element_type=F32).astype(vt_ref.dtype)
    kr = kr_ref[...]
    for h in range(heads):
        k_ref[:, 2 * h * LANES:(2 * h + 1) * LANES] = kn[:, h * LANES:(h + 1) * LANES].astype(k_ref.dtype)
        k_ref[:, (2 * h + 1) * LANES:(2 * h + 2) * LANES] = kr


def kv_expand(c_bf, kr_pad, wk, wvt, rows, heads, tm=256):
    kl = c_bf.shape[1]
    dv = heads * LANES
    row = lambda w: pl.BlockSpec((tm, w), lambda i: (i, 0))
    full = lambda a: pl.BlockSpec(a.shape, lambda i: (0, 0))
    return pl.pallas_call(
        functools.partial(_kv_expand_kernel, heads=heads),
        out_shape=(jax.ShapeDtypeStruct((rows, 2 * heads * LANES), BF16),
                   jax.ShapeDtypeStruct((dv, rows), BF16)),
        grid=(rows // tm,),
        in_specs=[row(kl), row(LANES), full(wk), full(wvt)],
        out_specs=(row(2 * heads * LANES), pl.BlockSpec((dv, tm), lambda i: (0, i))),
        compiler_params=_params(("parallel",), VMEM_LIMIT_BYTES),
        name="kv_expand",
    )(c_bf, kr_pad, wk, wvt)


def _attn_absorbed_kernel(q_ref, c_ref, kr_ref, wukt_ref, wuv_ref, o_ref, *, heads):
    tq = q_ref.shape[0]
    c = c_ref[...]
    kr = kr_ref[...]
    nt = (((1,), (1,)), ((), ()))
    for g in range(heads // ABSORBED_HEADS):
        hs = range(g * ABSORBED_HEADS, (g + 1) * ABSORBED_HEADS)
        q_lat = jnp.concatenate(
            [jnp.dot(q_ref[:, 2 * h * LANES:(2 * h + 1) * LANES], wukt_ref[h],
                     preferred_element_type=F32).astype(BF16) for h in hs], axis=0)
        q_rope = jnp.concatenate([q_ref[:, (2 * h + 1) * LANES:(2 * h + 2) * LANES] for h in hs], axis=0)
        s = (lax.dot_general(q_lat, c, nt, preferred_element_type=F32)
             + lax.dot_general(q_rope, kr, nt, preferred_element_type=F32))
        p = jnp.exp2(s - jnp.max(s, axis=-1, keepdims=True))
        l = jnp.sum(p, axis=-1, keepdims=True)
        o_lat = (jnp.dot(p.astype(BF16), c, preferred_element_type=F32) / l).astype(BF16)
        for i, h in enumerate(hs):
            o_ref[:, h * LANES:(h + 1) * LANES] = jnp.dot(
                o_lat[i * tq:(i + 1) * tq], wuv_ref[h], preferred_element_type=F32).astype(o_ref.dtype)


def attention_absorbed(q, c_all, kr_all, wukt, wuv, heads, *, batch, tq, q_row0, tk):
    assert heads % ABSORBED_HEADS == 0 and q_row0 % tq == 0
    kl = c_all.shape[1]
    qb0 = q_row0 // tq
    full = lambda a: pl.BlockSpec(a.shape, lambda b: (0, 0, 0))
    return pl.pallas_call(
        functools.partial(_attn_absorbed_kernel, heads=heads),
        out_shape=jax.ShapeDtypeStruct((batch * tq, heads * LANES), BF16),
        grid=(batch,),
        in_specs=[pl.BlockSpec((tq, 2 * heads * LANES), lambda b: (qb0 + b, 0)),
                  pl.BlockSpec((tk, kl), lambda b: (b, 0)),
                  pl.BlockSpec((tk, LANES), lambda b: (b, 0)),
                  full(wukt), full(wuv)],
        out_specs=pl.BlockSpec((tq, heads * LANES), lambda b: (b, 0)),
        compiler_params=_params(("parallel",), VMEM_LIMIT_BYTES),
        name="attention_absorbed",
    )(q, c_all, kr_all, wukt, wuv)


def _attn_t_kernel(q_ref, k_ref, vt_ref, o_ref, sa_ref, sb_ref, *, tq, tk):
    i = pl.program_id(1)
    dq, dv = 2 * LANES, LANES
    n_heads = q_ref.shape[1] // dq

    def scores(j, s_ref):
        off = pl.multiple_of(j * tk, tk)
        for h in range(n_heads):
            q = q_ref[:, h * dq:(h + 1) * dq]
            k = k_ref[pl.ds(off, tk), h * dq:(h + 1) * dq]
            s_ref[h] = lax.dot_general(k, q, (((1,), (1,)), ((), ())), preferred_element_type=F32)

    def softmax_pv(j, s_ref, carries, masked):
        off = pl.multiple_of(j * tk, tk)
        out = []
        for h in range(n_heads):
            m, l, acc = carries[h]
            s = s_ref[h]
            if masked:
                shift = CHUNK.bit_length() - 1
                kc = lax.shift_right_logical(j * tk + lax.broadcasted_iota(jnp.int32, s.shape, 0), shift)
                qc = lax.shift_right_logical(i * tq + lax.broadcasted_iota(jnp.int32, s.shape, 1), shift)
                s = jnp.where(kc <= qc, s, -1e30)
            m_new = jnp.maximum(m, jnp.max(s, axis=0, keepdims=True))
            alpha = jnp.exp2(m - m_new)
            p = jnp.exp2(s - m_new)
            l = alpha * l + jnp.sum(p, axis=0, keepdims=True)
            vt = vt_ref[h * dv:(h + 1) * dv, pl.ds(off, tk)]
            acc = alpha * acc + jnp.dot(vt, p.astype(BF16), preferred_element_type=F32)
            out.append((m_new, l, acc))
        return tuple(out)

    def finish(carries):
        for h, (_, l, acc) in enumerate(carries):
            o_ref[:, h * dv:(h + 1) * dv] = (acc / l).T.astype(o_ref.dtype)

    scores(0, sa_ref)

    def pair(t, carries):
        scores(2 * t + 1, sb_ref)
        carries = softmax_pv(2 * t, sa_ref, carries, False)
        scores(2 * t + 2, sa_ref)
        return softmax_pv(2 * t + 1, sb_ref, carries, False)

    init = (jnp.full((1, tq), -jnp.inf, F32), jnp.zeros((1, tq), F32), jnp.zeros((dv, tq), F32))
    carries = lax.fori_loop(0, i // 2, pair, (init,) * n_heads)

    @pl.when(i % 2 == 0)
    def _():
        finish(softmax_pv(i, sa_ref, carries, True))

    @pl.when(i % 2 == 1)
    def _():
        scores(i, sb_ref)
        finish(softmax_pv(i, sb_ref, softmax_pv(i - 1, sa_ref, carries, False), True))


def attention_causal(q, k, vt, heads, *, t, tq, hp=2):
    assert CHUNK & (CHUNK - 1) == 0 and tq % CHUNK == 0 and heads % hp == 0
    dq = 2 * LANES
    return pl.pallas_call(
        functools.partial(_attn_t_kernel, tq=tq, tk=tq),
        out_shape=jax.ShapeDtypeStruct((t, heads * LANES), BF16),
        grid=(heads // hp, t // tq),
        in_specs=[pl.BlockSpec((tq, hp * dq), lambda h, i: (i, h)),
                  pl.BlockSpec((t, hp * dq), lambda h, i: (0, h)),
                  pl.BlockSpec((hp * LANES, t), lambda h, i: (h, 0))],
        out_specs=pl.BlockSpec((tq, hp * LANES), lambda h, i: (i, h)),
        scratch_shapes=[pltpu.VMEM((hp, tq, tq), F32), pltpu.VMEM((hp, tq, tq), F32)],
        compiler_params=_params(("parallel", "arbitrary"), VMEM_LIMIT_BYTES),
        name="attention_causal",
    )(q, k, vt)


def _retention_kernel(q_ref, k_ref, v_ref, s0_ref, d_ref, xi_ref, zeta_ref, gc_ref,
                      o_ref, sout_ref, s_scr, *, n_chunks):
    c = pl.program_id(2)
    dk = s_scr.shape[-1]

    @pl.when(c == 0)
    def _():
        s_scr[...] = s0_ref[...]

    for h in range(s_scr.shape[0]):
        cols = slice(h * dk, (h + 1) * dk)
        q = q_ref[:, cols]
        k = k_ref[:, cols]
        v = v_ref[:, cols]
        s = s_scr[h]
        a = lax.dot_general(q, k, (((1,), (1,)), ((), ())), preferred_element_type=F32) * d_ref[h]
        o = jnp.dot(a.astype(BF16), v, preferred_element_type=F32)
        o = o + jnp.dot(q, s.astype(BF16), preferred_element_type=F32) * xi_ref[h]
        kz = (k.astype(F32) * zeta_ref[h]).astype(BF16)
        s_new = s * gc_ref[h] + lax.dot_general(kz, v, (((0,), (0,)), ((), ())),
                                                preferred_element_type=F32)
        s_scr[h] = s_new
        o_ref[:, cols] = (o * lax.rsqrt(jnp.mean(o * o, axis=-1, keepdims=True) + NORM_EPS)).astype(o_ref.dtype)

    @pl.when(c == n_chunks - 1)
    def _():
        sout_ref[...] = s_scr[...]


def retention(big, s0, tabs, heads, dk, *, batch, t_total, row0, chunk, hp=4):
    d_tab, xi_tab, zeta_tab, gc_tab = tabs
    assert heads % hp == 0
    nc = t_total // chunk
    rb0 = row0 // chunk
    hg = heads // hp
    rowblk = lambda g0: pl.BlockSpec((chunk, hp * dk), lambda b, h, c: (rb0 + b * nc + c, g0 + h))
    per_head = lambda a: pl.BlockSpec((hp,) + a.shape[1:], lambda b, h, c: (h, 0, 0))
    st_spec = pl.BlockSpec((None, hp, dk, dk), lambda b, h, c: (b, h, 0, 0))
    return pl.pallas_call(
        functools.partial(_retention_kernel, n_chunks=nc),
        out_shape=(jax.ShapeDtypeStruct((batch * t_total, heads * dk), BF16),
                   jax.ShapeDtypeStruct((batch, heads, dk, dk), F32)),
        grid=(batch, hg, nc),
        in_specs=[rowblk(0), rowblk(hg), rowblk(2 * hg), st_spec,
                  per_head(d_tab), per_head(xi_tab), per_head(zeta_tab), per_head(gc_tab)],
        out_specs=(pl.BlockSpec((chunk, hp * dk), lambda b, h, c: (b * nc + c, h)), st_spec),
        scratch_shapes=[pltpu.VMEM((hp, dk, dk), F32)],
        compiler_params=_params(("parallel", "parallel", "arbitrary"), VMEM_LIMIT_BYTES),
        name="retention",
    )(big, big, big, s0, d_tab, xi_tab, zeta_tab, gc_tab)


def retention_tables(length, heads, dk):
    lg = jnp.log1p(-jnp.exp2(-5.0 - jnp.arange(heads, dtype=F32)))
    idx = jnp.arange(length, dtype=F32)
    diff = idx[:, None] - idx[None, :]
    decay = jnp.where(diff >= 0, jnp.exp(jnp.maximum(diff, 0.0)[None] * lg[:, None, None]), 0.0)
    xi = jnp.exp((idx + 1.0)[None, :] * lg[:, None])
    zeta = jnp.exp((length - 1.0 - idx)[None, :] * lg[:, None])
    gc = jnp.exp(length * lg)
    bc = lambda a: jnp.broadcast_to(a[:, :, None], (heads, a.shape[1], dk))
    return decay, bc(xi), bc(zeta), jnp.broadcast_to(gc[:, None, None], (heads, 1, dk))


def _mixer_out_kernel(oap_ref, oas_ref, orp_ref, ors_ref, rg_ref, ga_ref, gb_ref, xp_ref, xs_ref, wo_ref,
                      g_ref, wr_ref, br_ref, h_ref, hn_ref, lg_ref, *, prompt_blocks):
    is_prompt = pl.program_id(0) < prompt_blocks
    tm = h_ref.shape[0]
    sub = tm // MIXER_SUBTILES
    for s in range(MIXER_SUBTILES):
        r = slice(s * sub, (s + 1) * sub)
        oa = jnp.where(is_prompt, oap_ref[r, :], oas_ref[r, :]).astype(F32)
        orn = jnp.where(is_prompt, orp_ref[r, :], ors_ref[r, :]).astype(F32)
        x = jnp.where(is_prompt, xp_ref[r, :], xs_ref[r, :])
        rg = rg_ref[r, :].astype(F32)
        yb = rg * _sigmoid(rg) * orn
        merged = _sigmoid(ga_ref[r, :].astype(F32)) * oa + _sigmoid(gb_ref[r, :].astype(F32)) * yb
        h = x + jnp.dot(merged.astype(BF16), wo_ref[...], preferred_element_type=F32)
        h_ref[r, :] = h
        hn = _rms(h, g_ref[...])
        hn_ref[r, :] = hn
        lg_ref[r, :] = jnp.dot(hn, wr_ref[...], preferred_element_type=F32,
                               precision=lax.Precision.HIGHEST) + br_ref[...]


def mixer_out(oa_p, oa_s, or_p, or_s, big, gate_blk0, xp, xs, wo, g, wr, br, tm=256):
    d = xp.shape[1]
    m = xp.shape[0] + xs.shape[0]
    e = wr.shape[1]
    npb, prompt, sample = _group_specs(xp.shape[0], xs.shape[0], tm, d)
    row = lambda col: pl.BlockSpec((tm, d), lambda i: (i, col))
    full = lambda a: pl.BlockSpec(a.shape, lambda i: (0, 0))
    return pl.pallas_call(
        functools.partial(_mixer_out_kernel, prompt_blocks=npb),
        out_shape=(jax.ShapeDtypeStruct((m, d), F32),
                   jax.ShapeDtypeStruct((m, d), F32),
                   jax.ShapeDtypeStruct((m, e), F32)),
        grid=(m // tm,),
        in_specs=[prompt, sample, prompt, sample, row(gate_blk0), row(gate_blk0 + 1), row(gate_blk0 + 2),
                  prompt, sample, full(wo), full(g), full(wr), full(br)],
        out_specs=(row(0), row(0), pl.BlockSpec((tm, e), lambda i: (i, 0))),
        compiler_params=_params(("parallel",), VMEM_LIMIT_BYTES),
        name="mixer_out",
    )(oa_p, oa_s, or_p, or_s, big, big, big, xp, xs, wo, g, wr, br)


def _expert_kernel(tok_ref, ie_ref, ib_ref, inb_ref, nv_ref, hn_ref, wg_ref, wu_ref, bg_ref, bu_ref, wd_ref,
                   bd_ref, ye_ref, x32, y32, wgb, wub, wdb, sem_x, sem_out, *, rows, group, n_f, n_items,
                   n_blocks):
    k = pl.program_id(0)
    f = pl.program_id(1)
    nb = inb_ref[k]
    blk0 = ib_ref[k]
    n_valid = nv_ref[0]
    valid = k < n_valid
    has_next = k + 1 < n_valid
    last_f = f == n_f - 1
    cur = k % 2
    rs = rows * group
    d = y32.shape[1]
    per_block = rs // (n_f * group)

    def gather_copy(slot, buf, row):
        return pltpu.make_async_copy(hn_ref.at[pl.ds(tok_ref[slot], 1), :],
                                     x32.at[buf, pl.ds(row, 1), :], sem_x.at[buf])

    def out_copy(b):
        return pltpu.make_async_copy(y32.at[pl.ds(b * rows, rows), :],
                                     ye_ref.at[pl.ds((blk0 + b) * rows, rows), :], sem_out)

    @pl.when((k == 0) & (f == 0))
    def _():
        def body(r, _):
            gather_copy(blk0 * rows + r, 0, r).start()
            return 0
        lax.fori_loop(0, rs, body, 0, unroll=8)

    @pl.when(valid & (f == 0))
    def _():
        def drain(r, _):
            gather_copy(0, cur, 0).wait()
            return 0
        lax.fori_loop(0, rs, drain, 0, unroll=8)

        def init(b, _):
            y32[pl.ds(pl.multiple_of(b * rows, rows), rows), :] = jnp.broadcast_to(bd_ref[...], (rows, d))
            return 0
        lax.fori_loop(0, nb, init, 0)

    @pl.when(valid)
    def _():
        wgb[...] = wg_ref[...].astype(BF16)
        wub[...] = wu_ref[...].astype(BF16)
        wdb[...] = wd_ref[...].astype(BF16)

    def prefetch(b, next_slot0):
        row0 = (f * group + b) * per_block
        for r in range(per_block):
            gather_copy(next_slot0 + row0 + r, 1 - cur, row0 + r).start()

    def compute_blocks(b0, n, next_slot0):
        r = pl.ds(pl.multiple_of(b0 * rows, rows), n * rows)
        x = x32[cur, r, :].astype(BF16)
        gate = jnp.minimum(jnp.dot(x, wgb[...], preferred_element_type=F32) + bg_ref[...], SWIGLU_LIMIT)
        up = jnp.clip(jnp.dot(x, wub[...], preferred_element_type=F32) + bu_ref[...],
                      -SWIGLU_LIMIT, SWIGLU_LIMIT)
        h = (up + 1.0) * (gate * _sigmoid(gate * SWIGLU_ALPHA))
        y32[r, :] += jnp.dot(h.astype(BF16), wdb[...], preferred_element_type=F32)
        if next_slot0 is not None:
            for i in range(n):
                prefetch(b0 + i, next_slot0)

        @pl.when(last_f)
        def _():
            for i in range(n):
                out_copy(b0 + i).start()

    def run(next_slot0):
        def pair(p, _):
            compute_blocks(2 * p, 2, next_slot0)
            return 0
        lax.fori_loop(0, nb // 2, pair, 0)

        @pl.when(nb % 2 == 1)
        def _():
            compute_blocks(nb - 1, 1, next_slot0)

    @pl.when(has_next)
    def _():
        next_slot0 = ib_ref[k + 1] * rows
        run(next_slot0)

        def rest(b, _):
            prefetch(b, next_slot0)
            return 0
        lax.fori_loop(nb, group, rest, 0)

    @pl.when(valid & jnp.logical_not(has_next))
    def _():
        run(None)

    @pl.when(valid & last_f)
    def _():
        def body(b, _):
            out_copy(b).wait()
            return 0
        lax.fori_loop(0, nb, body, 0)

    @pl.when((k == n_items - 1) & last_f)
    def _():
        used = nv_ref[1]
        y32[0:rows, :] = jnp.zeros((rows, d), F32)

        def fill(b):
            return pltpu.make_async_copy(y32.at[pl.ds(0, rows), :], ye_ref.at[pl.ds(b * rows, rows), :], sem_out)

        def start(b, _):
            fill(b).start()
            return 0

        def wait(b, _):
            fill(b).wait()
            return 0
        lax.fori_loop(used, n_blocks, start, 0)
        lax.fori_loop(used, n_blocks, wait, 0)


def expert_mlp(slot_tok, items, hn, w_gate_up, b_gate_up, w_down, b_down, n_items, n_blocks):
    item_e, item_b0, item_nb, counts = items
    n_exp, d, two_f = w_gate_up.shape
    d_ff = two_f // 2
    tf = MOE_FTILE
    n_f = d_ff // tf
    rows, group = MOE_ROWS, MOE_GROUP
    rs = rows * group
    assert rs % (n_f * group) == 0

    def fi(k, f, nv):
        return jnp.where(k < nv[0], f, n_f - 1)

    in_specs = [
        pl.BlockSpec(memory_space=pl.ANY),
        pl.BlockSpec((None, d, tf), lambda k, f, tok, ie, ib, inb, nv: (ie[k], 0, fi(k, f, nv))),
        pl.BlockSpec((None, d, tf), lambda k, f, tok, ie, ib, inb, nv: (ie[k], 0, n_f + fi(k, f, nv))),
        pl.BlockSpec((None, 1, tf), lambda k, f, tok, ie, ib, inb, nv: (ie[k], 0, fi(k, f, nv))),
        pl.BlockSpec((None, 1, tf), lambda k, f, tok, ie, ib, inb, nv: (ie[k], 0, n_f + fi(k, f, nv))),
        pl.BlockSpec((None, tf, d), lambda k, f, tok, ie, ib, inb, nv: (ie[k], fi(k, f, nv), 0)),
        pl.BlockSpec((None, 1, d), lambda k, f, tok, ie, ib, inb, nv: (ie[k], 0, 0)),
    ]
    return pl.pallas_call(
        functools.partial(_expert_kernel, rows=rows, group=group, n_f=n_f, n_items=n_items, n_blocks=n_blocks),
        out_shape=jax.ShapeDtypeStruct((n_blocks * rows, d), F32),
        grid_spec=pltpu.PrefetchScalarGridSpec(
            num_scalar_prefetch=5, grid=(n_items, n_f),
            in_specs=in_specs,
            out_specs=pl.BlockSpec(memory_space=pl.ANY),
            scratch_shapes=[pltpu.VMEM((2, rs, d), F32), pltpu.VMEM((rs, d), F32),
                            pltpu.VMEM((d, tf), BF16), pltpu.VMEM((d, tf), BF16), pltpu.VMEM((tf, d), BF16),
                            pltpu.SemaphoreType.DMA((2,)), pltpu.SemaphoreType.DMA]),
        compiler_params=_params(("arbitrary", "arbitrary"), VMEM_LIMIT_BYTES),
        name="moe_experts",
    )(slot_tok, item_e, item_b0, item_nb, counts, hn, w_gate_up, w_gate_up,
      b_gate_up.reshape(n_exp, 1, two_f), b_gate_up.reshape(n_exp, 1, two_f),
      w_down, b_down.reshape(n_exp, 1, d))


def _combine_kernel(pos_ref, ye_ref, w_ref, h_ref, g_ref, o_ref, buf, sems, *, tc, blk0, n_steps, final_norm):
    i = pl.program_id(0)
    n_rows = TOP_K * tc

    def copy(r, src_row, par):
        return pltpu.make_async_copy(ye_ref.at[pl.ds(src_row, 1), :], buf.at[par, pl.ds(r, 1), :],
                                     sems.at[par])

    def issue(step, par):
        def body(r, _):
            copy(r, pos_ref[(blk0 + step) * n_rows + r], par).start()
            return 0
        lax.fori_loop(0, n_rows, body, 0, unroll=8)

    @pl.when(i == 0)
    def _():
        issue(0, 0)

    @pl.when(i + 1 < n_steps)
    def _():
        issue(i + 1, (i + 1) % 2)

    par = i % 2

    def drain(r, _):
        copy(0, 0, par).wait()
        return 0
    lax.fori_loop(0, n_rows, drain, 0, unroll=8)

    w = w_ref[...]
    y = h_ref[...]
    for k in range(TOP_K):
        y = y + w[:, k:k + 1] * buf[par, k * tc:(k + 1) * tc, :]
    o_ref[...] = _rms(y, g_ref[...]) if final_norm else y


def combine(pos, ye, top_w, h, g, row0, rows, final_norm, tc=COMBINE_ROWS):
    d = h.shape[1]
    blk0 = row0 // tc
    n_steps = rows // tc
    return pl.pallas_call(
        functools.partial(_combine_kernel, tc=tc, blk0=blk0, n_steps=n_steps, final_norm=final_norm),
        out_shape=jax.ShapeDtypeStruct((rows, d), F32),
        grid_spec=pltpu.PrefetchScalarGridSpec(
            num_scalar_prefetch=1, grid=(n_steps,),
            in_specs=[pl.BlockSpec(memory_space=pl.ANY),
                      pl.BlockSpec((tc, TOP_K), lambda i, p: (blk0 + i, 0)),
                      pl.BlockSpec((tc, d), lambda i, p: (blk0 + i, 0)),
                      pl.BlockSpec((1, d), lambda i, p: (0, 0))],
            out_specs=pl.BlockSpec((tc, d), lambda i, p: (i, 0)),
            scratch_shapes=[pltpu.VMEM((2, TOP_K * tc, d), F32), pltpu.SemaphoreType.DMA((2,))]),
        compiler_params=_params(("arbitrary",), VMEM_LIMIT_BYTES),
        name="moe_combine",
    )(pos, ye, top_w, h, g)


def route(logits, tc):
    m, n_exp = logits.shape
    a = m * TOP_K
    rows, group = MOE_ROWS, MOE_GROUP
    top_logit, top_idx = lax.top_k(logits, TOP_K)
    top_w = jax.nn.softmax(top_logit, axis=-1)
    flat_e = top_idx.reshape(a)
    onehot = (flat_e[:, None] == jnp.arange(n_exp, dtype=jnp.int32)[None, :]).astype(jnp.int32)
    csum = jnp.cumsum(onehot, axis=0)
    counts = csum[-1]
    rank = jnp.take_along_axis(csum, flat_e[:, None], axis=1)[:, 0] - 1
    nb_e = (counts + rows - 1) // rows
    blk_start = jnp.cumsum(nb_e) - nb_e
    dest = blk_start[flat_e] * rows + rank
    n_blocks = a // rows + n_exp
    slot_tok = jnp.zeros(((n_blocks + group) * rows,), jnp.int32).at[dest].set(
        jnp.arange(a, dtype=jnp.int32) // TOP_K)
    ni_e = (nb_e + group - 1) // group
    item_end = jnp.cumsum(ni_e)
    n_valid = item_end[-1]
    n_items = (n_blocks + n_exp * (group - 1)) // group
    kk = jnp.arange(n_items, dtype=jnp.int32)
    item_e = jnp.minimum(jnp.sum(item_end[None, :] <= kk[:, None], axis=1), n_exp - 1).astype(jnp.int32)
    local = kk - (item_end - ni_e)[item_e]
    valid = kk < n_valid
    item_e = jnp.where(valid, item_e, item_e[jnp.maximum(n_valid - 1, 0)])
    item_b0 = jnp.where(valid, blk_start[item_e] + local * group, 0).astype(jnp.int32)
    item_nb = jnp.where(valid, jnp.clip(nb_e[item_e] - local * group, 0, group), 0).astype(jnp.int32)
    pos = dest.reshape(m // tc, tc, TOP_K).transpose(0, 2, 1).reshape(a)
    counts = jnp.stack([n_valid, jnp.sum(nb_e)]).astype(jnp.int32)
    items = (item_e, item_b0, item_nb, counts)
    return top_w, slot_tok, pos.astype(jnp.int32), items, n_items, n_blocks


def _rope_tables(pos, half):
    inv = jnp.power(ROPE_THETA, -jnp.arange(half, dtype=F32) / half)
    ang = pos.astype(F32)[:, None] * inv[None, :]
    return jnp.cos(ang), jnp.sin(ang)


def kernel(x_prompt, x_sample, cache_latent, cache_k_rope, state_retention, g_mix_norm, w_in, g_q_norm, w_uq, g_kv_norm, w_uk, w_uv, w_o, g_ffn_norm, w_router, b_router, w_gate_up, b_gate_up, w_down, b_down, g_final):
    bp, t, d = x_prompt.shape
    db, ts, _ = x_sample.shape
    depth = w_in.shape[0]
    past = cache_latent.shape[2]
    ql, ha, qk = w_uq.shape[1:]
    kl = w_uk.shape[1]
    rope = cache_k_rope.shape[-1]
    nope = qk - rope
    hr, dk = state_retention.shape[2:4]
    assert bp == 1 and nope == LANES and w_uv.shape[-1] == LANES and 2 * rope == LANES
    assert past % CHUNK == 0 and ts <= CHUNK
    mp, ms = bp * t, db * ts
    m = mp + ms
    tk_s = past + ts
    mla_scale = float(qk) ** -0.5 * 1.4426950408889634

    pos = jnp.concatenate([jnp.arange(t, dtype=jnp.int32),
                           jnp.tile(past + jnp.arange(ts, dtype=jnp.int32), db)])
    cos_a, sin_a = _rope_tables(pos, rope // 2)
    z = lambda w: jnp.zeros((m, w), F32)
    rope_c = jnp.concatenate([cos_a, cos_a, z(LANES - rope)], axis=1)
    rope_sa = jnp.concatenate([z(rope // 2), sin_a, z(LANES - rope)], axis=1)
    rope_sb = jnp.concatenate([-sin_a, z(LANES - rope // 2)], axis=1)
    cos_r, sin_r = _rope_tables(pos, dk // 2)
    tabs_p = retention_tables(RET_CHUNK, hr, dk)
    tabs_s = retention_tables(ts, hr, dk)

    xp, xs = x_prompt.reshape(mp, d), x_sample.reshape(ms, d)
    lat_p, kr_p, st_p, lat_s, kr_s, st_s = [], [], [], [], [], []
    small = ql + kl + rope
    for l in range(depth):
        wa = jnp.pad(w_in[l][:, :small], ((0, 0), (0, LANES - rope))).astype(BF16)
        wb = w_in[l][:, small:].astype(BF16)
        wq = jnp.pad(w_uq[l], ((0, 0), (0, 0), (0, 2 * LANES - qk))).reshape(ql, ha * 2 * LANES).astype(BF16)
        wk = w_uk[l].reshape(kl, ha * nope).astype(BF16)
        wv = w_uv[l].reshape(kl, ha * LANES).astype(BF16)
        wo = w_o[l].astype(BF16)

        xn = norm_cast(xp, xs, g_mix_norm[l].reshape(1, d))
        cqn, ckv, ckv_b, kr, kr_pad = proj_small(xn, wa, g_q_norm[l].reshape(1, ql), g_kv_norm[l].reshape(1, kl),
                                                 rope_c, rope_sa, rope_sb, ql, kl, rope)
        big = proj_big(xn, wb, cos_r, sin_r, 2 * hr * dk, dk, float(dk) ** -0.5)
        qcat = q_proj(cqn, wq, rope_c, rope_sa, rope_sb, ha, mla_scale)

        k_p, vt_p = kv_expand(ckv_b, kr_pad, wk, wv.T, mp, ha)
        oa_p = attention_causal(qcat, k_p, vt_p, ha, t=mp, tq=512)
        c_all = jnp.concatenate([cache_latent[l], ckv[mp:].reshape(db, ts, kl)], axis=1)
        kr_all = jnp.concatenate([cache_k_rope[l], kr[mp:].reshape(db, ts, rope)], axis=1)
        kr_all = jnp.pad(kr_all, ((0, 0), (0, 0), (0, LANES - rope)))
        oa_s = attention_absorbed(qcat, c_all.reshape(db * tk_s, kl).astype(BF16),
                                  kr_all.reshape(db * tk_s, LANES).astype(BF16),
                                  w_uk[l].transpose(1, 2, 0).astype(BF16), w_uv[l].transpose(1, 0, 2).astype(BF16),
                                  ha, batch=db, tq=ts, q_row0=mp, tk=tk_s)

        or_p, s_p = retention(big, jnp.zeros((bp, hr, dk, dk), F32), tabs_p, hr, dk,
                              batch=bp, t_total=t, row0=0, chunk=RET_CHUNK)
        or_s, s_s = retention(big, state_retention[l].astype(F32), tabs_s, hr, dk,
                              batch=db, t_total=ts, row0=mp, chunk=ts)

        h1, hn, logits = mixer_out(oa_p, oa_s, or_p, or_s, big, 3, xp, xs, wo, g_ffn_norm[l].reshape(1, d),
                                   w_router[l], b_router[l].reshape(1, -1))

        top_w, slot_tok, pos_flat, items, n_items, n_blocks = route(logits, COMBINE_ROWS)
        ye = expert_mlp(slot_tok, items, hn, w_gate_up[l], b_gate_up[l], w_down[l], b_down[l], n_items, n_blocks)
        last = l == depth - 1
        g_out = g_final.reshape(1, d)
        y_p = combine(pos_flat, ye, top_w, h1, g_out, 0, mp, last)
        y_s = combine(pos_flat, ye, top_w, h1, g_out, mp, ms, last)
        xp, xs = y_p, y_s

        lat_p.append(ckv[:mp].reshape(bp, t, kl))
        kr_p.append(kr[:mp].reshape(bp, t, rope))
        st_p.append(s_p)
        lat_s.append(ckv[mp:].reshape(db, ts, kl))
        kr_s.append(kr[mp:].reshape(db, ts, rope))
        st_s.append(s_s)

    return (y_p.reshape(bp, t, d), y_s.reshape(db, ts, d),
            jnp.stack(lat_p), jnp.stack(kr_p), jnp.stack(st_p),
            jnp.stack(lat_s), jnp.stack(kr_s), jnp.stack(st_s))
```

```python
import functools

import jax
import jax.numpy as jnp
from jax import lax
from jax.experimental import pallas as pl
from jax.experimental.pallas import tpu as pltpu

F32 = jnp.float32
BF16 = jnp.bfloat16

NORM_EPS = 1e-6
ROPE_THETA = 10000.0
CHUNK = 64
TOP_K = 4
SWIGLU_LIMIT = 7.0
SWIGLU_ALPHA = 1.702

LANES = 128
VMEM_LIMIT_BYTES = 56 * 1024 * 1024

RET_CHUNK = 256
MOE_ROWS = 256
MOE_GROUP = 5
MOE_FTILE = 256
COMBINE_ROWS = 128
MIXER_SUBTILES = 2
ABSORBED_HEADS = 4


def _params(sem, vmem=None):
    return pltpu.CompilerParams(dimension_semantics=sem, vmem_limit_bytes=vmem)


def _rms(x, g):
    return x * lax.rsqrt(jnp.mean(x * x, axis=-1, keepdims=True) + NORM_EPS) * g


def _sigmoid(x):
    return 1.0 / (1.0 + jnp.exp(-x))


def _group_specs(rows_p, rows_s, tm, d):
    npb, nsb = rows_p // tm, rows_s // tm
    prompt = pl.BlockSpec((tm, d), lambda i: (jnp.minimum(i, npb - 1), 0))
    sample = pl.BlockSpec((tm, d), lambda i: (jnp.clip(i - npb, 0, nsb - 1), 0))
    return npb, prompt, sample


def _norm_cast_kernel(xp_ref, xs_ref, g_ref, o_ref, *, prompt_blocks):
    x = jnp.where(pl.program_id(0) < prompt_blocks, xp_ref[...], xs_ref[...])
    o_ref[...] = _rms(x, g_ref[...]).astype(o_ref.dtype)


def norm_cast(xp, xs, g, tm=512):
    d = xp.shape[1]
    m = xp.shape[0] + xs.shape[0]
    npb, prompt, sample = _group_specs(xp.shape[0], xs.shape[0], tm, d)
    return pl.pallas_call(
        functools.partial(_norm_cast_kernel, prompt_blocks=npb),
        out_shape=jax.ShapeDtypeStruct((m, d), BF16),
        grid=(m // tm,),
        in_specs=[prompt, sample, pl.BlockSpec((1, d), lambda i: (0, 0))],
        out_specs=pl.BlockSpec((tm, d), lambda i: (i, 0)),
        compiler_params=_params(("parallel",), VMEM_LIMIT_BYTES),
        name="norm_cast",
    )(xp, xs, g)


def _cast_w_in_kernel(w_ref, wa_ref, wb_ref, *, small):
    aligned = small // LANES * LANES
    wa_ref[:, :aligned] = w_ref[:, :aligned].astype(BF16)
    tail = w_ref[:, aligned:aligned + LANES]
    keep = lax.broadcasted_iota(jnp.int32, tail.shape, 1) < small - aligned
    wa_ref[:, aligned:] = jnp.where(keep, tail, 0.0).astype(BF16)
    wb_ref[...] = w_ref[:, small:].astype(BF16)


def cast_w_in(w, small, tm=128):
    d, n = w.shape
    na = -(-small // LANES) * LANES
    return pl.pallas_call(
        functools.partial(_cast_w_in_kernel, small=small),
        out_shape=(jax.ShapeDtypeStruct((d, na), BF16), jax.ShapeDtypeStruct((d, n - small), BF16)),
        grid=(d // tm,),
        in_specs=[pl.BlockSpec((tm, n), lambda i: (i, 0))],
        out_specs=(pl.BlockSpec((tm, na), lambda i: (i, 0)), pl.BlockSpec((tm, n - small), lambda i: (i, 0))),
        compiler_params=_params(("parallel",), VMEM_LIMIT_BYTES),
        name="cast_w_in",
    )(w)


def _rope_hi(x, c, sa, sb):
    return x * c + pltpu.roll(x, 32, 1) * sa + pltpu.roll(x, 96, 1) * sb


def _proj_small_kernel(x_ref, w_ref, gq_ref, gkv_ref, c_ref, sa_ref, sb_ref,
                       cqn_ref, ckv_ref, ckvb_ref, kr_ref, krp_ref, *, ql, kl, rope):
    acc = jnp.dot(x_ref[...], w_ref[...], preferred_element_type=F32)
    cqn_ref[...] = _rms(acc[:, :ql], gq_ref[...]).astype(cqn_ref.dtype)
    ckv = _rms(acc[:, ql:ql + kl], gkv_ref[...])
    ckv_ref[...] = ckv
    ckvb_ref[...] = ckv.astype(ckvb_ref.dtype)
    kr = _rope_hi(acc[:, ql + kl:ql + kl + LANES], c_ref[...], sa_ref[...], sb_ref[...])
    kr_ref[...] = kr[:, :rope]
    krp_ref[...] = kr.astype(krp_ref.dtype)


def proj_small(xn, wa, gq, gkv, c, sa, sb, ql, kl, rope, tm=256):
    m, d = xn.shape
    n = wa.shape[1]
    row = lambda w: pl.BlockSpec((tm, w), lambda i: (i, 0))
    full = lambda a: pl.BlockSpec(a.shape, lambda i: (0, 0))
    return pl.pallas_call(
        functools.partial(_proj_small_kernel, ql=ql, kl=kl, rope=rope),
        out_shape=(jax.ShapeDtypeStruct((m, ql), BF16),
                   jax.ShapeDtypeStruct((m, kl), F32),
                   jax.ShapeDtypeStruct((m, kl), BF16),
                   jax.ShapeDtypeStruct((m, rope), F32),
                   jax.ShapeDtypeStruct((m, LANES), BF16)),
        grid=(m // tm,),
        in_specs=[row(d), full(wa), full(gq), full(gkv), row(LANES), row(LANES), row(LANES)],
        out_specs=(row(ql), row(kl), row(kl), row(rope), row(LANES)),
        compiler_params=_params(("parallel",), VMEM_LIMIT_BYTES),
        name="proj_small",
    )(xn, wa, gq, gkv, c, sa, sb)


def _proj_big_kernel(x_ref, w_ref, cos_ref, sin_ref, o_ref, *, n_rope, dk, k_scale):
    j = pl.program_id(0)
    acc = jnp.dot(x_ref[...], w_ref[...], preferred_element_type=F32)

    @pl.when(j >= n_rope)
    def _():
        o_ref[...] = acc.astype(o_ref.dtype)

    @pl.when(j < n_rope)
    def _():
        scale = jnp.where(j >= n_rope // 2, k_scale, 1.0).astype(F32)
        c = cos_ref[...] * scale
        s = sin_ref[...] * scale
        half = dk // 2
        for h in range(acc.shape[1] // dk):
            x1 = acc[:, h * dk:h * dk + half]
            x2 = acc[:, h * dk + half:(h + 1) * dk]
            o_ref[:, h * dk:h * dk + half] = (x1 * c - x2 * s).astype(o_ref.dtype)
            o_ref[:, h * dk + half:(h + 1) * dk] = (x2 * c + x1 * s).astype(o_ref.dtype)


def proj_big(xn, wb, cos_r, sin_r, rope_cols, dk, k_scale, tm=512, tn=1024):
    m, d = xn.shape
    n = wb.shape[1]
    return pl.pallas_call(
        functools.partial(_proj_big_kernel, n_rope=rope_cols // tn, dk=dk, k_scale=k_scale),
        out_shape=jax.ShapeDtypeStruct((m, n), BF16),
        grid=(n // tn, m // tm),
        in_specs=[pl.BlockSpec((tm, d), lambda j, i: (i, 0)),
                  pl.BlockSpec((d, tn), lambda j, i: (0, j)),
                  pl.BlockSpec((tm, dk // 2), lambda j, i: (i, 0)),
                  pl.BlockSpec((tm, dk // 2), lambda j, i: (i, 0))],
        out_specs=pl.BlockSpec((tm, tn), lambda j, i: (i, j)),
        compiler_params=_params(("parallel", "parallel"), VMEM_LIMIT_BYTES),
        name="proj_big",
    )(xn, wb, cos_r, sin_r)


def _q_proj_kernel(x_ref, w_ref, c_ref, sa_ref, sb_ref, o_ref, *, heads, scale):
    acc = jnp.dot(x_ref[...], w_ref[...], preferred_element_type=F32)
    c = c_ref[...] * scale
    sa = sa_ref[...] * scale
    sb = sb_ref[...] * scale
    for h in range(heads):
        lo = 2 * h * LANES
        o_ref[:, lo:lo + LANES] = (acc[:, lo:lo + LANES] * scale).astype(o_ref.dtype)
        o_ref[:, lo + LANES:lo + 2 * LANES] = _rope_hi(
            acc[:, lo + LANES:lo + 2 * LANES], c, sa, sb).astype(o_ref.dtype)


def q_proj(cqn, wq, c, sa, sb, heads, scale, tm=256):
    m, k = cqn.shape
    n = wq.shape[1]
    row = lambda w: pl.BlockSpec((tm, w), lambda i: (i, 0))
    return pl.pallas_call(
        functools.partial(_q_proj_kernel, heads=heads, scale=scale),
        out_shape=jax.ShapeDtypeStruct((m, n), BF16),
        grid=(m // tm,),
        in_specs=[row(k), pl.BlockSpec((k, n), lambda i: (0, 0)), row(LANES), row(LANES), row(LANES)],
        out_specs=row(n),
        compiler_params=_params(("parallel",), VMEM_LIMIT_BYTES),
        name="q_proj",
    )(cqn, wq, c, sa, sb)


def _kv_expand_kernel(c_ref, kr_ref, wk_ref, wvt_ref, k_ref, vt_ref, *, heads):
    c = c_ref[...]
    kn = jnp.dot(c, wk_ref[...], preferred_element_type=F32)
    vt_ref[...] = lax.dot_general(wvt_ref[...], c, (((1,), (1,)), ((), ())),
                                  preferred_element_type=F32).astype(vt_ref.dtype)
    kr = kr_ref[...]
    for h in range(heads):
        k_ref[:, 2 * h * LANES:(2 * h + 1) * LANES] = kn[:, h * LANES:(h + 1) * LANES].astype(k_ref.dtype)
        k_ref[:, (2 * h + 1) * LANES:(2 * h + 2) * LANES] = kr


def kv_expand(c_bf, kr_pad, wk, wvt, rows, heads, tm=256):
    kl = c_bf.shape[1]
    dv = heads * LANES
    row = lambda w: pl.BlockSpec((tm, w), lambda i: (i, 0))
    full = lambda a: pl.BlockSpec(a.shape, lambda i: (0, 0))
    return pl.pallas_call(
        functools.partial(_kv_expand_kernel, heads=heads),
        out_shape=(jax.ShapeDtypeStruct((rows, 2 * heads * LANES), BF16),
                   jax.ShapeDtypeStruct((dv, rows), BF16)),
        grid=(rows // tm,),
        in_specs=[row(kl), row(LANES), full(wk), full(wvt)],
        out_specs=(row(2 * heads * LANES), pl.BlockSpec((dv, tm), lambda i: (0, i))),
        compiler_params=_params(("parallel",), VMEM_LIMIT_BYTES),
        name="kv_expand",
    )(c_bf, kr_pad, wk, wvt)


def _attn_absorbed_kernel(q_ref, c_ref, kr_ref, wukt_ref, wuv_ref, o_ref, *, heads):
    tq = q_ref.shape[0]
    c = c_ref[...]
    kr = kr_ref[...]
    nt = (((1,), (1,)), ((), ()))
    for g in range(heads // ABSORBED_HEADS):
        hs = range(g * ABSORBED_HEADS, (g + 1) * ABSORBED_HEADS)
        q_lat = jnp.concatenate(
            [jnp.dot(q_ref[:, 2 * h * LANES:(2 * h + 1) * LANES], wukt_ref[h],
                     preferred_element_type=F32).astype(BF16) for h in hs], axis=0)
        q_rope = jnp.concatenate([q_ref[:, (2 * h + 1) * LANES:(2 * h + 2) * LANES] for h in hs], axis=0)
        s = (lax.dot_general(q_lat, c, nt, preferred_element_type=F32)
             + lax.dot_general(q_rope, kr, nt, preferred_element_type=F32))
        p = jnp.exp2(s - jnp.max(s, axis=-1, keepdims=True))
        l = jnp.sum(p, axis=-1, keepdims=True)
        o_lat = (jnp.dot(p.astype(BF16), c, preferred_element_type=F32) / l).astype(BF16)
        for i, h in enumerate(hs):
            o_ref[:, h * LANES:(h + 1) * LANES] = jnp.dot(
                o_lat[i * tq:(i + 1) * tq], wuv_ref[h], preferred_element_type=F32).astype(o_ref.dtype)


def attention_absorbed(q, c_all, kr_all, wukt, wuv, heads, *, batch, tq, q_row0, tk):
    assert heads % ABSORBED_HEADS == 0 and q_row0 % tq == 0
    kl = c_all.shape[1]
    qb0 = q_row0 // tq
    full = lambda a: pl.BlockSpec(a.shape, lambda b: (0, 0, 0))
    return pl.pallas_call(
        functools.partial(_attn_absorbed_kernel, heads=heads),
        out_shape=jax.ShapeDtypeStruct((batch * tq, heads * LANES), BF16),
        grid=(batch,),
        in_specs=[pl.BlockSpec((tq, 2 * heads * LANES), lambda b: (qb0 + b, 0)),
                  pl.BlockSpec((tk, kl), lambda b: (b, 0)),
                  pl.BlockSpec((tk, LANES), lambda b: (b, 0)),
                  full(wukt), full(wuv)],
        out_specs=pl.BlockSpec((tq, heads * LANES), lambda b: (b, 0)),
        compiler_params=_params(("parallel",), VMEM_LIMIT_BYTES),
        name="attention_absorbed",
    )(q, c_all, kr_all, wukt, wuv)


def _attn_t_kernel(q_ref, k_ref, vt_ref, o_ref, sa_ref, sb_ref, *, tq, tk):
    i = pl.program_id(1)
    dq, dv = 2 * LANES, LANES
    n_heads = q_ref.shape[1] // dq

    def scores(j, s_ref):
        off = pl.multiple_of(j * tk, tk)
        for h in range(n_heads):
            q = q_ref[:, h * dq:(h + 1) * dq]
            k = k_ref[pl.ds(off, tk), h * dq:(h + 1) * dq]
            s_ref[h] = lax.dot_general(k, q, (((1,), (1,)), ((), ())), preferred_element_type=F32)

    def softmax_pv(j, s_ref, carries, masked):
        off = pl.multiple_of(j * tk, tk)
        out = []
        for h in range(n_heads):
            m, l, acc = carries[h]
            s = s_ref[h]
            if masked:
                shift = CHUNK.bit_length() - 1
                kc = lax.shift_right_logical(j * tk + lax.broadcasted_iota(jnp.int32, s.shape, 0), shift)
                qc = lax.shift_right_logical(i * tq + lax.broadcasted_iota(jnp.int32, s.shape, 1), shift)
                s = jnp.where(kc <= qc, s, -1e30)
            m_new = jnp.maximum(m, jnp.max(s, axis=0, keepdims=True))
            alpha = jnp.exp2(m - m_new)
            p = jnp.exp2(s - m_new)
            l = alpha * l + jnp.sum(p, axis=0, keepdims=True)
            vt = vt_ref[h * dv:(h + 1) * dv, pl.ds(off, tk)]
            acc = alpha * acc + jnp.dot(vt, p.astype(BF16), preferred_element_type=F32)
            out.append((m_new, l, acc))
        return tuple(out)

    def finish(carries):
        for h, (_, l, acc) in enumerate(carries):
            o_ref[:, h * dv:(h + 1) * dv] = (acc / l).T.astype(o_ref.dtype)

    scores(0, sa_ref)

    def pair(t, carries):
        scores(2 * t + 1, sb_ref)
        carries = softmax_pv(2 * t, sa_ref, carries, False)
        scores(2 * t + 2, sa_ref)
        return softmax_pv(2 * t + 1, sb_ref, carries, False)

    init = (jnp.full((1, tq), -jnp.inf, F32), jnp.zeros((1, tq), F32), jnp.zeros((dv, tq), F32))
    carries = lax.fori_loop(0, i // 2, pair, (init,) * n_heads)

    @pl.when(i % 2 == 0)
    def _():
        finish(softmax_pv(i, sa_ref, carries, True))

    @pl.when(i % 2 == 1)
    def _():
        scores(i, sb_ref)
        finish(softmax_pv(i, sb_ref, softmax_pv(i - 1, sa_ref, carries, False), True))


def attention_causal(q, k, vt, heads, *, t, tq, hp=2):
    assert CHUNK & (CHUNK - 1) == 0 and tq % CHUNK == 0 and heads % hp == 0
    dq = 2 * LANES
    return pl.pallas_call(
        functools.partial(_attn_t_kernel, tq=tq, tk=tq),
        out_shape=jax.ShapeDtypeStruct((t, heads * LANES), BF16),
        grid=(heads // hp, t // tq),
        in_specs=[pl.BlockSpec((tq, hp * dq), lambda h, i: (i, h)),
                  pl.BlockSpec((t, hp * dq), lambda h, i: (0, h)),
                  pl.BlockSpec((hp * LANES, t), lambda h, i: (h, 0))],
        out_specs=pl.BlockSpec((tq, hp * LANES), lambda h, i: (i, h)),
        scratch_shapes=[pltpu.VMEM((hp, tq, tq), F32), pltpu.VMEM((hp, tq, tq), F32)],
        compiler_params=_params(("parallel", "arbitrary"), VMEM_LIMIT_BYTES),
        name="attention_causal",
    )(q, k, vt)


def _retention_kernel(q_ref, k_ref, v_ref, s0_ref, d_ref, xi_ref, zeta_ref, gc_ref,
                      o_ref, sout_ref, s_scr, *, n_chunks):
    c = pl.program_id(2)
    dk = s_scr.shape[-1]

    @pl.when(c == 0)
    def _():
        s_scr[...] = s0_ref[...]

    for h in range(s_scr.shape[0]):
        cols = slice(h * dk, (h + 1) * dk)
        q = q_ref[:, cols]
        k = k_ref[:, cols]
        v = v_ref[:, cols]
        s = s_scr[h]
        a = lax.dot_general(q, k, (((1,), (1,)), ((), ())), preferred_element_type=F32) * d_ref[h]
        o = jnp.dot(a.astype(BF16), v, preferred_element_type=F32)
        o = o + jnp.dot(q, s.astype(BF16), preferred_element_type=F32) * xi_ref[h]
        kz = (k.astype(F32) * zeta_ref[h]).astype(BF16)
        s_new = s * gc_ref[h] + lax.dot_general(kz, v, (((0,), (0,)), ((), ())),
                                                preferred_element_type=F32)
        s_scr[h] = s_new
        o_ref[:, cols] = (o * lax.rsqrt(jnp.mean(o * o, axis=-1, keepdims=True) + NORM_EPS)).astype(o_ref.dtype)

    @pl.when(c == n_chunks - 1)
    def _():
        sout_ref[...] = s_scr[...]


def retention(big, s0, tabs, heads, dk, *, batch, t_total, row0, chunk, hp=4):
    d_tab, xi_tab, zeta_tab, gc_tab = tabs
    assert heads % hp == 0
    nc = t_total // chunk
    rb0 = row0 // chunk
    hg = heads // hp
    rowblk = lambda g0: pl.BlockSpec((chunk, hp * dk), lambda b, h, c: (rb0 + b * nc + c, g0 + h))
    per_head = lambda a: pl.BlockSpec((hp,) + a.shape[1:], lambda b, h, c: (h, 0, 0))
    st_spec = pl.BlockSpec((None, hp, dk, dk), lambda b, h, c: (b, h, 0, 0))
    return pl.pallas_call(
        functools.partial(_retention_kernel, n_chunks=nc),
        out_shape=(jax.ShapeDtypeStruct((batch * t_total, heads * dk), BF16),
                   jax.ShapeDtypeStruct((batch, heads, dk, dk), F32)),
        grid=(batch, hg, nc),
        in_specs=[rowblk(0), rowblk(hg), rowblk(2 * hg), st_spec,
                  per_head(d_tab), per_head(xi_tab), per_head(zeta_tab), per_head(gc_tab)],
        out_specs=(pl.BlockSpec((chunk, hp * dk), lambda b, h, c: (b * nc + c, h)), st_spec),
        scratch_shapes=[pltpu.VMEM((hp, dk, dk), F32)],
        compiler_params=_params(("parallel", "parallel", "arbitrary"), VMEM_LIMIT_BYTES),
        name="retention",
    )(big, big, big, s0, d_tab, xi_tab, zeta_tab, gc_tab)


def retention_tables(length, heads, dk):
    lg = jnp.log1p(-jnp.exp2(-5.0 - jnp.arange(heads, dtype=F32)))
    idx = jnp.arange(length, dtype=F32)
    diff = idx[:, None] - idx[None, :]
    decay = jnp.where(diff >= 0, jnp.exp(jnp.maximum(diff, 0.0)[None] * lg[:, None, None]), 0.0)
    xi = jnp.exp((idx + 1.0)[None, :] * lg[:, None])
    zeta = jnp.exp((length - 1.0 - idx)[None, :] * lg[:, None])
    gc = jnp.exp(length * lg)
    bc = lambda a: jnp.broadcast_to(a[:, :, None], (heads, a.shape[1], dk))
    return decay, bc(xi), bc(zeta), jnp.broadcast_to(gc[:, None, None], (heads, 1, dk))


def _mixer_out_kernel(oap_ref, oas_ref, orp_ref, ors_ref, rg_ref, ga_ref, gb_ref, xp_ref, xs_ref, wo_ref,
                      g_ref, wr_ref, br_ref, h_ref, hn_ref, lg_ref, *, prompt_blocks):
    is_prompt = pl.program_id(0) < prompt_blocks
    tm = h_ref.shape[0]
    sub = tm // MIXER_SUBTILES
    for s in range(MIXER_SUBTILES):
        r = slice(s * sub, (s + 1) * sub)
        oa = jnp.where(is_prompt, oap_ref[r, :], oas_ref[r, :]).astype(F32)
        orn = jnp.where(is_prompt, orp_ref[r, :], ors_ref[r, :]).astype(F32)
        x = jnp.where(is_prompt, xp_ref[r, :], xs_ref[r, :])
        rg = rg_ref[r, :].astype(F32)
        yb = rg * _sigmoid(rg) * orn
        merged = _sigmoid(ga_ref[r, :].astype(F32)) * oa + _sigmoid(gb_ref[r, :].astype(F32)) * yb
        h = x + jnp.dot(merged.astype(BF16), wo_ref[...], preferred_element_type=F32)
        h_ref[r, :] = h
        hn = _rms(h, g_ref[...])
        hn_ref[r, :] = hn
        lg_ref[r, :] = jnp.dot(hn, wr_ref[...], preferred_element_type=F32,
                               precision=lax.Precision.HIGHEST) + br_ref[...]


def mixer_out(oa_p, oa_s, or_p, or_s, big, gate_blk0, xp, xs, wo, g, wr, br, tm=256):
    d = xp.shape[1]
    m = xp.shape[0] + xs.shape[0]
    e = wr.shape[1]
    npb, prompt, sample = _group_specs(xp.shape[0], xs.shape[0], tm, d)
    row = lambda col: pl.BlockSpec((tm, d), lambda i: (i, col))
    full = lambda a: pl.BlockSpec(a.shape, lambda i: (0, 0))
    return pl.pallas_call(
        functools.partial(_mixer_out_kernel, prompt_blocks=npb),
        out_shape=(jax.ShapeDtypeStruct((m, d), F32),
                   jax.ShapeDtypeStruct((m, d), F32),
                   jax.ShapeDtypeStruct((m, e), F32)),
        grid=(m // tm,),
        in_specs=[prompt, sample, prompt, sample, row(gate_blk0), row(gate_blk0 + 1), row(gate_blk0 + 2),
                  prompt, sample, full(wo), full(g), full(wr), full(br)],
        out_specs=(row(0), row(0), pl.BlockSpec((tm, e), lambda i: (i, 0))),
        compiler_params=_params(("parallel",), VMEM_LIMIT_BYTES),
        name="mixer_out",
    )(oa_p, oa_s, or_p, or_s, big, big, big, xp, xs, wo, g, wr, br)


def _expert_kernel(tok_ref, ie_ref, ib_ref, inb_ref, nv_ref, hn_ref, wg_ref, wu_ref, bg_ref, bu_ref, wd_ref,
                   bd_ref, ye_ref, x32, y32, wgb, wub, wdb, sem_x, sem_out, *, rows, group, n_f, n_items,
                   n_blocks):
    k = pl.program_id(0)
    f = pl.program_id(1)
    nb = inb_ref[k]
    blk0 = ib_ref[k]
    n_valid = nv_ref[0]
    valid = k < n_valid
    has_next = k + 1 < n_valid
    last_f = f == n_f - 1
    cur = k % 2
    rs = rows * group
    d = y32.shape[1]
    per_block = rs // (n_f * group)

    def gather_copy(slot, buf, row):
        return pltpu.make_async_copy(hn_ref.at[pl.ds(tok_ref[slot], 1), :],
                                     x32.at[buf, pl.ds(row, 1), :], sem_x.at[buf])

    def out_copy(b):
        return pltpu.make_async_copy(y32.at[pl.ds(b * rows, rows), :],
                                     ye_ref.at[pl.ds((blk0 + b) * rows, rows), :], sem_out)

    @pl.when((k == 0) & (f == 0))
    def _():
        def body(r, _):
            gather_copy(blk0 * rows + r, 0, r).start()
            return 0
        lax.fori_loop(0, rs, body, 0, unroll=8)

    @pl.when(valid & (f == 0))
    def _():
        def drain(r, _):
            gather_copy(0, cur, 0).wait()
            return 0
        lax.fori_loop(0, rs, drain, 0, unroll=8)

        def init(b, _):
            y32[pl.ds(pl.multiple_of(b * rows, rows), rows), :] = jnp.broadcast_to(bd_ref[...], (rows, d))
            return 0
        lax.fori_loop(0, nb, init, 0)

    @pl.when(valid)
    def _():
        wgb[...] = wg_ref[...].astype(BF16)
        wub[...] = wu_ref[...].astype(BF16)
        wdb[...] = wd_ref[...].astype(BF16)

    def prefetch(b, next_slot0):
        row0 = (f * group + b) * per_block
        for r in range(per_block):
            gather_copy(next_slot0 + row0 + r, 1 - cur, row0 + r).start(priority=1)

    def compute_blocks(b0, n, next_slot0):
        r = pl.ds(pl.multiple_of(b0 * rows, rows), n * rows)
        x = x32[cur, r, :].astype(BF16)
        gate = jnp.minimum(jnp.dot(x, wgb[...], preferred_element_type=F32) + bg_ref[...], SWIGLU_LIMIT)
        up = jnp.clip(jnp.dot(x, wub[...], preferred_element_type=F32) + bu_ref[...],
                      -SWIGLU_LIMIT, SWIGLU_LIMIT)
        h = (up + 1.0) * (gate * _sigmoid(gate * SWIGLU_ALPHA))
        y32[r, :] += jnp.dot(h.astype(BF16), wdb[...], preferred_element_type=F32)
        if next_slot0 is not None:
            for i in range(n):
                prefetch(b0 + i, next_slot0)

        @pl.when(last_f)
        def _():
            for i in range(n):
                out_copy(b0 + i).start()

    def run(next_slot0):
        def pair(p, _):
            compute_blocks(2 * p, 2, next_slot0)
            return 0
        lax.fori_loop(0, nb // 2, pair, 0)

        @pl.when(nb % 2 == 1)
        def _():
            compute_blocks(nb - 1, 1, next_slot0)

    @pl.when(has_next)
    def _():
        next_slot0 = ib_ref[k + 1] * rows
        run(next_slot0)

        def rest(b, _):
            prefetch(b, next_slot0)
            return 0
        lax.fori_loop(nb, group, rest, 0)

    @pl.when(valid & jnp.logical_not(has_next))
    def _():
        run(None)

    @pl.when(valid & last_f)
    def _():
        def body(b, _):
            out_copy(b).wait()
            return 0
        lax.fori_loop(0, nb, body, 0)

    @pl.when((k == n_items - 1) & last_f)
    def _():
        used = nv_ref[1]
        y32[0:rows, :] = jnp.zeros((rows, d), F32)

        def fill(b):
            return pltpu.make_async_copy(y32.at[pl.ds(0, rows), :], ye_ref.at[pl.ds(b * rows, rows), :], sem_out)

        def start(b, _):
            fill(b).start()
            return 0

        def wait(b, _):
            fill(b).wait()
            return 0
        lax.fori_loop(used, n_blocks, start, 0)
        lax.fori_loop(used, n_blocks, wait, 0)


def expert_mlp(slot_tok, items, hn, w_gate_up, b_gate_up, w_down, b_down, n_items, n_blocks):
    item_e, item_b0, item_nb, counts = items
    n_exp, d, two_f = w_gate_up.shape
    d_ff = two_f // 2
    tf = MOE_FTILE
    n_f = d_ff // tf
    rows, group = MOE_ROWS, MOE_GROUP
    rs = rows * group
    assert rs % (n_f * group) == 0

    def fi(k, f, nv):
        return jnp.where(k < nv[0], f, n_f - 1)

    in_specs = [
        pl.BlockSpec(memory_space=pl.ANY),
        pl.BlockSpec((None, d, tf), lambda k, f, tok, ie, ib, inb, nv: (ie[k], 0, fi(k, f, nv))),
        pl.BlockSpec((None, d, tf), lambda k, f, tok, ie, ib, inb, nv: (ie[k], 0, n_f + fi(k, f, nv))),
        pl.BlockSpec((None, 1, tf), lambda k, f, tok, ie, ib, inb, nv: (ie[k], 0, fi(k, f, nv))),
        pl.BlockSpec((None, 1, tf), lambda k, f, tok, ie, ib, inb, nv: (ie[k], 0, n_f + fi(k, f, nv))),
        pl.BlockSpec((None, tf, d), lambda k, f, tok, ie, ib, inb, nv: (ie[k], fi(k, f, nv), 0)),
        pl.BlockSpec((None, 1, d), lambda k, f, tok, ie, ib, inb, nv: (ie[k], 0, 0)),
    ]
    return pl.pallas_call(
        functools.partial(_expert_kernel, rows=rows, group=group, n_f=n_f, n_items=n_items, n_blocks=n_blocks),
        out_shape=jax.ShapeDtypeStruct((n_blocks * rows, d), F32),
        grid_spec=pltpu.PrefetchScalarGridSpec(
            num_scalar_prefetch=5, grid=(n_items, n_f),
            in_specs=in_specs,
            out_specs=pl.BlockSpec(memory_space=pl.ANY),
            scratch_shapes=[pltpu.VMEM((2, rs, d), F32), pltpu.VMEM((rs, d), F32),
                            pltpu.VMEM((d, tf), BF16), pltpu.VMEM((d, tf), BF16), pltpu.VMEM((tf, d), BF16),
                            pltpu.SemaphoreType.DMA((2,)), pltpu.SemaphoreType.DMA]),
        compiler_params=_params(("arbitrary", "arbitrary"), VMEM_LIMIT_BYTES),
        name="moe_experts",
    )(slot_tok, item_e, item_b0, item_nb, counts, hn, w_gate_up, w_gate_up,
      b_gate_up.reshape(n_exp, 1, two_f), b_gate_up.reshape(n_exp, 1, two_f),
      w_down, b_down.reshape(n_exp, 1, d))


def _combine_kernel(pos_ref, ye_ref, w_ref, h_ref, g_ref, o_ref, buf, sems, *, tc, blk0, n_steps, final_norm):
    i = pl.program_id(0)
    n_rows = TOP_K * tc

    def copy(r, src_row, par):
        return pltpu.make_async_copy(ye_ref.at[pl.ds(src_row, 1), :], buf.at[par, pl.ds(r, 1), :],
                                     sems.at[par])

    def issue(step, par):
        def body(g, _):
            for j in range(8):
                r = g * 8 + j
                copy(r, pos_ref[(blk0 + step) * n_rows + r], par).start(priority=j % 2)
            return 0
        lax.fori_loop(0, n_rows // 8, body, 0)

    @pl.when(i == 0)
    def _():
        issue(0, 0)

    @pl.when(i + 1 < n_steps)
    def _():
        issue(i + 1, (i + 1) % 2)

    par = i % 2

    def drain(r, _):
        copy(0, 0, par).wait()
        return 0
    lax.fori_loop(0, n_rows, drain, 0, unroll=8)

    w = w_ref[...]
    y = h_ref[...]
    for k in range(TOP_K):
        y = y + w[:, k:k + 1] * buf[par, k * tc:(k + 1) * tc, :]
    o_ref[...] = _rms(y, g_ref[...]) if final_norm else y


def combine(pos, ye, top_w, h, g, row0, rows, final_norm, tc=COMBINE_ROWS):
    d = h.shape[1]
    blk0 = row0 // tc
    n_steps = rows // tc
    return pl.pallas_call(
        functools.partial(_combine_kernel, tc=tc, blk0=blk0, n_steps=n_steps, final_norm=final_norm),
        out_shape=jax.ShapeDtypeStruct((rows, d), F32),
        grid_spec=pltpu.PrefetchScalarGridSpec(
            num_scalar_prefetch=1, grid=(n_steps,),
            in_specs=[pl.BlockSpec(memory_space=pl.ANY),
                      pl.BlockSpec((tc, TOP_K), lambda i, p: (blk0 + i, 0)),
                      pl.BlockSpec((tc, d), lambda i, p: (blk0 + i, 0)),
                      pl.BlockSpec((1, d), lambda i, p: (0, 0))],
            out_specs=pl.BlockSpec((tc, d), lambda i, p: (i, 0)),
            scratch_shapes=[pltpu.VMEM((2, TOP_K * tc, d), F32), pltpu.SemaphoreType.DMA((2,))]),
        compiler_params=_params(("arbitrary",), VMEM_LIMIT_BYTES),
        name="moe_combine",
    )(pos, ye, top_w, h, g)


def route(logits, tc):
    m, n_exp = logits.shape
    a = m * TOP_K
    rows, group = MOE_ROWS, MOE_GROUP
    top_logit, top_idx = lax.top_k(logits, TOP_K)
    top_w = jax.nn.softmax(top_logit, axis=-1)
    flat_e = top_idx.reshape(a)
    onehot = (flat_e[:, None] == jnp.arange(n_exp, dtype=jnp.int32)[None, :]).astype(jnp.int32)
    csum = jnp.cumsum(onehot, axis=0)
    counts = csum[-1]
    rank = jnp.take_along_axis(csum, flat_e[:, None], axis=1)[:, 0] - 1
    nb_e = (counts + rows - 1) // rows
    blk_start = jnp.cumsum(nb_e) - nb_e
    dest = blk_start[flat_e] * rows + rank
    n_blocks = a // rows + n_exp
    slot_tok = jnp.zeros(((n_blocks + group) * rows,), jnp.int32).at[dest].set(
        jnp.arange(a, dtype=jnp.int32) // TOP_K)
    ni_e = (nb_e + group - 1) // group
    item_end = jnp.cumsum(ni_e)
    n_valid = item_end[-1]
    n_items = (n_blocks + n_exp * (group - 1)) // group
    kk = jnp.arange(n_items, dtype=jnp.int32)
    item_e = jnp.minimum(jnp.sum(item_end[None, :] <= kk[:, None], axis=1), n_exp - 1).astype(jnp.int32)
    local = kk - (item_end - ni_e)[item_e]
    valid = kk < n_valid
    item_e = jnp.where(valid, item_e, item_e[jnp.maximum(n_valid - 1, 0)])
    item_b0 = jnp.where(valid, blk_start[item_e] + local * group, 0).astype(jnp.int32)
    item_nb = jnp.where(valid, jnp.clip(nb_e[item_e] - local * group, 0, group), 0).astype(jnp.int32)
    pos = dest.reshape(m // tc, tc, TOP_K).transpose(0, 2, 1).reshape(a)
    counts = jnp.stack([n_valid, jnp.sum(nb_e)]).astype(jnp.int32)
    items = (item_e, item_b0, item_nb, counts)
    return top_w, slot_tok, pos.astype(jnp.int32), items, n_items, n_blocks


def _rope_tables(pos, half):
    inv = jnp.power(ROPE_THETA, -jnp.arange(half, dtype=F32) / half)
    ang = pos.astype(F32)[:, None] * inv[None, :]
    return jnp.cos(ang), jnp.sin(ang)


def kernel(x_prompt, x_sample, cache_latent, cache_k_rope, state_retention, g_mix_norm, w_in, g_q_norm, w_uq, g_kv_norm, w_uk, w_uv, w_o, g_ffn_norm, w_router, b_router, w_gate_up, b_gate_up, w_down, b_down, g_final):
    bp, t, d = x_prompt.shape
    db, ts, _ = x_sample.shape
    depth = w_in.shape[0]
    past = cache_latent.shape[2]
    ql, ha, qk = w_uq.shape[1:]
    kl = w_uk.shape[1]
    rope = cache_k_rope.shape[-1]
    nope = qk - rope
    hr, dk = state_retention.shape[2:4]
    assert bp == 1 and nope == LANES and w_uv.shape[-1] == LANES and 2 * rope == LANES
    assert past % CHUNK == 0 and ts <= CHUNK
    mp, ms = bp * t, db * ts
    m = mp + ms
    tk_s = past + ts
    mla_scale = float(qk) ** -0.5 * 1.4426950408889634

    pos = jnp.concatenate([jnp.arange(t, dtype=jnp.int32),
                           jnp.tile(past + jnp.arange(ts, dtype=jnp.int32), db)])
    cos_a, sin_a = _rope_tables(pos, rope // 2)
    z = lambda w: jnp.zeros((m, w), F32)
    rope_c = jnp.concatenate([cos_a, cos_a, z(LANES - rope)], axis=1)
    rope_sa = jnp.concatenate([z(rope // 2), sin_a, z(LANES - rope)], axis=1)
    rope_sb = jnp.concatenate([-sin_a, z(LANES - rope // 2)], axis=1)
    cos_r, sin_r = _rope_tables(pos, dk // 2)
    tabs_p = retention_tables(RET_CHUNK, hr, dk)
    tabs_s = retention_tables(ts, hr, dk)

    xp, xs = x_prompt.reshape(mp, d), x_sample.reshape(ms, d)
    lat_p, kr_p, st_p, lat_s, kr_s, st_s = [], [], [], [], [], []
    small = ql + kl + rope
    for l in range(depth):
        wa, wb = cast_w_in(w_in[l], small)
        wq = jnp.pad(w_uq[l], ((0, 0), (0, 0), (0, 2 * LANES - qk))).reshape(ql, ha * 2 * LANES).astype(BF16)
        wk = w_uk[l].reshape(kl, ha * nope).astype(BF16)
        wv = w_uv[l].reshape(kl, ha * LANES).astype(BF16)
        wo = w_o[l].astype(BF16)

        xn = norm_cast(xp, xs, g_mix_norm[l].reshape(1, d))
        cqn, ckv, ckv_b, kr, kr_pad = proj_small(xn, wa, g_q_norm[l].reshape(1, ql), g_kv_norm[l].reshape(1, kl),
                                                 rope_c, rope_sa, rope_sb, ql, kl, rope)
        big = proj_big(xn, wb, cos_r, sin_r, 2 * hr * dk, dk, float(dk) ** -0.5)
        qcat = q_proj(cqn, wq, rope_c, rope_sa, rope_sb, ha, mla_scale)

        k_p, vt_p = kv_expand(ckv_b, kr_pad, wk, wv.T, mp, ha)
        oa_p = attention_causal(qcat, k_p, vt_p, ha, t=mp, tq=512)
        c_all = jnp.concatenate([cache_latent[l], ckv[mp:].reshape(db, ts, kl)], axis=1)
        kr_all = jnp.concatenate([cache_k_rope[l], kr[mp:].reshape(db, ts, rope)], axis=1)
        kr_all = jnp.pad(kr_all, ((0, 0), (0, 0), (0, LANES - rope)))
        oa_s = attention_absorbed(qcat, c_all.reshape(db * tk_s, kl).astype(BF16),
                                  kr_all.reshape(db * tk_s, LANES).astype(BF16),
                                  w_uk[l].transpose(1, 2, 0).astype(BF16), w_uv[l].transpose(1, 0, 2).astype(BF16),
                                  ha, batch=db, tq=ts, q_row0=mp, tk=tk_s)

        or_p, s_p = retention(big, jnp.zeros((bp, hr, dk, dk), F32), tabs_p, hr, dk,
                              batch=bp, t_total=t, row0=0, chunk=RET_CHUNK)
        or_s, s_s = retention(big, state_retention[l].astype(F32), tabs_s, hr, dk,
                              batch=db, t_total=ts, row0=mp, chunk=ts)

        h1, hn, logits = mixer_out(oa_p, oa_s, or_p, or_s, big, 3, xp, xs, wo, g_ffn_norm[l].reshape(1, d),
                                   w_router[l], b_router[l].reshape(1, -1))

        top_w, slot_tok, pos_flat, items, n_items, n_blocks = route(logits, COMBINE_ROWS)
        ye = expert_mlp(slot_tok, items, hn, w_gate_up[l], b_gate_up[l], w_down[l], b_down[l], n_items, n_blocks)
        last = l == depth - 1
        g_out = g_final.reshape(1, d)
        y_p = combine(pos_flat, ye, top_w, h1, g_out, 0, mp, last)
        y_s = combine(pos_flat, ye, top_w, h1, g_out, mp, ms, last)
        xp, xs = y_p, y_s

        lat_p.append(ckv[:mp].reshape(bp, t, kl))
        kr_p.append(kr[:mp].reshape(bp, t, rope))
        st_p.append(s_p)
        lat_s.append(ckv[mp:].reshape(db, ts, kl))
        kr_s.append(kr[mp:].reshape(db, ts, rope))
        st_s.append(s_s)

    return (y_p.reshape(bp, t, d), y_s.reshape(db, ts, d),
            jnp.stack(lat_p), jnp.stack(kr_p), jnp.stack(st_p),
            jnp.stack(lat_s), jnp.stack(kr_s), jnp.stack(st_s))
```

```python
import functools

import jax
import jax.numpy as jnp
from jax import lax
from jax.experimental import pallas as pl
from jax.experimental.pallas import tpu as pltpu

F32 = jnp.float32
BF16 = jnp.bfloat16

NORM_EPS = 1e-6
ROPE_THETA = 10000.0
CHUNK = 64
TOP_K = 4
SWIGLU_LIMIT = 7.0
SWIGLU_ALPHA = 1.702

LANES = 128
VMEM_LIMIT_BYTES = 56 * 1024 * 1024

RET_CHUNK = 256
MOE_ROWS = 128
MOE_GROUP = 10
MOE_CHAIN = 4
MOE_FTILE = 256
COMBINE_ROWS = 128
MIXER_SUBTILES = 2
ABSORBED_HEADS = 4


def _params(sem, vmem=None):
    return pltpu.CompilerParams(dimension_semantics=sem, vmem_limit_bytes=vmem)


def _rms(x, g):
    return x * lax.rsqrt(jnp.mean(x * x, axis=-1, keepdims=True) + NORM_EPS) * g


def _sigmoid(x):
    return 1.0 / (1.0 + jnp.exp(-x))


def _group_specs(rows_p, rows_s, tm, d):
    npb, nsb = rows_p // tm, rows_s // tm
    prompt = pl.BlockSpec((tm, d), lambda i: (jnp.minimum(i, npb - 1), 0))
    sample = pl.BlockSpec((tm, d), lambda i: (jnp.clip(i - npb, 0, nsb - 1), 0))
    return npb, prompt, sample


def _norm_cast_kernel(xp_ref, xs_ref, g_ref, o_ref, *, prompt_blocks):
    x = jnp.where(pl.program_id(0) < prompt_blocks, xp_ref[...], xs_ref[...])
    o_ref[...] = _rms(x, g_ref[...]).astype(o_ref.dtype)


def norm_cast(xp, xs, g, tm=512):
    d = xp.shape[1]
    m = xp.shape[0] + xs.shape[0]
    npb, prompt, sample = _group_specs(xp.shape[0], xs.shape[0], tm, d)
    return pl.pallas_call(
        functools.partial(_norm_cast_kernel, prompt_blocks=npb),
        out_shape=jax.ShapeDtypeStruct((m, d), BF16),
        grid=(m // tm,),
        in_specs=[prompt, sample, pl.BlockSpec((1, d), lambda i: (0, 0))],
        out_specs=pl.BlockSpec((tm, d), lambda i: (i, 0)),
        compiler_params=_params(("parallel",), VMEM_LIMIT_BYTES),
        name="norm_cast",
    )(xp, xs, g)


def _rope_hi(x, c, sa, sb):
    return x * c + pltpu.roll(x, 32, 1) * sa + pltpu.roll(x, 96, 1) * sb


def _proj_small_kernel(x_ref, w_ref, gq_ref, gkv_ref, c_ref, sa_ref, sb_ref,
                       cqn_ref, ckv_ref, ckvb_ref, kr_ref, krp_ref, *, ql, kl, rope):
    acc = jnp.dot(x_ref[...], w_ref[...], preferred_element_type=F32)
    cqn_ref[...] = _rms(acc[:, :ql], gq_ref[...]).astype(cqn_ref.dtype)
    ckv = _rms(acc[:, ql:ql + kl], gkv_ref[...])
    ckv_ref[...] = ckv
    ckvb_ref[...] = ckv.astype(ckvb_ref.dtype)
    kr = _rope_hi(acc[:, ql + kl:ql + kl + LANES], c_ref[...], sa_ref[...], sb_ref[...])
    kr_ref[...] = kr[:, :rope]
    krp_ref[...] = kr.astype(krp_ref.dtype)


def proj_small(xn, wa, gq, gkv, c, sa, sb, ql, kl, rope, tm=256):
    m, d = xn.shape
    n = wa.shape[1]
    row = lambda w: pl.BlockSpec((tm, w), lambda i: (i, 0))
    full = lambda a: pl.BlockSpec(a.shape, lambda i: (0, 0))
    return pl.pallas_call(
        functools.partial(_proj_small_kernel, ql=ql, kl=kl, rope=rope),
        out_shape=(jax.ShapeDtypeStruct((m, ql), BF16),
                   jax.ShapeDtypeStruct((m, kl), F32),
                   jax.ShapeDtypeStruct((m, kl), BF16),
                   jax.ShapeDtypeStruct((m, rope), F32),
                   jax.ShapeDtypeStruct((m, LANES), BF16)),
        grid=(m // tm,),
        in_specs=[row(d), full(wa), full(gq), full(gkv), row(LANES), row(LANES), row(LANES)],
        out_specs=(row(ql), row(kl), row(kl), row(rope), row(LANES)),
        compiler_params=_params(("parallel",), VMEM_LIMIT_BYTES),
        name="proj_small",
    )(xn, wa, gq, gkv, c, sa, sb)


def _proj_big_kernel(x_ref, w_ref, cos_ref, sin_ref, o_ref, *, n_rope, dk, k_scale):
    j = pl.program_id(0)
    acc = jnp.dot(x_ref[...], w_ref[...], preferred_element_type=F32)

    @pl.when(j >= n_rope)
    def _():
        o_ref[...] = acc.astype(o_ref.dtype)

    @pl.when(j < n_rope)
    def _():
        scale = jnp.where(j >= n_rope // 2, k_scale, 1.0).astype(F32)
        c = cos_ref[...] * scale
        s = sin_ref[...] * scale
        half = dk // 2
        for h in range(acc.shape[1] // dk):
            x1 = acc[:, h * dk:h * dk + half]
            x2 = acc[:, h * dk + half:(h + 1) * dk]
            o_ref[:, h * dk:h * dk + half] = (x1 * c - x2 * s).astype(o_ref.dtype)
            o_ref[:, h * dk + half:(h + 1) * dk] = (x2 * c + x1 * s).astype(o_ref.dtype)


def proj_big(xn, wb, cos_r, sin_r, rope_cols, dk, k_scale, tm=512, tn=1024):
    m, d = xn.shape
    n = wb.shape[1]
    return pl.pallas_call(
        functools.partial(_proj_big_kernel, n_rope=rope_cols // tn, dk=dk, k_scale=k_scale),
        out_shape=jax.ShapeDtypeStruct((m, n), BF16),
        grid=(n // tn, m // tm),
        in_specs=[pl.BlockSpec((tm, d), lambda j, i: (i, 0)),
                  pl.BlockSpec((d, tn), lambda j, i: (0, j)),
                  pl.BlockSpec((tm, dk // 2), lambda j, i: (i, 0)),
                  pl.BlockSpec((tm, dk // 2), lambda j, i: (i, 0))],
        out_specs=pl.BlockSpec((tm, tn), lambda j, i: (i, j)),
        compiler_params=_params(("parallel", "parallel"), VMEM_LIMIT_BYTES),
        name="proj_big",
    )(xn, wb, cos_r, sin_r)


def _q_proj_kernel(x_ref, w_ref, c_ref, sa_ref, sb_ref, o_ref, *, heads, scale):
    acc = jnp.dot(x_ref[...], w_ref[...], preferred_element_type=F32)
    c = c_ref[...] * scale
    sa = sa_ref[...] * scale
    sb = sb_ref[...] * scale
    for h in range(heads):
        lo = 2 * h * LANES
        o_ref[:, lo:lo + LANES] = (acc[:, lo:lo + LANES] * scale).astype(o_ref.dtype)
        o_ref[:, lo + LANES:lo + 2 * LANES] = _rope_hi(
            acc[:, lo + LANES:lo + 2 * LANES], c, sa, sb).astype(o_ref.dtype)


def q_proj(cqn, wq, c, sa, sb, heads, scale, tm=256):
    m, k = cqn.shape
    n = wq.shape[1]
    row = lambda w: pl.BlockSpec((tm, w), lambda i: (i, 0))
    return pl.pallas_call(
        functools.partial(_q_proj_kernel, heads=heads, scale=scale),
        out_shape=jax.ShapeDtypeStruct((m, n), BF16),
        grid=(m // tm,),
        in_specs=[row(k), pl.BlockSpec((k, n), lambda i: (0, 0)), row(LANES), row(LANES), row(LANES)],
        out_specs=row(n),
        compiler_params=_params(("parallel",), VMEM_LIMIT_BYTES),
        name="q_proj",
    )(cqn, wq, c, sa, sb)


def _kv_expand_kernel(c_ref, kr_ref, wk_ref, wvt_ref, k_ref, vt_ref, *, heads):
    c = c_ref[...]
    kn = jnp.dot(c, wk_ref[...], preferred_element_type=F32)
    vt_ref[...] = lax.dot_general(wvt_ref[...], c, (((1,), (1,)), ((), ())),
                                  preferred_element_type=F32).astype(vt_ref.dtype)
    kr = kr_ref[...]
    for h in range(heads):
        k_ref[:, 2 * h * LANES:(2 * h + 1) * LANES] = kn[:, h * LANES:(h + 1) * LANES].astype(k_ref.dtype)
        k_ref[:, (2 * h + 1) * LANES:(2 * h + 2) * LANES] = kr


def kv_expand(c_bf, kr_pad, wk, wvt, rows, heads, tm=256):
    kl = c_bf.shape[1]
    dv = heads * LANES
    row = lambda w: pl.BlockSpec((tm, w), lambda i: (i, 0))
    full = lambda a: pl.BlockSpec(a.shape, lambda i: (0, 0))
    return pl.pallas_call(
        functools.partial(_kv_expand_kernel, heads=heads),
        out_shape=(jax.ShapeDtypeStruct((rows, 2 * heads * LANES), BF16),
                   jax.ShapeDtypeStruct((dv, rows), BF16)),
        grid=(rows // tm,),
        in_specs=[row(kl), row(LANES), full(wk), full(wvt)],
        out_specs=(row(2 * heads * LANES), pl.BlockSpec((dv, tm), lambda i: (0, i))),
        compiler_params=_params(("parallel",), VMEM_LIMIT_BYTES),
        name="kv_expand",
    )(c_bf, kr_pad, wk, wvt)


def _attn_absorbed_kernel(q_ref, c_ref, kr_ref, wukt_ref, wuv_ref, o_ref, *, heads):
    tq = q_ref.shape[0]
    c = c_ref[...]
    kr = kr_ref[...]
    nt = (((1,), (1,)), ((), ()))
    for g in range(heads // ABSORBED_HEADS):
        hs = range(g * ABSORBED_HEADS, (g + 1) * ABSORBED_HEADS)
        q_lat = jnp.concatenate(
            [jnp.dot(q_ref[:, 2 * h * LANES:(2 * h + 1) * LANES], wukt_ref[h],
                     preferred_element_type=F32).astype(BF16) for h in hs], axis=0)
        q_rope = jnp.concatenate([q_ref[:, (2 * h + 1) * LANES:(2 * h + 2) * LANES] for h in hs], axis=0)
        s = (lax.dot_general(q_lat, c, nt, preferred_element_type=F32)
             + lax.dot_general(q_rope, kr, nt, preferred_element_type=F32))
        p = jnp.exp2(s - jnp.max(s, axis=-1, keepdims=True))
        l = jnp.sum(p, axis=-1, keepdims=True)
        o_lat = (jnp.dot(p.astype(BF16), c, preferred_element_type=F32) / l).astype(BF16)
        for i, h in enumerate(hs):
            o_ref[:, h * LANES:(h + 1) * LANES] = jnp.dot(
                o_lat[i * tq:(i + 1) * tq], wuv_ref[h], preferred_element_type=F32).astype(o_ref.dtype)


def attention_absorbed(q, c_all, kr_all, wukt, wuv, heads, *, batch, tq, q_row0, tk):
    assert heads % ABSORBED_HEADS == 0 and q_row0 % tq == 0
    kl = c_all.shape[1]
    qb0 = q_row0 // tq
    full = lambda a: pl.BlockSpec(a.shape, lambda b: (0, 0, 0))
    return pl.pallas_call(
        functools.partial(_attn_absorbed_kernel, heads=heads),
        out_shape=jax.ShapeDtypeStruct((batch * tq, heads * LANES), BF16),
        grid=(batch,),
        in_specs=[pl.BlockSpec((tq, 2 * heads * LANES), lambda b: (qb0 + b, 0)),
                  pl.BlockSpec((tk, kl), lambda b: (b, 0)),
                  pl.BlockSpec((tk, LANES), lambda b: (b, 0)),
                  full(wukt), full(wuv)],
        out_specs=pl.BlockSpec((tq, heads * LANES), lambda b: (b, 0)),
        compiler_params=_params(("parallel",), VMEM_LIMIT_BYTES),
        name="attention_absorbed",
    )(q, c_all, kr_all, wukt, wuv)


def _attn_t_kernel(q_ref, k_ref, vt_ref, o_ref, sa_ref, sb_ref, *, tq, tk):
    i = pl.program_id(1)
    dq, dv = 2 * LANES, LANES
    n_heads = q_ref.shape[1] // dq

    def scores(j, s_ref):
        off = pl.multiple_of(j * tk, tk)
        for h in range(n_heads):
            q = q_ref[:, h * dq:(h + 1) * dq]
            k = k_ref[pl.ds(off, tk), h * dq:(h + 1) * dq]
            s_ref[h] = lax.dot_general(k, q, (((1,), (1,)), ((), ())), preferred_element_type=F32)

    def softmax_pv(j, s_ref, carries, masked):
        off = pl.multiple_of(j * tk, tk)
        out = []
        for h in range(n_heads):
            m, l, acc = carries[h]
            s = s_ref[h]
            if masked:
                shift = CHUNK.bit_length() - 1
                kc = lax.shift_right_logical(j * tk + lax.broadcasted_iota(jnp.int32, s.shape, 0), shift)
                qc = lax.shift_right_logical(i * tq + lax.broadcasted_iota(jnp.int32, s.shape, 1), shift)
                s = jnp.where(kc <= qc, s, -1e30)
            m_new = jnp.maximum(m, jnp.max(s, axis=0, keepdims=True))
            alpha = jnp.exp2(m - m_new)
            p = jnp.exp2(s - m_new)
            l = alpha * l + jnp.sum(p, axis=0, keepdims=True)
            vt = vt_ref[h * dv:(h + 1) * dv, pl.ds(off, tk)]
            acc = alpha * acc + jnp.dot(vt, p.astype(BF16), preferred_element_type=F32)
            out.append((m_new, l, acc))
        return tuple(out)

    def finish(carries):
        for h, (_, l, acc) in enumerate(carries):
            o_ref[:, h * dv:(h + 1) * dv] = (acc / l).T.astype(o_ref.dtype)

    scores(0, sa_ref)

    def pair(t, carries):
        scores(2 * t + 1, sb_ref)
        carries = softmax_pv(2 * t, sa_ref, carries, False)
        scores(2 * t + 2, sa_ref)
        return softmax_pv(2 * t + 1, sb_ref, carries, False)

    init = (jnp.full((1, tq), -jnp.inf, F32), jnp.zeros((1, tq), F32), jnp.zeros((dv, tq), F32))
    carries = lax.fori_loop(0, i // 2, pair, (init,) * n_heads)

    @pl.when(i % 2 == 0)
    def _():
        finish(softmax_pv(i, sa_ref, carries, True))

    @pl.when(i % 2 == 1)
    def _():
        scores(i, sb_ref)
        finish(softmax_pv(i, sb_ref, softmax_pv(i - 1, sa_ref, carries, False), True))


def attention_causal(q, k, vt, heads, *, t, tq, hp=2):
    assert CHUNK & (CHUNK - 1) == 0 and tq % CHUNK == 0 and heads % hp == 0
    dq = 2 * LANES
    return pl.pallas_call(
        functools.partial(_attn_t_kernel, tq=tq, tk=tq),
        out_shape=jax.ShapeDtypeStruct((t, heads * LANES), BF16),
        grid=(heads // hp, t // tq),
        in_specs=[pl.BlockSpec((tq, hp * dq), lambda h, i: (i, h)),
                  pl.BlockSpec((t, hp * dq), lambda h, i: (0, h)),
                  pl.BlockSpec((hp * LANES, t), lambda h, i: (h, 0))],
        out_specs=pl.BlockSpec((tq, hp * LANES), lambda h, i: (i, h)),
        scratch_shapes=[pltpu.VMEM((hp, tq, tq), F32), pltpu.VMEM((hp, tq, tq), F32)],
        compiler_params=_params(("parallel", "arbitrary"), VMEM_LIMIT_BYTES),
        name="attention_causal",
    )(q, k, vt)


def _retention_kernel(q_ref, k_ref, v_ref, s0_ref, d_ref, xi_ref, zeta_ref, gc_ref,
                      o_ref, sout_ref, s_scr, *, n_chunks):
    c = pl.program_id(2)
    dk = s_scr.shape[-1]

    @pl.when(c == 0)
    def _():
        s_scr[...] = s0_ref[...]

    for h in range(s_scr.shape[0]):
        cols = slice(h * dk, (h + 1) * dk)
        q = q_ref[:, cols]
        k = k_ref[:, cols]
        v = v_ref[:, cols]
        s = s_scr[h]
        a = lax.dot_general(q, k, (((1,), (1,)), ((), ())), preferred_element_type=F32) * d_ref[h]
        o = jnp.dot(a.astype(BF16), v, preferred_element_type=F32)
        o = o + jnp.dot(q, s.astype(BF16), preferred_element_type=F32) * xi_ref[h]
        kz = (k.astype(F32) * zeta_ref[h]).astype(BF16)
        s_new = s * gc_ref[h] + lax.dot_general(kz, v, (((0,), (0,)), ((), ())),
                                                preferred_element_type=F32)
        s_scr[h] = s_new
        o_ref[:, cols] = (o * lax.rsqrt(jnp.mean(o * o, axis=-1, keepdims=True) + NORM_EPS)).astype(o_ref.dtype)

    @pl.when(c == n_chunks - 1)
    def _():
        sout_ref[...] = s_scr[...]


def retention(big, s0, tabs, heads, dk, *, batch, t_total, row0, chunk, hp=4):
    d_tab, xi_tab, zeta_tab, gc_tab = tabs
    assert heads % hp == 0
    nc = t_total // chunk
    rb0 = row0 // chunk
    hg = heads // hp
    rowblk = lambda g0: pl.BlockSpec((chunk, hp * dk), lambda b, h, c: (rb0 + b * nc + c, g0 + h))
    per_head = lambda a: pl.BlockSpec((hp,) + a.shape[1:], lambda b, h, c: (h, 0, 0))
    st_spec = pl.BlockSpec((None, hp, dk, dk), lambda b, h, c: (b, h, 0, 0))
    return pl.pallas_call(
        functools.partial(_retention_kernel, n_chunks=nc),
        out_shape=(jax.ShapeDtypeStruct((batch * t_total, heads * dk), BF16),
                   jax.ShapeDtypeStruct((batch, heads, dk, dk), F32)),
        grid=(batch, hg, nc),
        in_specs=[rowblk(0), rowblk(hg), rowblk(2 * hg), st_spec,
                  per_head(d_tab), per_head(xi_tab), per_head(zeta_tab), per_head(gc_tab)],
        out_specs=(pl.BlockSpec((chunk, hp * dk), lambda b, h, c: (b * nc + c, h)), st_spec),
        scratch_shapes=[pltpu.VMEM((hp, dk, dk), F32)],
        compiler_params=_params(("parallel", "parallel", "arbitrary"), VMEM_LIMIT_BYTES),
        name="retention",
    )(big, big, big, s0, d_tab, xi_tab, zeta_tab, gc_tab)


def retention_tables(length, heads, dk):
    lg = jnp.log1p(-jnp.exp2(-5.0 - jnp.arange(heads, dtype=F32)))
    idx = jnp.arange(length, dtype=F32)
    diff = idx[:, None] - idx[None, :]
    decay = jnp.where(diff >= 0, jnp.exp(jnp.maximum(diff, 0.0)[None] * lg[:, None, None]), 0.0)
    xi = jnp.exp((idx + 1.0)[None, :] * lg[:, None])
    zeta = jnp.exp((length - 1.0 - idx)[None, :] * lg[:, None])
    gc = jnp.exp(length * lg)
    bc = lambda a: jnp.broadcast_to(a[:, :, None], (heads, a.shape[1], dk))
    return decay, bc(xi), bc(zeta), jnp.broadcast_to(gc[:, None, None], (heads, 1, dk))


def _mixer_out_kernel(oap_ref, oas_ref, orp_ref, ors_ref, rg_ref, ga_ref, gb_ref, xp_ref, xs_ref, wo_ref,
                      g_ref, wr_ref, br_ref, h_ref, hn_ref, lg_ref, *, prompt_blocks):
    is_prompt = pl.program_id(0) < prompt_blocks
    tm = h_ref.shape[0]
    sub = tm // MIXER_SUBTILES
    for s in range(MIXER_SUBTILES):
        r = slice(s * sub, (s + 1) * sub)
        oa = jnp.where(is_prompt, oap_ref[r, :], oas_ref[r, :]).astype(F32)
        orn = jnp.where(is_prompt, orp_ref[r, :], ors_ref[r, :]).astype(F32)
        x = jnp.where(is_prompt, xp_ref[r, :], xs_ref[r, :])
        rg = rg_ref[r, :].astype(F32)
        yb = rg * _sigmoid(rg) * orn
        merged = _sigmoid(ga_ref[r, :].astype(F32)) * oa + _sigmoid(gb_ref[r, :].astype(F32)) * yb
        h = x + jnp.dot(merged.astype(BF16), wo_ref[...], preferred_element_type=F32)
        h_ref[r, :] = h
        hn = _rms(h, g_ref[...])
        hn_ref[r, :] = hn
        lg_ref[r, :] = jnp.dot(hn, wr_ref[...], preferred_element_type=F32,
                               precision=lax.Precision.HIGHEST) + br_ref[...]


def mixer_out(oa_p, oa_s, or_p, or_s, big, gate_blk0, xp, xs, wo, g, wr, br, tm=256):
    d = xp.shape[1]
    m = xp.shape[0] + xs.shape[0]
    e = wr.shape[1]
    npb, prompt, sample = _group_specs(xp.shape[0], xs.shape[0], tm, d)
    row = lambda col: pl.BlockSpec((tm, d), lambda i: (i, col))
    full = lambda a: pl.BlockSpec(a.shape, lambda i: (0, 0))
    return pl.pallas_call(
        functools.partial(_mixer_out_kernel, prompt_blocks=npb),
        out_shape=(jax.ShapeDtypeStruct((m, d), F32),
                   jax.ShapeDtypeStruct((m, d), F32),
                   jax.ShapeDtypeStruct((m, e), F32)),
        grid=(m // tm,),
        in_specs=[prompt, sample, prompt, sample, row(gate_blk0), row(gate_blk0 + 1), row(gate_blk0 + 2),
                  prompt, sample, full(wo), full(g), full(wr), full(br)],
        out_specs=(row(0), row(0), pl.BlockSpec((tm, e), lambda i: (i, 0))),
        compiler_params=_params(("parallel",), VMEM_LIMIT_BYTES),
        name="mixer_out",
    )(oa_p, oa_s, or_p, or_s, big, big, big, xp, xs, wo, g, wr, br)


def _expert_kernel(tok_ref, ie_ref, ib_ref, inb_ref, nv_ref, hn_ref, wg_ref, wu_ref, bg_ref, bu_ref, wd_ref,
                   bd_ref, ye_ref, x32, y32, wgb, wub, wdb, sem_x, sem_out, *, rows, group, n_f, n_items,
                   n_blocks):
    k = pl.program_id(0)
    f = pl.program_id(1)
    nb = inb_ref[k]
    blk0 = ib_ref[k]
    n_valid = nv_ref[0]
    valid = k < n_valid
    has_next = k + 1 < n_valid
    last_f = f == n_f - 1
    cur = k % 2
    rs = rows * group
    d = y32.shape[1]
    per_block = rs // (n_f * group)

    def gather_copy(slot, buf, row):
        return pltpu.make_async_copy(hn_ref.at[pl.ds(tok_ref[slot], 1), :],
                                     x32.at[buf, pl.ds(row, 1), :], sem_x.at[buf])

    def out_copy(b):
        return pltpu.make_async_copy(y32.at[pl.ds(b * rows, rows), :],
                                     ye_ref.at[pl.ds((blk0 + b) * rows, rows), :], sem_out)

    @pl.when((k == 0) & (f == 0))
    def _():
        def body(r, _):
            gather_copy(blk0 * rows + r, 0, r).start()
            return 0
        lax.fori_loop(0, rs, body, 0, unroll=8)

    @pl.when(valid & (f == 0))
    def _():
        def drain(r, _):
            gather_copy(0, cur, 0).wait()
            return 0
        lax.fori_loop(0, rs, drain, 0, unroll=8)

        def init(b, _):
            y32[pl.ds(pl.multiple_of(b * rows, rows), rows), :] = jnp.broadcast_to(bd_ref[...], (rows, d))
            return 0
        lax.fori_loop(0, nb, init, 0)

    @pl.when(valid)
    def _():
        wgb[...] = wg_ref[...].astype(BF16)
        wub[...] = wu_ref[...].astype(BF16)
        wdb[...] = wd_ref[...].astype(BF16)

    def prefetch(b, next_slot0):
        row0 = (f * group + b) * per_block
        for r in range(per_block):
            gather_copy(next_slot0 + row0 + r, 1 - cur, row0 + r).start(priority=1)

    def compute_blocks(b0, n, next_slot0):
        r = pl.ds(pl.multiple_of(b0 * rows, rows), n * rows)
        x = x32[cur, r, :].astype(BF16)
        gate = jnp.minimum(jnp.dot(x, wgb[...], preferred_element_type=F32) + bg_ref[...], SWIGLU_LIMIT)
        up = jnp.clip(jnp.dot(x, wub[...], preferred_element_type=F32) + bu_ref[...],
                      -SWIGLU_LIMIT, SWIGLU_LIMIT)
        h = (up + 1.0) * (gate * _sigmoid(gate * SWIGLU_ALPHA))
        y32[r, :] += jnp.dot(h.astype(BF16), wdb[...], preferred_element_type=F32)
        if next_slot0 is not None:
            for i in range(n):
                prefetch(b0 + i, next_slot0)

        @pl.when(last_f)
        def _():
            for i in range(n):
                out_copy(b0 + i).start()

    def run(next_slot0):
        def chain(p, _):
            compute_blocks(MOE_CHAIN * p, MOE_CHAIN, next_slot0)
            return 0
        lax.fori_loop(0, nb // MOE_CHAIN, chain, 0)
        n = MOE_CHAIN // 2
        while n >= 1:
            @pl.when(nb & n != 0)
            def _(n=n):
                compute_blocks(nb - (nb & (2 * n - 1)), n, next_slot0)
            n //= 2

    @pl.when(has_next)
    def _():
        next_slot0 = ib_ref[k + 1] * rows
        run(next_slot0)

        def rest(b, _):
            prefetch(b, next_slot0)
            return 0
        lax.fori_loop(nb, group, rest, 0)

    @pl.when(valid & jnp.logical_not(has_next))
    def _():
        run(None)

    @pl.when(valid & last_f)
    def _():
        def body(b, _):
            out_copy(b).wait()
            return 0
        lax.fori_loop(0, nb, body, 0)

    @pl.when((k == n_items - 1) & last_f)
    def _():
        used = nv_ref[1]
        y32[0:rows, :] = jnp.zeros((rows, d), F32)

        def fill(b):
            return pltpu.make_async_copy(y32.at[pl.ds(0, rows), :], ye_ref.at[pl.ds(b * rows, rows), :], sem_out)

        def start(b, _):
            fill(b).start()
            return 0

        def wait(b, _):
            fill(b).wait()
            return 0
        lax.fori_loop(used, n_blocks, start, 0)
        lax.fori_loop(used, n_blocks, wait, 0)


def expert_mlp(slot_tok, items, hn, w_gate_up, b_gate_up, w_down, b_down, n_items, n_blocks):
    item_e, item_b0, item_nb, counts = items
    n_exp, d, two_f = w_gate_up.shape
    d_ff = two_f // 2
    tf = MOE_FTILE
    n_f = d_ff // tf
    rows, group = MOE_ROWS, MOE_GROUP
    rs = rows * group
    assert rs % (n_f * group) == 0

    def fi(k, f, nv):
        return jnp.where(k < nv[0], f, n_f - 1)

    in_specs = [
        pl.BlockSpec(memory_space=pl.ANY),
        pl.BlockSpec((None, d, tf), lambda k, f, tok, ie, ib, inb, nv: (ie[k], 0, fi(k, f, nv))),
        pl.BlockSpec((None, d, tf), lambda k, f, tok, ie, ib, inb, nv: (ie[k], 0, n_f + fi(k, f, nv))),
        pl.BlockSpec((None, 1, tf), lambda k, f, tok, ie, ib, inb, nv: (ie[k], 0, fi(k, f, nv))),
        pl.BlockSpec((None, 1, tf), lambda k, f, tok, ie, ib, inb, nv: (ie[k], 0, n_f + fi(k, f, nv))),
        pl.BlockSpec((None, tf, d), lambda k, f, tok, ie, ib, inb, nv: (ie[k], fi(k, f, nv), 0)),
        pl.BlockSpec((None, 1, d), lambda k, f, tok, ie, ib, inb, nv: (ie[k], 0, 0)),
    ]
    return pl.pallas_call(
        functools.partial(_expert_kernel, rows=rows, group=group, n_f=n_f, n_items=n_items, n_blocks=n_blocks),
        out_shape=jax.ShapeDtypeStruct((n_blocks * rows, d), F32),
        grid_spec=pltpu.PrefetchScalarGridSpec(
            num_scalar_prefetch=5, grid=(n_items, n_f),
            in_specs=in_specs,
            out_specs=pl.BlockSpec(memory_space=pl.ANY),
            scratch_shapes=[pltpu.VMEM((2, rs, d), F32), pltpu.VMEM((rs, d), F32),
                            pltpu.VMEM((d, tf), BF16), pltpu.VMEM((d, tf), BF16), pltpu.VMEM((tf, d), BF16),
                            pltpu.SemaphoreType.DMA((2,)), pltpu.SemaphoreType.DMA]),
        compiler_params=_params(("arbitrary", "arbitrary"), VMEM_LIMIT_BYTES),
        name="moe_experts",
    )(slot_tok, item_e, item_b0, item_nb, counts, hn, w_gate_up, w_gate_up,
      b_gate_up.reshape(n_exp, 1, two_f), b_gate_up.reshape(n_exp, 1, two_f),
      w_down, b_down.reshape(n_exp, 1, d))


def _combine_kernel(pos_ref, ye_ref, w_ref, h_ref, g_ref, o_ref, buf, sems, *, tc, blk0, n_steps, final_norm):
    i = pl.program_id(0)
    n_rows = TOP_K * tc

    def copy(r, src_row, par):
        return pltpu.make_async_copy(ye_ref.at[pl.ds(src_row, 1), :], buf.at[par, pl.ds(r, 1), :],
                                     sems.at[par])

    def issue(step, par):
        def body(g, _):
            for j in range(8):
                r = g * 8 + j
                copy(r, pos_ref[(blk0 + step) * n_rows + r], par).start(priority=j % 2)
            return 0
        lax.fori_loop(0, n_rows // 8, body, 0)

    @pl.when(i == 0)
    def _():
        issue(0, 0)

    @pl.when(i + 1 < n_steps)
    def _():
        issue(i + 1, (i + 1) % 2)

    par = i % 2

    def drain(r, _):
        copy(0, 0, par).wait()
        return 0
    lax.fori_loop(0, n_rows, drain, 0, unroll=8)

    w = w_ref[...]
    y = h_ref[...]
    for k in range(TOP_K):
        y = y + w[:, k:k + 1] * buf[par, k * tc:(k + 1) * tc, :]
    o_ref[...] = _rms(y, g_ref[...]) if final_norm else y


def combine(pos, ye, top_w, h, g, row0, rows, final_norm, tc=COMBINE_ROWS):
    d = h.shape[1]
    blk0 = row0 // tc
    n_steps = rows // tc
    return pl.pallas_call(
        functools.partial(_combine_kernel, tc=tc, blk0=blk0, n_steps=n_steps, final_norm=final_norm),
        out_shape=jax.ShapeDtypeStruct((rows, d), F32),
        grid_spec=pltpu.PrefetchScalarGridSpec(
            num_scalar_prefetch=1, grid=(n_steps,),
            in_specs=[pl.BlockSpec(memory_space=pl.ANY),
                      pl.BlockSpec((tc, TOP_K), lambda i, p: (blk0 + i, 0)),
                      pl.BlockSpec((tc, d), lambda i, p: (blk0 + i, 0)),
                      pl.BlockSpec((1, d), lambda i, p: (0, 0))],
            out_specs=pl.BlockSpec((tc, d), lambda i, p: (i, 0)),
            scratch_shapes=[pltpu.VMEM((2, TOP_K * tc, d), F32), pltpu.SemaphoreType.DMA((2,))]),
        compiler_params=_params(("arbitrary",), VMEM_LIMIT_BYTES),
        name="moe_combine",
    )(pos, ye, top_w, h, g)


def route(logits, tc):
    m, n_exp = logits.shape
    a = m * TOP_K
    rows, group = MOE_ROWS, MOE_GROUP
    top_logit, top_idx = lax.top_k(logits, TOP_K)
    top_w = jax.nn.softmax(top_logit, axis=-1)
    flat_e = top_idx.reshape(a)
    onehot = (flat_e[:, None] == jnp.arange(n_exp, dtype=jnp.int32)[None, :]).astype(jnp.int32)
    csum = jnp.cumsum(onehot, axis=0)
    counts = csum[-1]
    rank = jnp.take_along_axis(csum, flat_e[:, None], axis=1)[:, 0] - 1
    nb_e = (counts + rows - 1) // rows
    blk_start = jnp.cumsum(nb_e) - nb_e
    dest = blk_start[flat_e] * rows + rank
    n_blocks = a // rows + n_exp
    slot_tok = jnp.zeros(((n_blocks + group) * rows,), jnp.int32).at[dest].set(
        jnp.arange(a, dtype=jnp.int32) // TOP_K)
    ni_e = (nb_e + group - 1) // group
    item_end = jnp.cumsum(ni_e)
    n_valid = item_end[-1]
    n_items = (n_blocks + n_exp * (group - 1)) // group
    kk = jnp.arange(n_items, dtype=jnp.int32)
    item_e = jnp.minimum(jnp.sum(item_end[None, :] <= kk[:, None], axis=1), n_exp - 1).astype(jnp.int32)
    local = kk - (item_end - ni_e)[item_e]
    valid = kk < n_valid
    item_e = jnp.where(valid, item_e, item_e[jnp.maximum(n_valid - 1, 0)])
    item_b0 = jnp.where(valid, blk_start[item_e] + local * group, 0).astype(jnp.int32)
    item_nb = jnp.where(valid, jnp.clip(nb_e[item_e] - local * group, 0, group), 0).astype(jnp.int32)
    pos = dest.reshape(m // tc, tc, TOP_K).transpose(0, 2, 1).reshape(a)
    counts = jnp.stack([n_valid, jnp.sum(nb_e)]).astype(jnp.int32)
    items = (item_e, item_b0, item_nb, counts)
    return top_w, slot_tok, pos.astype(jnp.int32), items, n_items, n_blocks


def _rope_tables(pos, half):
    inv = jnp.power(ROPE_THETA, -jnp.arange(half, dtype=F32) / half)
    ang = pos.astype(F32)[:, None] * inv[None, :]
    return jnp.cos(ang), jnp.sin(ang)


def kernel(x_prompt, x_sample, cache_latent, cache_k_rope, state_retention, g_mix_norm, w_in, g_q_norm, w_uq, g_kv_norm, w_uk, w_uv, w_o, g_ffn_norm, w_router, b_router, w_gate_up, b_gate_up, w_down, b_down, g_final):
    bp, t, d = x_prompt.shape
    db, ts, _ = x_sample.shape
    depth = w_in.shape[0]
    past = cache_latent.shape[2]
    ql, ha, qk = w_uq.shape[1:]
    kl = w_uk.shape[1]
    rope = cache_k_rope.shape[-1]
    nope = qk - rope
    hr, dk = state_retention.shape[2:4]
    assert bp == 1 and nope == LANES and w_uv.shape[-1] == LANES and 2 * rope == LANES
    assert past % CHUNK == 0 and ts <= CHUNK
    mp, ms = bp * t, db * ts
    m = mp + ms
    tk_s = past + ts
    mla_scale = float(qk) ** -0.5 * 1.4426950408889634

    pos = jnp.concatenate([jnp.arange(t, dtype=jnp.int32),
                           jnp.tile(past + jnp.arange(ts, dtype=jnp.int32), db)])
    cos_a, sin_a = _rope_tables(pos, rope // 2)
    z = lambda w: jnp.zeros((m, w), F32)
    rope_c = jnp.concatenate([cos_a, cos_a, z(LANES - rope)], axis=1)
    rope_sa = jnp.concatenate([z(rope // 2), sin_a, z(LANES - rope)], axis=1)
    rope_sb = jnp.concatenate([-sin_a, z(LANES - rope // 2)], axis=1)
    cos_r, sin_r = _rope_tables(pos, dk // 2)
    tabs_p = retention_tables(RET_CHUNK, hr, dk)
    tabs_s = retention_tables(ts, hr, dk)

    xp, xs = x_prompt.reshape(mp, d), x_sample.reshape(ms, d)
    lat_p, kr_p, st_p, lat_s, kr_s, st_s = [], [], [], [], [], []
    small = ql + kl + rope
    for l in range(depth):
        wa = jnp.pad(w_in[l][:, :small], ((0, 0), (0, LANES - rope))).astype(BF16)
        wb = w_in[l][:, small:].astype(BF16)
        wq = jnp.pad(w_uq[l], ((0, 0), (0, 0), (0, 2 * LANES - qk))).reshape(ql, ha * 2 * LANES).astype(BF16)
        wk = w_uk[l].reshape(kl, ha * nope).astype(BF16)
        wv = w_uv[l].reshape(kl, ha * LANES).astype(BF16)
        wo = w_o[l].astype(BF16)

        xn = norm_cast(xp, xs, g_mix_norm[l].reshape(1, d))
        cqn, ckv, ckv_b, kr, kr_pad = proj_small(xn, wa, g_q_norm[l].reshape(1, ql), g_kv_norm[l].reshape(1, kl),
                                                 rope_c, rope_sa, rope_sb, ql, kl, rope)
        big = proj_big(xn, wb, cos_r, sin_r, 2 * hr * dk, dk, float(dk) ** -0.5)
        qcat = q_proj(cqn, wq, rope_c, rope_sa, rope_sb, ha, mla_scale)

        k_p, vt_p = kv_expand(ckv_b, kr_pad, wk, wv.T, mp, ha)
        oa_p = attention_causal(qcat, k_p, vt_p, ha, t=mp, tq=512)
        c_all = jnp.concatenate([cache_latent[l], ckv[mp:].reshape(db, ts, kl)], axis=1)
        kr_all = jnp.concatenate([cache_k_rope[l], kr[mp:].reshape(db, ts, rope)], axis=1)
        kr_all = jnp.pad(kr_all, ((0, 0), (0, 0), (0, LANES - rope)))
        oa_s = attention_absorbed(qcat, c_all.reshape(db * tk_s, kl).astype(BF16),
                                  kr_all.reshape(db * tk_s, LANES).astype(BF16),
                                  w_uk[l].transpose(1, 2, 0).astype(BF16), w_uv[l].transpose(1, 0, 2).astype(BF16),
                                  ha, batch=db, tq=ts, q_row0=mp, tk=tk_s)

        or_p, s_p = retention(big, jnp.zeros((bp, hr, dk, dk), F32), tabs_p, hr, dk,
                              batch=bp, t_total=t, row0=0, chunk=RET_CHUNK)
        or_s, s_s = retention(big, state_retention[l].astype(F32), tabs_s, hr, dk,
                              batch=db, t_total=ts, row0=mp, chunk=ts)

        h1, hn, logits = mixer_out(oa_p, oa_s, or_p, or_s, big, 3, xp, xs, wo, g_ffn_norm[l].reshape(1, d),
                                   w_router[l], b_router[l].reshape(1, -1))

        top_w, slot_tok, pos_flat, items, n_items, n_blocks = route(logits, COMBINE_ROWS)
        ye = expert_mlp(slot_tok, items, hn, w_gate_up[l], b_gate_up[l], w_down[l], b_down[l], n_items, n_blocks)
        last = l == depth - 1
        g_out = g_final.reshape(1, d)
        y_p = combine(pos_flat, ye, top_w, h1, g_out, 0, mp, last)
        y_s = combine(pos_flat, ye, top_w, h1, g_out, mp, ms, last)
        xp, xs = y_p, y_s

        lat_p.append(ckv[:mp].reshape(bp, t, kl))
        kr_p.append(kr[:mp].reshape(bp, t, rope))
        st_p.append(s_p)
        lat_s.append(ckv[mp:].reshape(db, ts, kl))
        kr_s.append(kr[mp:].reshape(db, ts, rope))
        st_s.append(s_s)

    return (y_p.reshape(bp, t, d), y_s.reshape(db, ts, d),
            jnp.stack(lat_p), jnp.stack(kr_p), jnp.stack(st_p),
            jnp.stack(lat_s), jnp.stack(kr_s), jnp.stack(st_s))
```

```python
import functools

import jax
import jax.numpy as jnp
from jax import lax
from jax.experimental import pallas as pl
from jax.experimental.pallas import tpu as pltpu

F32 = jnp.float32
BF16 = jnp.bfloat16

NORM_EPS = 1e-6
ROPE_THETA = 10000.0
CHUNK = 64
TOP_K = 4
SWIGLU_LIMIT = 7.0
SWIGLU_ALPHA = 1.702

LANES = 128
VMEM_LIMIT_BYTES = 56 * 1024 * 1024

RET_CHUNK = 256
MOE_ROWS = 128
MOE_GROUP = 10
MOE_CHAIN = 8
PROJ_ROWS = 1152
MOE_FTILE = 256
COMBINE_ROWS = 128
MIXER_SUBTILES = 2
ABSORBED_HEADS = 4


def _params(sem, vmem=None):
    return pltpu.CompilerParams(dimension_semantics=sem, vmem_limit_bytes=vmem)


def _rms(x, g):
    return x * lax.rsqrt(jnp.mean(x * x, axis=-1, keepdims=True) + NORM_EPS) * g


def _sigmoid(x):
    return 1.0 / (1.0 + jnp.exp(-x))


def _group_specs(rows_p, rows_s, tm, d):
    npb, nsb = rows_p // tm, rows_s // tm
    prompt = pl.BlockSpec((tm, d), lambda i: (jnp.minimum(i, npb - 1), 0))
    sample = pl.BlockSpec((tm, d), lambda i: (jnp.clip(i - npb, 0, nsb - 1), 0))
    return npb, prompt, sample


def _norm_cast_kernel(xp_ref, xs_ref, g_ref, o_ref, *, prompt_blocks):
    x = jnp.where(pl.program_id(0) < prompt_blocks, xp_ref[...], xs_ref[...])
    o_ref[...] = _rms(x, g_ref[...]).astype(o_ref.dtype)


def norm_cast(xp, xs, g, tm=512):
    d = xp.shape[1]
    m = xp.shape[0] + xs.shape[0]
    npb, prompt, sample = _group_specs(xp.shape[0], xs.shape[0], tm, d)
    return pl.pallas_call(
        functools.partial(_norm_cast_kernel, prompt_blocks=npb),
        out_shape=jax.ShapeDtypeStruct((m, d), BF16),
        grid=(m // tm,),
        in_specs=[prompt, sample, pl.BlockSpec((1, d), lambda i: (0, 0))],
        out_specs=pl.BlockSpec((tm, d), lambda i: (i, 0)),
        compiler_params=_params(("parallel",), VMEM_LIMIT_BYTES),
        name="norm_cast",
    )(xp, xs, g)


def _rope_hi(x, c, sa, sb):
    return x * c + pltpu.roll(x, 32, 1) * sa + pltpu.roll(x, 96, 1) * sb


def _proj_small_kernel(x_ref, w_ref, gq_ref, gkv_ref, c_ref, sa_ref, sb_ref,
                       cqn_ref, ckv_ref, ckvb_ref, kr_ref, krp_ref, *, ql, kl, rope):
    acc = jnp.dot(x_ref[...], w_ref[...], preferred_element_type=F32)
    cqn_ref[...] = _rms(acc[:, :ql], gq_ref[...]).astype(cqn_ref.dtype)
    ckv = _rms(acc[:, ql:ql + kl], gkv_ref[...])
    ckv_ref[...] = ckv
    ckvb_ref[...] = ckv.astype(ckvb_ref.dtype)
    kr = _rope_hi(acc[:, ql + kl:ql + kl + LANES], c_ref[...], sa_ref[...], sb_ref[...])
    kr_ref[...] = kr[:, :rope]
    krp_ref[...] = kr.astype(krp_ref.dtype)


def proj_small(xn, wa, gq, gkv, c, sa, sb, ql, kl, rope, tm=256):
    m, d = xn.shape
    n = wa.shape[1]
    row = lambda w: pl.BlockSpec((tm, w), lambda i: (i, 0))
    full = lambda a: pl.BlockSpec(a.shape, lambda i: (0, 0))
    return pl.pallas_call(
        functools.partial(_proj_small_kernel, ql=ql, kl=kl, rope=rope),
        out_shape=(jax.ShapeDtypeStruct((m, ql), BF16),
                   jax.ShapeDtypeStruct((m, kl), F32),
                   jax.ShapeDtypeStruct((m, kl), BF16),
                   jax.ShapeDtypeStruct((m, rope), F32),
                   jax.ShapeDtypeStruct((m, LANES), BF16)),
        grid=(m // tm,),
        in_specs=[row(d), full(wa), full(gq), full(gkv), row(LANES), row(LANES), row(LANES)],
        out_specs=(row(ql), row(kl), row(kl), row(rope), row(LANES)),
        compiler_params=_params(("parallel",), VMEM_LIMIT_BYTES),
        name="proj_small",
    )(xn, wa, gq, gkv, c, sa, sb)


def _proj_big_kernel(x_ref, w_ref, cos_ref, sin_ref, o_ref, *, n_rope, dk, k_scale):
    j = pl.program_id(0)
    acc = jnp.dot(x_ref[...], w_ref[...], preferred_element_type=F32)

    @pl.when(j >= n_rope)
    def _():
        o_ref[...] = acc.astype(o_ref.dtype)

    @pl.when(j < n_rope)
    def _():
        scale = jnp.where(j >= n_rope // 2, k_scale, 1.0).astype(F32)
        c = cos_ref[...] * scale
        s = sin_ref[...] * scale
        half = dk // 2
        for h in range(acc.shape[1] // dk):
            x1 = acc[:, h * dk:h * dk + half]
            x2 = acc[:, h * dk + half:(h + 1) * dk]
            o_ref[:, h * dk:h * dk + half] = (x1 * c - x2 * s).astype(o_ref.dtype)
            o_ref[:, h * dk + half:(h + 1) * dk] = (x2 * c + x1 * s).astype(o_ref.dtype)


def proj_big(xn, wb, cos_r, sin_r, rope_cols, dk, k_scale, tm=None, tn=1024):
    m, d = xn.shape
    n = wb.shape[1]
    if tm is None:
        tm = max(t for t in range(16, PROJ_ROWS + 1, 16) if m % t == 0)
    return pl.pallas_call(
        functools.partial(_proj_big_kernel, n_rope=rope_cols // tn, dk=dk, k_scale=k_scale),
        out_shape=jax.ShapeDtypeStruct((m, n), BF16),
        grid=(n // tn, m // tm),
        in_specs=[pl.BlockSpec((tm, d), lambda j, i: (i, 0)),
                  pl.BlockSpec((d, tn), lambda j, i: (0, j)),
                  pl.BlockSpec((tm, dk // 2), lambda j, i: (i, 0)),
                  pl.BlockSpec((tm, dk // 2), lambda j, i: (i, 0))],
        out_specs=pl.BlockSpec((tm, tn), lambda j, i: (i, j)),
        compiler_params=_params(("parallel", "parallel"), VMEM_LIMIT_BYTES),
        name="proj_big",
    )(xn, wb, cos_r, sin_r)


def _q_proj_kernel(x_ref, w_ref, c_ref, sa_ref, sb_ref, o_ref, *, heads, scale):
    acc = jnp.dot(x_ref[...], w_ref[...], preferred_element_type=F32)
    c = c_ref[...] * scale
    sa = sa_ref[...] * scale
    sb = sb_ref[...] * scale
    for h in range(heads):
        lo = 2 * h * LANES
        o_ref[:, lo:lo + LANES] = (acc[:, lo:lo + LANES] * scale).astype(o_ref.dtype)
        o_ref[:, lo + LANES:lo + 2 * LANES] = _rope_hi(
            acc[:, lo + LANES:lo + 2 * LANES], c, sa, sb).astype(o_ref.dtype)


def q_proj(cqn, wq, c, sa, sb, heads, scale, tm=256):
    m, k = cqn.shape
    n = wq.shape[1]
    row = lambda w: pl.BlockSpec((tm, w), lambda i: (i, 0))
    return pl.pallas_call(
        functools.partial(_q_proj_kernel, heads=heads, scale=scale),
        out_shape=jax.ShapeDtypeStruct((m, n), BF16),
        grid=(m // tm,),
        in_specs=[row(k), pl.BlockSpec((k, n), lambda i: (0, 0)), row(LANES), row(LANES), row(LANES)],
        out_specs=row(n),
        compiler_params=_params(("parallel",), VMEM_LIMIT_BYTES),
        name="q_proj",
    )(cqn, wq, c, sa, sb)


def _kv_expand_kernel(c_ref, kr_ref, wk_ref, wvt_ref, k_ref, vt_ref, *, heads):
    c = c_ref[...]
    kn = jnp.dot(c, wk_ref[...], preferred_element_type=F32)
    vt_ref[...] = lax.dot_general(wvt_ref[...], c, (((1,), (1,)), ((), ())),
                                  preferred_element_type=F32).astype(vt_ref.dtype)
    kr = kr_ref[...]
    for h in range(heads):
        k_ref[:, 2 * h * LANES:(2 * h + 1) * LANES] = kn[:, h * LANES:(h + 1) * LANES].astype(k_ref.dtype)
        k_ref[:, (2 * h + 1) * LANES:(2 * h + 2) * LANES] = kr


def kv_expand(c_bf, kr_pad, wk, wvt, rows, heads, tm=256):
    kl = c_bf.shape[1]
    dv = heads * LANES
    row = lambda w: pl.BlockSpec((tm, w), lambda i: (i, 0))
    full = lambda a: pl.BlockSpec(a.shape, lambda i: (0, 0))
    return pl.pallas_call(
        functools.partial(_kv_expand_kernel, heads=heads),
        out_shape=(jax.ShapeDtypeStruct((rows, 2 * heads * LANES), BF16),
                   jax.ShapeDtypeStruct((dv, rows), BF16)),
        grid=(rows // tm,),
        in_specs=[row(kl), row(LANES), full(wk), full(wvt)],
        out_specs=(row(2 * heads * LANES), pl.BlockSpec((dv, tm), lambda i: (0, i))),
        compiler_params=_params(("parallel",), VMEM_LIMIT_BYTES),
        name="kv_expand",
    )(c_bf, kr_pad, wk, wvt)


def _attn_absorbed_kernel(q_ref, c_ref, kr_ref, wukt_ref, wuv_ref, o_ref, *, heads):
    tq = q_ref.shape[0]
    c = c_ref[...]
    kr = kr_ref[...]
    nt = (((1,), (1,)), ((), ()))
    for g in range(heads // ABSORBED_HEADS):
        hs = range(g * ABSORBED_HEADS, (g + 1) * ABSORBED_HEADS)
        q_lat = jnp.concatenate(
            [jnp.dot(q_ref[:, 2 * h * LANES:(2 * h + 1) * LANES], wukt_ref[h],
                     preferred_element_type=F32).astype(BF16) for h in hs], axis=0)
        q_rope = jnp.concatenate([q_ref[:, (2 * h + 1) * LANES:(2 * h + 2) * LANES] for h in hs], axis=0)
        s = (lax.dot_general(q_lat, c, nt, preferred_element_type=F32)
             + lax.dot_general(q_rope, kr, nt, preferred_element_type=F32))
        p = jnp.exp2(s - jnp.max(s, axis=-1, keepdims=True))
        l = jnp.sum(p, axis=-1, keepdims=True)
        o_lat = (jnp.dot(p.astype(BF16), c, preferred_element_type=F32) / l).astype(BF16)
        for i, h in enumerate(hs):
            o_ref[:, h * LANES:(h + 1) * LANES] = jnp.dot(
                o_lat[i * tq:(i + 1) * tq], wuv_ref[h], preferred_element_type=F32).astype(o_ref.dtype)


def attention_absorbed(q, c_all, kr_all, wukt, wuv, heads, *, batch, tq, q_row0, tk):
    assert heads % ABSORBED_HEADS == 0 and q_row0 % tq == 0
    kl = c_all.shape[1]
    qb0 = q_row0 // tq
    full = lambda a: pl.BlockSpec(a.shape, lambda b: (0, 0, 0))
    return pl.pallas_call(
        functools.partial(_attn_absorbed_kernel, heads=heads),
        out_shape=jax.ShapeDtypeStruct((batch * tq, heads * LANES), BF16),
        grid=(batch,),
        in_specs=[pl.BlockSpec((tq, 2 * heads * LANES), lambda b: (qb0 + b, 0)),
                  pl.BlockSpec((tk, kl), lambda b: (b, 0)),
                  pl.BlockSpec((tk, LANES), lambda b: (b, 0)),
                  full(wukt), full(wuv)],
        out_specs=pl.BlockSpec((tq, heads * LANES), lambda b: (b, 0)),
        compiler_params=_params(("parallel",), VMEM_LIMIT_BYTES),
        name="attention_absorbed",
    )(q, c_all, kr_all, wukt, wuv)


def _attn_t_kernel(q_ref, k_ref, vt_ref, bias_ref, o_ref, sa_ref, sb_ref, *, tq, tk):
    i = pl.program_id(1)
    dq, dv = 2 * LANES, LANES
    n_heads = q_ref.shape[1] // dq

    def scores(j, s_ref):
        off = pl.multiple_of(j * tk, tk)
        for h in range(n_heads):
            q = q_ref[:, h * dq:(h + 1) * dq]
            k = k_ref[pl.ds(off, tk), h * dq:(h + 1) * dq]
            s_ref[h] = lax.dot_general(k, q, (((1,), (1,)), ((), ())), preferred_element_type=F32)

    def softmax_pv(j, s_ref, carries, masked):
        off = pl.multiple_of(j * tk, tk)
        out = []
        for h in range(n_heads):
            m, l, acc = carries[h]
            s = s_ref[h]
            if masked:
                s = jnp.minimum(s, bias_ref[...])
            m_new = jnp.maximum(m, jnp.max(s, axis=0, keepdims=True))
            alpha = jnp.exp2(m - m_new)
            p = jnp.exp2(s - m_new)
            l = alpha * l + jnp.sum(p, axis=0, keepdims=True)
            vt = vt_ref[h * dv:(h + 1) * dv, pl.ds(off, tk)]
            acc = alpha * acc + jnp.dot(vt, p.astype(BF16), preferred_element_type=F32)
            out.append((m_new, l, acc))
        return tuple(out)

    def finish(carries):
        for h, (_, l, acc) in enumerate(carries):
            o_ref[:, h * dv:(h + 1) * dv] = (acc / l).T.astype(o_ref.dtype)

    scores(0, sa_ref)

    def pair(t, carries):
        scores(2 * t + 1, sb_ref)
        carries = softmax_pv(2 * t, sa_ref, carries, False)
        scores(2 * t + 2, sa_ref)
        return softmax_pv(2 * t + 1, sb_ref, carries, False)

    init = (jnp.full((1, tq), -jnp.inf, F32), jnp.zeros((1, tq), F32), jnp.zeros((dv, tq), F32))
    carries = lax.fori_loop(0, i // 2, pair, (init,) * n_heads)

    @pl.when(i % 2 == 0)
    def _():
        finish(softmax_pv(i, sa_ref, carries, True))

    @pl.when(i % 2 == 1)
    def _():
        scores(i, sb_ref)
        finish(softmax_pv(i, sb_ref, softmax_pv(i - 1, sa_ref, carries, False), True))


def attention_causal(q, k, vt, heads, *, t, tq, hp=2):
    assert tq % CHUNK == 0 and heads % hp == 0
    dq = 2 * LANES
    chunk_of = jnp.arange(tq, dtype=jnp.int32) // CHUNK
    ceiling = jnp.where(chunk_of[:, None] <= chunk_of[None, :],
                        jnp.finfo(F32).max, -1e30).astype(F32)
    return pl.pallas_call(
        functools.partial(_attn_t_kernel, tq=tq, tk=tq),
        out_shape=jax.ShapeDtypeStruct((t, heads * LANES), BF16),
        grid=(heads // hp, t // tq),
        in_specs=[pl.BlockSpec((tq, hp * dq), lambda h, i: (i, h)),
                  pl.BlockSpec((t, hp * dq), lambda h, i: (0, h)),
                  pl.BlockSpec((hp * LANES, t), lambda h, i: (h, 0)),
                  pl.BlockSpec((tq, tq), lambda h, i: (0, 0))],
        out_specs=pl.BlockSpec((tq, hp * LANES), lambda h, i: (i, h)),
        scratch_shapes=[pltpu.VMEM((hp, tq, tq), F32), pltpu.VMEM((hp, tq, tq), F32)],
        compiler_params=_params(("parallel", "arbitrary"), VMEM_LIMIT_BYTES),
        name="attention_causal",
    )(q, k, vt, ceiling)


def _retention_kernel(q_ref, k_ref, v_ref, s0_ref, d_ref, xi_ref, zeta_ref, gc_ref,
                      o_ref, sout_ref, s_scr, *, n_chunks):
    c = pl.program_id(2)
    dk = s_scr.shape[-1]

    @pl.when(c == 0)
    def _():
        s_scr[...] = s0_ref[...]

    for h in range(s_scr.shape[0]):
        cols = slice(h * dk, (h + 1) * dk)
        q = q_ref[:, cols]
        k = k_ref[:, cols]
        v = v_ref[:, cols]
        s = s_scr[h]
        a = lax.dot_general(q, k, (((1,), (1,)), ((), ())), preferred_element_type=F32) * d_ref[h]
        o = jnp.dot(a.astype(BF16), v, preferred_element_type=F32)
        o = o + jnp.dot(q, s.astype(BF16), preferred_element_type=F32) * xi_ref[h]
        kz = (k.astype(F32) * zeta_ref[h]).astype(BF16)
        s_new = s * gc_ref[h] + lax.dot_general(kz, v, (((0,), (0,)), ((), ())),
                                                preferred_element_type=F32)
        s_scr[h] = s_new
        o_ref[:, cols] = (o * lax.rsqrt(jnp.mean(o * o, axis=-1, keepdims=True) + NORM_EPS)).astype(o_ref.dtype)

    @pl.when(c == n_chunks - 1)
    def _():
        sout_ref[...] = s_scr[...]


def retention(big, s0, tabs, heads, dk, *, batch, t_total, row0, chunk, hp=4):
    d_tab, xi_tab, zeta_tab, gc_tab = tabs
    assert heads % hp == 0
    nc = t_total // chunk
    rb0 = row0 // chunk
    hg = heads // hp
    rowblk = lambda g0: pl.BlockSpec((chunk, hp * dk), lambda b, h, c: (rb0 + b * nc + c, g0 + h))
    per_head = lambda a: pl.BlockSpec((hp,) + a.shape[1:], lambda b, h, c: (h, 0, 0))
    st_spec = pl.BlockSpec((None, hp, dk, dk), lambda b, h, c: (b, h, 0, 0))
    return pl.pallas_call(
        functools.partial(_retention_kernel, n_chunks=nc),
        out_shape=(jax.ShapeDtypeStruct((batch * t_total, heads * dk), BF16),
                   jax.ShapeDtypeStruct((batch, heads, dk, dk), F32)),
        grid=(batch, hg, nc),
        in_specs=[rowblk(0), rowblk(hg), rowblk(2 * hg), st_spec,
                  per_head(d_tab), per_head(xi_tab), per_head(zeta_tab), per_head(gc_tab)],
        out_specs=(pl.BlockSpec((chunk, hp * dk), lambda b, h, c: (b * nc + c, h)), st_spec),
        scratch_shapes=[pltpu.VMEM((hp, dk, dk), F32)],
        compiler_params=_params(("parallel", "parallel", "arbitrary"), VMEM_LIMIT_BYTES),
        name="retention",
    )(big, big, big, s0, d_tab, xi_tab, zeta_tab, gc_tab)


def retention_tables(length, heads, dk):
    lg = jnp.log1p(-jnp.exp2(-5.0 - jnp.arange(heads, dtype=F32)))
    idx = jnp.arange(length, dtype=F32)
    diff = idx[:, None] - idx[None, :]
    decay = jnp.where(diff >= 0, jnp.exp(jnp.maximum(diff, 0.0)[None] * lg[:, None, None]), 0.0)
    xi = jnp.exp((idx + 1.0)[None, :] * lg[:, None])
    zeta = jnp.exp((length - 1.0 - idx)[None, :] * lg[:, None])
    gc = jnp.exp(length * lg)
    bc = lambda a: jnp.broadcast_to(a[:, :, None], (heads, a.shape[1], dk))
    return decay, bc(xi), bc(zeta), jnp.broadcast_to(gc[:, None, None], (heads, 1, dk))


def _mixer_out_kernel(oap_ref, oas_ref, orp_ref, ors_ref, rg_ref, ga_ref, gb_ref, xp_ref, xs_ref, wo_ref,
                      g_ref, wrh_ref, wrl_ref, br_ref, h_ref, hn_ref, lg_ref, *, prompt_blocks):
    is_prompt = pl.program_id(0) < prompt_blocks
    tm = h_ref.shape[0]
    sub = tm // MIXER_SUBTILES
    for s in range(MIXER_SUBTILES):
        r = slice(s * sub, (s + 1) * sub)
        oa = jnp.where(is_prompt, oap_ref[r, :], oas_ref[r, :]).astype(F32)
        orn = jnp.where(is_prompt, orp_ref[r, :], ors_ref[r, :]).astype(F32)
        x = jnp.where(is_prompt, xp_ref[r, :], xs_ref[r, :])
        rg = rg_ref[r, :].astype(F32)
        yb = rg * _sigmoid(rg) * orn
        merged = _sigmoid(ga_ref[r, :].astype(F32)) * oa + _sigmoid(gb_ref[r, :].astype(F32)) * yb
        h = x + jnp.dot(merged.astype(BF16), wo_ref[...], preferred_element_type=F32)
        h_ref[r, :] = h
        hn = _rms(h, g_ref[...])
        hn_ref[r, :] = hn
        hn_hi = hn.astype(BF16)
        hn_lo = (hn - hn_hi.astype(F32)).astype(BF16)
        lg_ref[r, :] = (jnp.dot(hn_hi, wrh_ref[...], preferred_element_type=F32)
                        + jnp.dot(hn_lo, wrh_ref[...], preferred_element_type=F32)
                        + jnp.dot(hn_hi, wrl_ref[...], preferred_element_type=F32) + br_ref[...])


def mixer_out(oa_p, oa_s, or_p, or_s, big, gate_blk0, xp, xs, wo, g, wr, br, tm=256):
    d = xp.shape[1]
    m = xp.shape[0] + xs.shape[0]
    e = wr.shape[1]
    wr_hi = wr.astype(BF16)
    wr_lo = (wr - wr_hi.astype(F32)).astype(BF16)
    npb, prompt, sample = _group_specs(xp.shape[0], xs.shape[0], tm, d)
    row = lambda col: pl.BlockSpec((tm, d), lambda i: (i, col))
    full = lambda a: pl.BlockSpec(a.shape, lambda i: (0, 0))
    return pl.pallas_call(
        functools.partial(_mixer_out_kernel, prompt_blocks=npb),
        out_shape=(jax.ShapeDtypeStruct((m, d), F32),
                   jax.ShapeDtypeStruct((m, d), F32),
                   jax.ShapeDtypeStruct((m, e), F32)),
        grid=(m // tm,),
        in_specs=[prompt, sample, prompt, sample, row(gate_blk0), row(gate_blk0 + 1), row(gate_blk0 + 2),
                  prompt, sample, full(wo), full(g), full(wr_hi), full(wr_lo), full(br)],
        out_specs=(row(0), row(0), pl.BlockSpec((tm, e), lambda i: (i, 0))),
        compiler_params=_params(("parallel",), VMEM_LIMIT_BYTES),
        name="mixer_out",
    )(oa_p, oa_s, or_p, or_s, big, big, big, xp, xs, wo, g, wr_hi, wr_lo, br)


def _expert_kernel(tok_ref, ie_ref, ib_ref, inb_ref, nv_ref, hn_ref, wg_ref, wu_ref, bg_ref, bu_ref, wd_ref,
                   bd_ref, ye_ref, x32, y32, wgb, wub, wdb, sem_x, sem_out, *, rows, group, n_f, n_items,
                   n_blocks):
    k = pl.program_id(0)
    f = pl.program_id(1)
    nb = inb_ref[k]
    blk0 = ib_ref[k]
    n_valid = nv_ref[0]
    valid = k < n_valid
    has_next = k + 1 < n_valid
    last_f = f == n_f - 1
    cur = k % 2
    rs = rows * group
    d = y32.shape[1]
    per_block = rs // (n_f * group)

    def gather_copy(slot, buf, row):
        return pltpu.make_async_copy(hn_ref.at[pl.ds(tok_ref[slot], 1), :],
                                     x32.at[buf, pl.ds(row, 1), :], sem_x.at[buf])

    def out_copy(b):
        return pltpu.make_async_copy(y32.at[pl.ds(b * rows, rows), :],
                                     ye_ref.at[pl.ds((blk0 + b) * rows, rows), :], sem_out)

    @pl.when((k == 0) & (f == 0))
    def _():
        def body(r, _):
            gather_copy(blk0 * rows + r, 0, r).start()
            return 0
        lax.fori_loop(0, rs, body, 0, unroll=8)

    @pl.when(valid & (f == 0))
    def _():
        def drain(r, _):
            gather_copy(0, cur, 0).wait()
            return 0
        lax.fori_loop(0, rs, drain, 0, unroll=8)

        def init(b, _):
            y32[pl.ds(pl.multiple_of(b * rows, rows), rows), :] = jnp.broadcast_to(bd_ref[...], (rows, d))
            return 0
        lax.fori_loop(0, nb, init, 0)

    @pl.when(valid)
    def _():
        wgb[...] = wg_ref[...].astype(BF16)
        wub[...] = wu_ref[...].astype(BF16)
        wdb[...] = wd_ref[...].astype(BF16)

    def prefetch(b, next_slot0):
        row0 = (f * group + b) * per_block
        for r in range(per_block):
            gather_copy(next_slot0 + row0 + r, 1 - cur, row0 + r).start(priority=1)

    def compute_blocks(b0, n, next_slot0):
        r = pl.ds(pl.multiple_of(b0 * rows, rows), n * rows)
        x = x32[cur, r, :].astype(BF16)
        gate = jnp.minimum(jnp.dot(x, wgb[...], preferred_element_type=F32) + bg_ref[...], SWIGLU_LIMIT)
        up = jnp.clip(jnp.dot(x, wub[...], preferred_element_type=F32) + bu_ref[...],
                      -SWIGLU_LIMIT, SWIGLU_LIMIT)
        h = (up + 1.0) * (gate * _sigmoid(gate * SWIGLU_ALPHA))
        y32[r, :] += jnp.dot(h.astype(BF16), wdb[...], preferred_element_type=F32)
        if next_slot0 is not None:
            for i in range(n):
                prefetch(b0 + i, next_slot0)

        @pl.when(last_f)
        def _():
            for i in range(n):
                out_copy(b0 + i).start()

    def run(next_slot0):
        def chain(p, _):
            compute_blocks(MOE_CHAIN * p, MOE_CHAIN, next_slot0)
            return 0
        lax.fori_loop(0, nb // MOE_CHAIN, chain, 0)
        n = MOE_CHAIN // 2
        while n >= 1:
            @pl.when(nb & n != 0)
            def _(n=n):
                compute_blocks(nb - (nb & (2 * n - 1)), n, next_slot0)
            n //= 2

    @pl.when(has_next)
    def _():
        next_slot0 = ib_ref[k + 1] * rows
        run(next_slot0)

        def rest(b, _):
            prefetch(b, next_slot0)
            return 0
        lax.fori_loop(nb, group, rest, 0)

    @pl.when(valid & jnp.logical_not(has_next))
    def _():
        run(None)

    @pl.when(valid & last_f)
    def _():
        def body(b, _):
            out_copy(b).wait()
            return 0
        lax.fori_loop(0, nb, body, 0)

    @pl.when((k == n_items - 1) & last_f)
    def _():
        used = nv_ref[1]
        y32[0:rows, :] = jnp.zeros((rows, d), F32)

        def fill(b):
            return pltpu.make_async_copy(y32.at[pl.ds(0, rows), :], ye_ref.at[pl.ds(b * rows, rows), :], sem_out)

        def start(b, _):
            fill(b).start()
            return 0

        def wait(b, _):
            fill(b).wait()
            return 0
        lax.fori_loop(used, n_blocks, start, 0)
        lax.fori_loop(used, n_blocks, wait, 0)


def expert_mlp(slot_tok, items, hn, w_gate_up, b_gate_up, w_down, b_down, n_items, n_blocks):
    item_e, item_b0, item_nb, counts = items
    n_exp, d, two_f = w_gate_up.shape
    d_ff = two_f // 2
    tf = MOE_FTILE
    n_f = d_ff // tf
    rows, group = MOE_ROWS, MOE_GROUP
    rs = rows * group
    assert rs % (n_f * group) == 0

    def fi(k, f, nv):
        return jnp.where(k < nv[0], f, n_f - 1)

    in_specs = [
        pl.BlockSpec(memory_space=pl.ANY),
        pl.BlockSpec((None, d, tf), lambda k, f, tok, ie, ib, inb, nv: (ie[k], 0, fi(k, f, nv))),
        pl.BlockSpec((None, d, tf), lambda k, f, tok, ie, ib, inb, nv: (ie[k], 0, n_f + fi(k, f, nv))),
        pl.BlockSpec((None, 1, tf), lambda k, f, tok, ie, ib, inb, nv: (ie[k], 0, fi(k, f, nv))),
        pl.BlockSpec((None, 1, tf), lambda k, f, tok, ie, ib, inb, nv: (ie[k], 0, n_f + fi(k, f, nv))),
        pl.BlockSpec((None, tf, d), lambda k, f, tok, ie, ib, inb, nv: (ie[k], fi(k, f, nv), 0)),
        pl.BlockSpec((None, 1, d), lambda k, f, tok, ie, ib, inb, nv: (ie[k], 0, 0)),
    ]
    return pl.pallas_call(
        functools.partial(_expert_kernel, rows=rows, group=group, n_f=n_f, n_items=n_items, n_blocks=n_blocks),
        out_shape=jax.ShapeDtypeStruct((n_blocks * rows, d), F32),
        grid_spec=pltpu.PrefetchScalarGridSpec(
            num_scalar_prefetch=5, grid=(n_items, n_f),
            in_specs=in_specs,
            out_specs=pl.BlockSpec(memory_space=pl.ANY),
            scratch_shapes=[pltpu.VMEM((2, rs, d), F32), pltpu.VMEM((rs, d), F32),
                            pltpu.VMEM((d, tf), BF16), pltpu.VMEM((d, tf), BF16), pltpu.VMEM((tf, d), BF16),
                            pltpu.SemaphoreType.DMA((2,)), pltpu.SemaphoreType.DMA]),
        compiler_params=_params(("arbitrary", "arbitrary"), VMEM_LIMIT_BYTES),
        name="moe_experts",
    )(slot_tok, item_e, item_b0, item_nb, counts, hn, w_gate_up, w_gate_up,
      b_gate_up.reshape(n_exp, 1, two_f), b_gate_up.reshape(n_exp, 1, two_f),
      w_down, b_down.reshape(n_exp, 1, d))


def _combine_kernel(pos_ref, ye_ref, w_ref, h_ref, g_ref, o_ref, buf, sems, *, tc, blk0, n_steps, final_norm):
    i = pl.program_id(0)
    n_rows = TOP_K * tc

    def copy(r, src_row, par):
        return pltpu.make_async_copy(ye_ref.at[pl.ds(src_row, 1), :], buf.at[par, pl.ds(r, 1), :],
                                     sems.at[par])

    def issue(step, par):
        def body(g, _):
            for j in range(8):
                r = g * 8 + j
                copy(r, pos_ref[(blk0 + step) * n_rows + r], par).start(priority=j % 2)
            return 0
        lax.fori_loop(0, n_rows // 8, body, 0)

    @pl.when(i == 0)
    def _():
        issue(0, 0)

    @pl.when(i + 1 < n_steps)
    def _():
        issue(i + 1, (i + 1) % 2)

    par = i % 2

    def drain(r, _):
        copy(0, 0, par).wait()
        return 0
    lax.fori_loop(0, n_rows, drain, 0, unroll=8)

    w = w_ref[...]
    y = h_ref[...]
    for k in range(TOP_K):
        y = y + w[:, k:k + 1] * buf[par, k * tc:(k + 1) * tc, :]
    o_ref[...] = _rms(y, g_ref[...]) if final_norm else y


def combine(pos, ye, top_w, h, g, row0, rows, final_norm, tc=COMBINE_ROWS):
    d = h.shape[1]
    blk0 = row0 // tc
    n_steps = rows // tc
    return pl.pallas_call(
        functools.partial(_combine_kernel, tc=tc, blk0=blk0, n_steps=n_steps, final_norm=final_norm),
        out_shape=jax.ShapeDtypeStruct((rows, d), F32),
        grid_spec=pltpu.PrefetchScalarGridSpec(
            num_scalar_prefetch=1, grid=(n_steps,),
            in_specs=[pl.BlockSpec(memory_space=pl.ANY),
                      pl.BlockSpec((tc, TOP_K), lambda i, p: (blk0 + i, 0)),
                      pl.BlockSpec((tc, d), lambda i, p: (blk0 + i, 0)),
                      pl.BlockSpec((1, d), lambda i, p: (0, 0))],
            out_specs=pl.BlockSpec((tc, d), lambda i, p: (i, 0)),
            scratch_shapes=[pltpu.VMEM((2, TOP_K * tc, d), F32), pltpu.SemaphoreType.DMA((2,))]),
        compiler_params=_params(("arbitrary",), VMEM_LIMIT_BYTES),
        name="moe_combine",
    )(pos, ye, top_w, h, g)


def route(logits, tc):
    m, n_exp = logits.shape
    a = m * TOP_K
    rows, group = MOE_ROWS, MOE_GROUP
    top_logit, top_idx = lax.top_k(logits, TOP_K)
    top_w = jax.nn.softmax(top_logit, axis=-1)
    flat_e = top_idx.reshape(a)
    onehot = (flat_e[:, None] == jnp.arange(n_exp, dtype=jnp.int32)[None, :]).astype(jnp.int32)
    csum = jnp.cumsum(onehot, axis=0)
    counts = csum[-1]
    rank = jnp.take_along_axis(csum, flat_e[:, None], axis=1)[:, 0] - 1
    nb_e = (counts + rows - 1) // rows
    blk_start = jnp.cumsum(nb_e) - nb_e
    dest = blk_start[flat_e] * rows + rank
    n_blocks = a // rows + n_exp
    slot_tok = jnp.zeros(((n_blocks + group) * rows,), jnp.int32).at[dest].set(
        jnp.arange(a, dtype=jnp.int32) // TOP_K)
    ni_e = (nb_e + group - 1) // group
    item_end = jnp.cumsum(ni_e)
    n_valid = item_end[-1]
    n_items = (n_blocks + n_exp * (group - 1)) // group
    kk = jnp.arange(n_items, dtype=jnp.int32)
    item_e = jnp.minimum(jnp.sum(item_end[None, :] <= kk[:, None], axis=1), n_exp - 1).astype(jnp.int32)
    local = kk - (item_end - ni_e)[item_e]
    valid = kk < n_valid
    item_e = jnp.where(valid, item_e, item_e[jnp.maximum(n_valid - 1, 0)])
    item_b0 = jnp.where(valid, blk_start[item_e] + local * group, 0).astype(jnp.int32)
    item_nb = jnp.where(valid, jnp.clip(nb_e[item_e] - local * group, 0, group), 0).astype(jnp.int32)
    pos = dest.reshape(m // tc, tc, TOP_K).transpose(0, 2, 1).reshape(a)
    counts = jnp.stack([n_valid, jnp.sum(nb_e)]).astype(jnp.int32)
    items = (item_e, item_b0, item_nb, counts)
    return top_w, slot_tok, pos.astype(jnp.int32), items, n_items, n_blocks


def _rope_tables(pos, half):
    inv = jnp.power(ROPE_THETA, -jnp.arange(half, dtype=F32) / half)
    ang = pos.astype(F32)[:, None] * inv[None, :]
    return jnp.cos(ang), jnp.sin(ang)


def kernel(x_prompt, x_sample, cache_latent, cache_k_rope, state_retention, g_mix_norm, w_in, g_q_norm, w_uq, g_kv_norm, w_uk, w_uv, w_o, g_ffn_norm, w_router, b_router, w_gate_up, b_gate_up, w_down, b_down, g_final):
    bp, t, d = x_prompt.shape
    db, ts, _ = x_sample.shape
    depth = w_in.shape[0]
    past = cache_latent.shape[2]
    ql, ha, qk = w_uq.shape[1:]
    kl = w_uk.shape[1]
    rope = cache_k_rope.shape[-1]
    nope = qk - rope
    hr, dk = state_retention.shape[2:4]
    assert bp == 1 and nope == LANES and w_uv.shape[-1] == LANES and 2 * rope == LANES
    assert past % CHUNK == 0 and ts <= CHUNK
    mp, ms = bp * t, db * ts
    m = mp + ms
    tk_s = past + ts
    mla_scale = float(qk) ** -0.5 * 1.4426950408889634

    pos = jnp.concatenate([jnp.arange(t, dtype=jnp.int32),
                           jnp.tile(past + jnp.arange(ts, dtype=jnp.int32), db)])
    cos_a, sin_a = _rope_tables(pos, rope // 2)
    z = lambda w: jnp.zeros((m, w), F32)
    rope_c = jnp.concatenate([cos_a, cos_a, z(LANES - rope)], axis=1)
    rope_sa = jnp.concatenate([z(rope // 2), sin_a, z(LANES - rope)], axis=1)
    rope_sb = jnp.concatenate([-sin_a, z(LANES - rope // 2)], axis=1)
    cos_r, sin_r = _rope_tables(pos, dk // 2)
    tabs_p = retention_tables(RET_CHUNK, hr, dk)
    tabs_s = retention_tables(ts, hr, dk)

    xp, xs = x_prompt.reshape(mp, d), x_sample.reshape(ms, d)
    lat_p, kr_p, st_p, lat_s, kr_s, st_s = [], [], [], [], [], []
    small = ql + kl + rope
    for l in range(depth):
        wa = jnp.pad(w_in[l][:, :small], ((0, 0), (0, LANES - rope))).astype(BF16)
        wb = w_in[l][:, small:].astype(BF16)
        wq = jnp.pad(w_uq[l], ((0, 0), (0, 0), (0, 2 * LANES - qk))).reshape(ql, ha * 2 * LANES).astype(BF16)
        wk = w_uk[l].reshape(kl, ha * nope).astype(BF16)
        wv = w_uv[l].reshape(kl, ha * LANES).astype(BF16)
        wo = w_o[l].astype(BF16)

        xn = norm_cast(xp, xs, g_mix_norm[l].reshape(1, d))
        cqn, ckv, ckv_b, kr, kr_pad = proj_small(xn, wa, g_q_norm[l].reshape(1, ql), g_kv_norm[l].reshape(1, kl),
                                                 rope_c, rope_sa, rope_sb, ql, kl, rope)
        big = proj_big(xn, wb, cos_r, sin_r, 2 * hr * dk, dk, float(dk) ** -0.5)
        qcat = q_proj(cqn, wq, rope_c, rope_sa, rope_sb, ha, mla_scale)

        k_p, vt_p = kv_expand(ckv_b, kr_pad, wk, wv.T, mp, ha)
        oa_p = attention_causal(qcat, k_p, vt_p, ha, t=mp, tq=512)
        c_all = jnp.concatenate([cache_latent[l], ckv[mp:].reshape(db, ts, kl)], axis=1)
        kr_all = jnp.concatenate([cache_k_rope[l], kr[mp:].reshape(db, ts, rope)], axis=1)
        kr_all = jnp.pad(kr_all, ((0, 0), (0, 0), (0, LANES - rope)))
        oa_s = attention_absorbed(qcat, c_all.reshape(db * tk_s, kl).astype(BF16),
                                  kr_all.reshape(db * tk_s, LANES).astype(BF16),
                                  w_uk[l].transpose(1, 2, 0).astype(BF16), w_uv[l].transpose(1, 0, 2).astype(BF16),
                                  ha, batch=db, tq=ts, q_row0=mp, tk=tk_s)

        or_p, s_p = retention(big, jnp.zeros((bp, hr, dk, dk), F32), tabs_p, hr, dk,
                              batch=bp, t_total=t, row0=0, chunk=RET_CHUNK)
        or_s, s_s = retention(big, state_retention[l].astype(F32), tabs_s, hr, dk,
                              batch=db, t_total=ts, row0=mp, chunk=ts)

        h1, hn, logits = mixer_out(oa_p, oa_s, or_p, or_s, big, 3, xp, xs, wo, g_ffn_norm[l].reshape(1, d),
                                   w_router[l], b_router[l].reshape(1, -1))

        top_w, slot_tok, pos_flat, items, n_items, n_blocks = route(logits, COMBINE_ROWS)
        ye = expert_mlp(slot_tok, items, hn, w_gate_up[l], b_gate_up[l], w_down[l], b_down[l], n_items, n_blocks)
        last = l == depth - 1
        g_out = g_final.reshape(1, d)
        y_p = combine(pos_flat, ye, top_w, h1, g_out, 0, mp, last)
        y_s = combine(pos_flat, ye, top_w, h1, g_out, mp, ms, last)
        xp, xs = y_p, y_s

        lat_p.append(ckv[:mp].reshape(bp, t, kl))
        kr_p.append(kr[:mp].reshape(bp, t, rope))
        st_p.append(s_p)
        lat_s.append(ckv[mp:].reshape(db, ts, kl))
        kr_s.append(kr[mp:].reshape(db, ts, rope))
        st_s.append(s_s)

    return (y_p.reshape(bp, t, d), y_s.reshape(db, ts, d),
            jnp.stack(lat_p), jnp.stack(kr_p), jnp.stack(st_p),
            jnp.stack(lat_s), jnp.stack(kr_s), jnp.stack(st_s))
```

```python
import functools

import jax
import jax.numpy as jnp
from jax import lax
from jax.experimental import pallas as pl
from jax.experimental.pallas import tpu as pltpu

F32 = jnp.float32
BF16 = jnp.bfloat16

NORM_EPS = 1e-6
ROPE_THETA = 10000.0
CHUNK = 64
TOP_K = 4
SWIGLU_LIMIT = 7.0
SWIGLU_ALPHA = 1.702

LANES = 128
VMEM_LIMIT_BYTES = 56 * 1024 * 1024

RET_CHUNK = 256
MOE_ROWS = 128
MOE_GROUP = 10
MOE_CHAIN = 8
PROJ_ROWS = 1152
MOE_FTILE = 256
COMBINE_ROWS = 128
MIXER_SUBTILES = 2
ABSORBED_HEADS = 4


def _params(sem, vmem=None):
    return pltpu.CompilerParams(dimension_semantics=sem, vmem_limit_bytes=vmem)


def _rms(x, g):
    return x * lax.rsqrt(jnp.mean(x * x, axis=-1, keepdims=True) + NORM_EPS) * g


def _sigmoid(x):
    return 1.0 / (1.0 + jnp.exp(-x))


def _group_specs(rows_p, rows_s, tm, d):
    npb, nsb = rows_p // tm, rows_s // tm
    prompt = pl.BlockSpec((tm, d), lambda i: (jnp.minimum(i, npb - 1), 0))
    sample = pl.BlockSpec((tm, d), lambda i: (jnp.clip(i - npb, 0, nsb - 1), 0))
    return npb, prompt, sample


def _norm_cast_kernel(xp_ref, xs_ref, g_ref, o_ref, *, prompt_blocks):
    x = jnp.where(pl.program_id(0) < prompt_blocks, xp_ref[...], xs_ref[...])
    o_ref[...] = _rms(x, g_ref[...]).astype(o_ref.dtype)


def norm_cast(xp, xs, g, tm=512):
    d = xp.shape[1]
    m = xp.shape[0] + xs.shape[0]
    npb, prompt, sample = _group_specs(xp.shape[0], xs.shape[0], tm, d)
    return pl.pallas_call(
        functools.partial(_norm_cast_kernel, prompt_blocks=npb),
        out_shape=jax.ShapeDtypeStruct((m, d), BF16),
        grid=(m // tm,),
        in_specs=[prompt, sample, pl.BlockSpec((1, d), lambda i: (0, 0))],
        out_specs=pl.BlockSpec((tm, d), lambda i: (i, 0)),
        compiler_params=_params(("parallel",), VMEM_LIMIT_BYTES),
        name="norm_cast",
    )(xp, xs, g)


def _rope_hi(x, c, sa, sb):
    return x * c + pltpu.roll(x, 32, 1) * sa + pltpu.roll(x, 96, 1) * sb


def _proj_small_kernel(x_ref, w_ref, gq_ref, gkv_ref, c_ref, sa_ref, sb_ref,
                       cqn_ref, ckv_ref, ckvb_ref, kr_ref, krp_ref, *, ql, kl, rope):
    acc = jnp.dot(x_ref[...], w_ref[...], preferred_element_type=F32)
    cqn_ref[...] = _rms(acc[:, :ql], gq_ref[...]).astype(cqn_ref.dtype)
    ckv = _rms(acc[:, ql:ql + kl], gkv_ref[...])
    ckv_ref[...] = ckv
    ckvb_ref[...] = ckv.astype(ckvb_ref.dtype)
    kr = _rope_hi(acc[:, ql + kl:ql + kl + LANES], c_ref[...], sa_ref[...], sb_ref[...])
    kr_ref[...] = kr[:, :rope]
    krp_ref[...] = kr.astype(krp_ref.dtype)


def proj_small(xn, wa, gq, gkv, c, sa, sb, ql, kl, rope, tm=256):
    m, d = xn.shape
    n = wa.shape[1]
    row = lambda w: pl.BlockSpec((tm, w), lambda i: (i, 0))
    full = lambda a: pl.BlockSpec(a.shape, lambda i: (0, 0))
    return pl.pallas_call(
        functools.partial(_proj_small_kernel, ql=ql, kl=kl, rope=rope),
        out_shape=(jax.ShapeDtypeStruct((m, ql), BF16),
                   jax.ShapeDtypeStruct((m, kl), F32),
                   jax.ShapeDtypeStruct((m, kl), BF16),
                   jax.ShapeDtypeStruct((m, rope), F32),
                   jax.ShapeDtypeStruct((m, LANES), BF16)),
        grid=(m // tm,),
        in_specs=[row(d), full(wa), full(gq), full(gkv), row(LANES), row(LANES), row(LANES)],
        out_specs=(row(ql), row(kl), row(kl), row(rope), row(LANES)),
        compiler_params=_params(("parallel",), VMEM_LIMIT_BYTES),
        name="proj_small",
    )(xn, wa, gq, gkv, c, sa, sb)


def _proj_big_kernel(x_ref, w_ref, cos_ref, sin_ref, o_ref, *, n_rope, dk, k_scale):
    j = pl.program_id(0)
    acc = jnp.dot(x_ref[...], w_ref[...], preferred_element_type=F32)

    @pl.when(j >= n_rope)
    def _():
        o_ref[...] = acc.astype(o_ref.dtype)

    @pl.when(j < n_rope)
    def _():
        scale = jnp.where(j >= n_rope // 2, k_scale, 1.0).astype(F32)
        c = cos_ref[...] * scale
        s = sin_ref[...] * scale
        half = dk // 2
        for h in range(acc.shape[1] // dk):
            x1 = acc[:, h * dk:h * dk + half]
            x2 = acc[:, h * dk + half:(h + 1) * dk]
            o_ref[:, h * dk:h * dk + half] = (x1 * c - x2 * s).astype(o_ref.dtype)
            o_ref[:, h * dk + half:(h + 1) * dk] = (x2 * c + x1 * s).astype(o_ref.dtype)


def proj_big(xn, wb, cos_r, sin_r, rope_cols, dk, k_scale, tm=None, tn=1024):
    m, d = xn.shape
    n = wb.shape[1]
    if tm is None:
        tm = max(t for t in range(16, PROJ_ROWS + 1, 16) if m % t == 0)
    return pl.pallas_call(
        functools.partial(_proj_big_kernel, n_rope=rope_cols // tn, dk=dk, k_scale=k_scale),
        out_shape=jax.ShapeDtypeStruct((m, n), BF16),
        grid=(n // tn, m // tm),
        in_specs=[pl.BlockSpec((tm, d), lambda j, i: (i, 0)),
                  pl.BlockSpec((d, tn), lambda j, i: (0, j)),
                  pl.BlockSpec((tm, dk // 2), lambda j, i: (i, 0)),
                  pl.BlockSpec((tm, dk // 2), lambda j, i: (i, 0))],
        out_specs=pl.BlockSpec((tm, tn), lambda j, i: (i, j)),
        compiler_params=_params(("parallel", "parallel"), VMEM_LIMIT_BYTES),
        name="proj_big",
    )(xn, wb, cos_r, sin_r)


def _q_proj_kernel(x_ref, w_ref, c_ref, sa_ref, sb_ref, o_ref, *, heads, scale):
    acc = jnp.dot(x_ref[...], w_ref[...], preferred_element_type=F32)
    c = c_ref[...] * scale
    sa = sa_ref[...] * scale
    sb = sb_ref[...] * scale
    for h in range(heads):
        lo = 2 * h * LANES
        o_ref[:, lo:lo + LANES] = (acc[:, lo:lo + LANES] * scale).astype(o_ref.dtype)
        o_ref[:, lo + LANES:lo + 2 * LANES] = _rope_hi(
            acc[:, lo + LANES:lo + 2 * LANES], c, sa, sb).astype(o_ref.dtype)


def q_proj(cqn, wq, c, sa, sb, heads, scale, tm=256):
    m, k = cqn.shape
    n = wq.shape[1]
    row = lambda w: pl.BlockSpec((tm, w), lambda i: (i, 0))
    return pl.pallas_call(
        functools.partial(_q_proj_kernel, heads=heads, scale=scale),
        out_shape=jax.ShapeDtypeStruct((m, n), BF16),
        grid=(m // tm,),
        in_specs=[row(k), pl.BlockSpec((k, n), lambda i: (0, 0)), row(LANES), row(LANES), row(LANES)],
        out_specs=row(n),
        compiler_params=_params(("parallel",), VMEM_LIMIT_BYTES),
        name="q_proj",
    )(cqn, wq, c, sa, sb)


def _kv_expand_kernel(c_ref, kr_ref, wk_ref, wvt_ref, k_ref, vt_ref, *, heads):
    c = c_ref[...]
    kn = jnp.dot(c, wk_ref[...], preferred_element_type=F32)
    vt_ref[...] = lax.dot_general(wvt_ref[...], c, (((1,), (1,)), ((), ())),
                                  preferred_element_type=F32).astype(vt_ref.dtype)
    kr = kr_ref[...]
    for h in range(heads):
        k_ref[:, 2 * h * LANES:(2 * h + 1) * LANES] = kn[:, h * LANES:(h + 1) * LANES].astype(k_ref.dtype)
        k_ref[:, (2 * h + 1) * LANES:(2 * h + 2) * LANES] = kr


def kv_expand(c_bf, kr_pad, wk, wvt, rows, heads, tm=256):
    kl = c_bf.shape[1]
    dv = heads * LANES
    row = lambda w: pl.BlockSpec((tm, w), lambda i: (i, 0))
    full = lambda a: pl.BlockSpec(a.shape, lambda i: (0, 0))
    return pl.pallas_call(
        functools.partial(_kv_expand_kernel, heads=heads),
        out_shape=(jax.ShapeDtypeStruct((rows, 2 * heads * LANES), BF16),
                   jax.ShapeDtypeStruct((dv, rows), BF16)),
        grid=(rows // tm,),
        in_specs=[row(kl), row(LANES), full(wk), full(wvt)],
        out_specs=(row(2 * heads * LANES), pl.BlockSpec((dv, tm), lambda i: (0, i))),
        compiler_params=_params(("parallel",), VMEM_LIMIT_BYTES),
        name="kv_expand",
    )(c_bf, kr_pad, wk, wvt)


def _attn_absorbed_kernel(q_ref, c_ref, kr_ref, wukt_ref, wuv_ref, o_ref, *, heads):
    tq = q_ref.shape[0]
    c = c_ref[...]
    kr = kr_ref[...]
    nt = (((1,), (1,)), ((), ()))
    for g in range(heads // ABSORBED_HEADS):
        hs = range(g * ABSORBED_HEADS, (g + 1) * ABSORBED_HEADS)
        q_lat = jnp.concatenate(
            [jnp.dot(q_ref[:, 2 * h * LANES:(2 * h + 1) * LANES], wukt_ref[h],
                     preferred_element_type=F32).astype(BF16) for h in hs], axis=0)
        q_rope = jnp.concatenate([q_ref[:, (2 * h + 1) * LANES:(2 * h + 2) * LANES] for h in hs], axis=0)
        s = (lax.dot_general(q_lat, c, nt, preferred_element_type=F32)
             + lax.dot_general(q_rope, kr, nt, preferred_element_type=F32))
        p = jnp.exp2(s - jnp.max(s, axis=-1, keepdims=True))
        l = jnp.sum(p, axis=-1, keepdims=True)
        o_lat = (jnp.dot(p.astype(BF16), c, preferred_element_type=F32) / l).astype(BF16)
        for i, h in enumerate(hs):
            o_ref[:, h * LANES:(h + 1) * LANES] = jnp.dot(
                o_lat[i * tq:(i + 1) * tq], wuv_ref[h], preferred_element_type=F32).astype(o_ref.dtype)


def attention_absorbed(q, c_all, kr_all, wukt, wuv, heads, *, batch, tq, q_row0, tk):
    assert heads % ABSORBED_HEADS == 0 and q_row0 % tq == 0
    kl = c_all.shape[1]
    qb0 = q_row0 // tq
    full = lambda a: pl.BlockSpec(a.shape, lambda b: (0, 0, 0))
    return pl.pallas_call(
        functools.partial(_attn_absorbed_kernel, heads=heads),
        out_shape=jax.ShapeDtypeStruct((batch * tq, heads * LANES), BF16),
        grid=(batch,),
        in_specs=[pl.BlockSpec((tq, 2 * heads * LANES), lambda b: (qb0 + b, 0)),
                  pl.BlockSpec((tk, kl), lambda b: (b, 0)),
                  pl.BlockSpec((tk, LANES), lambda b: (b, 0)),
                  full(wukt), full(wuv)],
        out_specs=pl.BlockSpec((tq, heads * LANES), lambda b: (b, 0)),
        compiler_params=_params(("parallel",), VMEM_LIMIT_BYTES),
        name="attention_absorbed",
    )(q, c_all, kr_all, wukt, wuv)


def _attn_t_kernel(q_ref, k_ref, vt_ref, bias_ref, o_ref, sa_ref, sb_ref, *, tq, tk):
    i = pl.program_id(1)
    dq, dv = 2 * LANES, LANES
    n_heads = q_ref.shape[1] // dq

    def scores(j, s_ref):
        off = pl.multiple_of(j * tk, tk)
        for h in range(n_heads):
            q = q_ref[:, h * dq:(h + 1) * dq]
            k = k_ref[pl.ds(off, tk), h * dq:(h + 1) * dq]
            s_ref[h] = lax.dot_general(k, q, (((1,), (1,)), ((), ())), preferred_element_type=F32)

    def softmax_pv(j, s_ref, carries, masked):
        off = pl.multiple_of(j * tk, tk)
        out = []
        for h in range(n_heads):
            m, l, acc = carries[h]
            s = s_ref[h]
            if masked:
                s = jnp.minimum(s, bias_ref[...])
            m_new = jnp.maximum(m, jnp.max(s, axis=0, keepdims=True))
            alpha = jnp.exp2(m - m_new)
            p = jnp.exp2(s - m_new)
            l = alpha * l + jnp.sum(p, axis=0, keepdims=True)
            vt = vt_ref[h * dv:(h + 1) * dv, pl.ds(off, tk)]
            acc = alpha * acc + jnp.dot(vt, p.astype(BF16), preferred_element_type=F32)
            out.append((m_new, l, acc))
        return tuple(out)

    def finish(carries):
        for h, (_, l, acc) in enumerate(carries):
            o_ref[:, h * dv:(h + 1) * dv] = (acc / l).T.astype(o_ref.dtype)

    scores(0, sa_ref)

    def pair(t, carries):
        scores(2 * t + 1, sb_ref)
        carries = softmax_pv(2 * t, sa_ref, carries, False)
        scores(2 * t + 2, sa_ref)
        return softmax_pv(2 * t + 1, sb_ref, carries, False)

    init = (jnp.full((1, tq), -jnp.inf, F32), jnp.zeros((1, tq), F32), jnp.zeros((dv, tq), F32))
    carries = lax.fori_loop(0, i // 2, pair, (init,) * n_heads)

    @pl.when(i % 2 == 0)
    def _():
        finish(softmax_pv(i, sa_ref, carries, True))

    @pl.when(i % 2 == 1)
    def _():
        scores(i, sb_ref)
        finish(softmax_pv(i, sb_ref, softmax_pv(i - 1, sa_ref, carries, False), True))


def attention_causal(q, k, vt, heads, *, t, tq, hp=2):
    assert tq % CHUNK == 0 and heads % hp == 0
    dq = 2 * LANES
    chunk_of = jnp.arange(tq, dtype=jnp.int32) // CHUNK
    ceiling = jnp.where(chunk_of[:, None] <= chunk_of[None, :],
                        jnp.finfo(F32).max, -1e30).astype(F32)
    return pl.pallas_call(
        functools.partial(_attn_t_kernel, tq=tq, tk=tq),
        out_shape=jax.ShapeDtypeStruct((t, heads * LANES), BF16),
        grid=(heads // hp, t // tq),
        in_specs=[pl.BlockSpec((tq, hp * dq), lambda h, i: (i, h)),
                  pl.BlockSpec((t, hp * dq), lambda h, i: (0, h)),
                  pl.BlockSpec((hp * LANES, t), lambda h, i: (h, 0)),
                  pl.BlockSpec((tq, tq), lambda h, i: (0, 0))],
        out_specs=pl.BlockSpec((tq, hp * LANES), lambda h, i: (i, h)),
        scratch_shapes=[pltpu.VMEM((hp, tq, tq), F32), pltpu.VMEM((hp, tq, tq), F32)],
        compiler_params=_params(("parallel", "arbitrary"), VMEM_LIMIT_BYTES),
        name="attention_causal",
    )(q, k, vt, ceiling)


def _retention_kernel(q_ref, k_ref, v_ref, s0_ref, d_ref, xi_ref, zeta_ref, gc_ref,
                      o_ref, sout_ref, s_scr, *, n_chunks):
    c = pl.program_id(2)
    dk = s_scr.shape[-1]

    @pl.when(c == 0)
    def _():
        s_scr[...] = s0_ref[...]

    for h in range(s_scr.shape[0]):
        cols = slice(h * dk, (h + 1) * dk)
        q = q_ref[:, cols]
        k = k_ref[:, cols]
        v = v_ref[:, cols]
        s = s_scr[h]
        a = lax.dot_general(q, k, (((1,), (1,)), ((), ())), preferred_element_type=F32) * d_ref[h]
        o = jnp.dot(a.astype(BF16), v, preferred_element_type=F32)
        o = o + jnp.dot(q, s.astype(BF16), preferred_element_type=F32) * xi_ref[h]
        kz = (k.astype(F32) * zeta_ref[h]).astype(BF16)
        s_new = s * gc_ref[h] + lax.dot_general(kz, v, (((0,), (0,)), ((), ())),
                                                preferred_element_type=F32)
        s_scr[h] = s_new
        o_ref[:, cols] = (o * lax.rsqrt(jnp.mean(o * o, axis=-1, keepdims=True) + NORM_EPS)).astype(o_ref.dtype)

    @pl.when(c == n_chunks - 1)
    def _():
        sout_ref[...] = s_scr[...]


def retention(big, s0, tabs, heads, dk, *, batch, t_total, row0, chunk, hp=4):
    d_tab, xi_tab, zeta_tab, gc_tab = tabs
    assert heads % hp == 0
    nc = t_total // chunk
    rb0 = row0 // chunk
    hg = heads // hp
    rowblk = lambda g0: pl.BlockSpec((chunk, hp * dk), lambda b, h, c: (rb0 + b * nc + c, g0 + h))
    per_head = lambda a: pl.BlockSpec((hp,) + a.shape[1:], lambda b, h, c: (h, 0, 0))
    st_spec = pl.BlockSpec((None, hp, dk, dk), lambda b, h, c: (b, h, 0, 0))
    return pl.pallas_call(
        functools.partial(_retention_kernel, n_chunks=nc),
        out_shape=(jax.ShapeDtypeStruct((batch * t_total, heads * dk), BF16),
                   jax.ShapeDtypeStruct((batch, heads, dk, dk), F32)),
        grid=(batch, hg, nc),
        in_specs=[rowblk(0), rowblk(hg), rowblk(2 * hg), st_spec,
                  per_head(d_tab), per_head(xi_tab), per_head(zeta_tab), per_head(gc_tab)],
        out_specs=(pl.BlockSpec((chunk, hp * dk), lambda b, h, c: (b * nc + c, h)), st_spec),
        scratch_shapes=[pltpu.VMEM((hp, dk, dk), F32)],
        compiler_params=_params(("parallel", "parallel", "arbitrary"), VMEM_LIMIT_BYTES),
        name="retention",
    )(big, big, big, s0, d_tab, xi_tab, zeta_tab, gc_tab)


def retention_tables(length, heads, dk):
    lg = jnp.log1p(-jnp.exp2(-5.0 - jnp.arange(heads, dtype=F32)))
    idx = jnp.arange(length, dtype=F32)
    diff = idx[:, None] - idx[None, :]
    decay = jnp.where(diff >= 0, jnp.exp(jnp.maximum(diff, 0.0)[None] * lg[:, None, None]), 0.0)
    xi = jnp.exp((idx + 1.0)[None, :] * lg[:, None])
    zeta = jnp.exp((length - 1.0 - idx)[None, :] * lg[:, None])
    gc = jnp.exp(length * lg)
    bc = lambda a: jnp.broadcast_to(a[:, :, None], (heads, a.shape[1], dk))
    return decay, bc(xi), bc(zeta), jnp.broadcast_to(gc[:, None, None], (heads, 1, dk))


def _mixer_out_kernel(oap_ref, oas_ref, orp_ref, ors_ref, rg_ref, ga_ref, gb_ref, xp_ref, xs_ref, wo_ref,
                      g_ref, wrh_ref, wrl_ref, br_ref, h_ref, hn_ref, lg_ref, *, prompt_blocks):
    is_prompt = pl.program_id(0) < prompt_blocks
    tm = h_ref.shape[0]
    sub = tm // MIXER_SUBTILES
    for s in range(MIXER_SUBTILES):
        r = slice(s * sub, (s + 1) * sub)
        oa = jnp.where(is_prompt, oap_ref[r, :], oas_ref[r, :]).astype(F32)
        orn = jnp.where(is_prompt, orp_ref[r, :], ors_ref[r, :]).astype(F32)
        x = jnp.where(is_prompt, xp_ref[r, :], xs_ref[r, :])
        rg = rg_ref[r, :].astype(F32)
        yb = rg * _sigmoid(rg) * orn
        merged = _sigmoid(ga_ref[r, :].astype(F32)) * oa + _sigmoid(gb_ref[r, :].astype(F32)) * yb
        h = x + jnp.dot(merged.astype(BF16), wo_ref[...], preferred_element_type=F32)
        h_ref[r, :] = h
        hn = _rms(h, g_ref[...])
        hn_ref[r, :] = hn
        hn_hi = hn.astype(BF16)
        hn_lo = (hn - hn_hi.astype(F32)).astype(BF16)
        lg_ref[r, :] = (jnp.dot(hn_hi, wrh_ref[...], preferred_element_type=F32)
                        + jnp.dot(hn_lo, wrh_ref[...], preferred_element_type=F32)
                        + jnp.dot(hn_hi, wrl_ref[...], preferred_element_type=F32) + br_ref[...])


def mixer_out(oa_p, oa_s, or_p, or_s, big, gate_blk0, xp, xs, wo, g, wr, br, tm=256):
    d = xp.shape[1]
    m = xp.shape[0] + xs.shape[0]
    e = wr.shape[1]
    wr_hi = wr.astype(BF16)
    wr_lo = (wr - wr_hi.astype(F32)).astype(BF16)
    npb, prompt, sample = _group_specs(xp.shape[0], xs.shape[0], tm, d)
    row = lambda col: pl.BlockSpec((tm, d), lambda i: (i, col))
    full = lambda a: pl.BlockSpec(a.shape, lambda i: (0, 0))
    return pl.pallas_call(
        functools.partial(_mixer_out_kernel, prompt_blocks=npb),
        out_shape=(jax.ShapeDtypeStruct((m, d), F32),
                   jax.ShapeDtypeStruct((m, d), F32),
                   jax.ShapeDtypeStruct((m, e), F32)),
        grid=(m // tm,),
        in_specs=[prompt, sample, prompt, sample, row(gate_blk0), row(gate_blk0 + 1), row(gate_blk0 + 2),
                  prompt, sample, full(wo), full(g), full(wr_hi), full(wr_lo), full(br)],
        out_specs=(row(0), row(0), pl.BlockSpec((tm, e), lambda i: (i, 0))),
        compiler_params=_params(("parallel",), VMEM_LIMIT_BYTES),
        name="mixer_out",
    )(oa_p, oa_s, or_p, or_s, big, big, big, xp, xs, wo, g, wr_hi, wr_lo, br)


def _expert_kernel(tok_ref, ie_ref, ib_ref, inb_ref, nv_ref, hn_ref, wg_ref, wu_ref, bg_ref, bu_ref, wd_ref,
                   bd_ref, ye_ref, x32, y32, wgb, wub, wdb, sem_x, sem_out, *, rows, group, n_f, n_items,
                   n_blocks):
    k = pl.program_id(0)
    f = pl.program_id(1)
    nb = inb_ref[k]
    blk0 = ib_ref[k]
    n_valid = nv_ref[0]
    valid = k < n_valid
    has_next = k + 1 < n_valid
    last_f = f == n_f - 1
    cur = k % 2
    rs = rows * group
    d = y32.shape[1]
    per_block = rs // (n_f * group)

    def gather_copy(slot, buf, row):
        return pltpu.make_async_copy(hn_ref.at[pl.ds(tok_ref[slot], 1), :],
                                     x32.at[buf, pl.ds(row, 1), :], sem_x.at[buf])

    def out_copy(b):
        return pltpu.make_async_copy(y32.at[pl.ds(b * rows, rows), :],
                                     ye_ref.at[pl.ds((blk0 + b) * rows, rows), :], sem_out)

    @pl.when((k == 0) & (f == 0))
    def _():
        def body(r, _):
            gather_copy(blk0 * rows + r, 0, r).start()
            return 0
        lax.fori_loop(0, rs, body, 0, unroll=8)

    @pl.when(valid & (f == 0))
    def _():
        def drain(r, _):
            gather_copy(0, cur, 0).wait()
            return 0
        lax.fori_loop(0, rs, drain, 0, unroll=8)

        def init(b, _):
            y32[pl.ds(pl.multiple_of(b * rows, rows), rows), :] = jnp.broadcast_to(bd_ref[...], (rows, d))
            return 0
        lax.fori_loop(0, nb, init, 0)

    def cast_weights():
        wg, wu, wd = wg_ref[...].astype(BF16), wu_ref[...].astype(BF16), wd_ref[...].astype(BF16)
        wgb[...] = wg
        wub[...] = wu
        wdb[...] = wd
        return wg, wu, wd

    @pl.when(valid & (nb < MOE_CHAIN))
    def _():
        cast_weights()

    def prefetch(b, next_slot0):
        row0 = (f * group + b) * per_block
        for r in range(per_block):
            gather_copy(next_slot0 + row0 + r, 1 - cur, row0 + r).start(priority=1)

    def compute_blocks(b0, n, next_slot0, cast=False):
        r = pl.ds(pl.multiple_of(b0 * rows, rows), n * rows)
        x = x32[cur, r, :].astype(BF16)
        wg, wu, wd = cast_weights() if cast else (wgb[...], wub[...], wdb[...])
        gate = jnp.minimum(jnp.dot(x, wg, preferred_element_type=F32) + bg_ref[...], SWIGLU_LIMIT)
        up = jnp.clip(jnp.dot(x, wu, preferred_element_type=F32) + bu_ref[...],
                      -SWIGLU_LIMIT, SWIGLU_LIMIT)
        h = (up + 1.0) * (gate * _sigmoid(gate * SWIGLU_ALPHA))
        y32[r, :] += jnp.dot(h.astype(BF16), wd, preferred_element_type=F32)
        if next_slot0 is not None:
            for i in range(n):
                prefetch(b0 + i, next_slot0)

        @pl.when(last_f)
        def _():
            for i in range(n):
                out_copy(b0 + i).start()

    def run(next_slot0):
        @pl.when(nb >= MOE_CHAIN)
        def _():
            compute_blocks(0, MOE_CHAIN, next_slot0, cast=True)

        def chain(p, _):
            compute_blocks(MOE_CHAIN * p, MOE_CHAIN, next_slot0)
            return 0
        lax.fori_loop(1, nb // MOE_CHAIN, chain, 0)
        n = MOE_CHAIN // 2
        while n >= 1:
            @pl.when(nb & n != 0)
            def _(n=n):
                compute_blocks(nb - (nb & (2 * n - 1)), n, next_slot0)
            n //= 2

    @pl.when(has_next)
    def _():
        next_slot0 = ib_ref[k + 1] * rows
        run(next_slot0)

        def rest(b, _):
            prefetch(b, next_slot0)
            return 0
        lax.fori_loop(nb, group, rest, 0)

    @pl.when(valid & jnp.logical_not(has_next))
    def _():
        run(None)

    @pl.when(valid & last_f)
    def _():
        def body(b, _):
            out_copy(b).wait()
            return 0
        lax.fori_loop(0, nb, body, 0)

    @pl.when((k == n_items - 1) & last_f)
    def _():
        used = nv_ref[1]
        y32[0:rows, :] = jnp.zeros((rows, d), F32)

        def fill(b):
            return pltpu.make_async_copy(y32.at[pl.ds(0, rows), :], ye_ref.at[pl.ds(b * rows, rows), :], sem_out)

        def start(b, _):
            fill(b).start()
            return 0

        def wait(b, _):
            fill(b).wait()
            return 0
        lax.fori_loop(used, n_blocks, start, 0)
        lax.fori_loop(used, n_blocks, wait, 0)


def expert_mlp(slot_tok, items, hn, w_gate_up, b_gate_up, w_down, b_down, n_items, n_blocks):
    item_e, item_b0, item_nb, counts = items
    n_exp, d, two_f = w_gate_up.shape
    d_ff = two_f // 2
    tf = MOE_FTILE
    n_f = d_ff // tf
    rows, group = MOE_ROWS, MOE_GROUP
    rs = rows * group
    assert rs % (n_f * group) == 0

    def fi(k, f, nv):
        return jnp.where(k < nv[0], f, n_f - 1)

    in_specs = [
        pl.BlockSpec(memory_space=pl.ANY),
        pl.BlockSpec((None, d, tf), lambda k, f, tok, ie, ib, inb, nv: (ie[k], 0, fi(k, f, nv))),
        pl.BlockSpec((None, d, tf), lambda k, f, tok, ie, ib, inb, nv: (ie[k], 0, n_f + fi(k, f, nv))),
        pl.BlockSpec((None, 1, tf), lambda k, f, tok, ie, ib, inb, nv: (ie[k], 0, fi(k, f, nv))),
        pl.BlockSpec((None, 1, tf), lambda k, f, tok, ie, ib, inb, nv: (ie[k], 0, n_f + fi(k, f, nv))),
        pl.BlockSpec((None, tf, d), lambda k, f, tok, ie, ib, inb, nv: (ie[k], fi(k, f, nv), 0)),
        pl.BlockSpec((None, 1, d), lambda k, f, tok, ie, ib, inb, nv: (ie[k], 0, 0)),
    ]
    return pl.pallas_call(
        functools.partial(_expert_kernel, rows=rows, group=group, n_f=n_f, n_items=n_items, n_blocks=n_blocks),
        out_shape=jax.ShapeDtypeStruct((n_blocks * rows, d), F32),
        grid_spec=pltpu.PrefetchScalarGridSpec(
            num_scalar_prefetch=5, grid=(n_items, n_f),
            in_specs=in_specs,
            out_specs=pl.BlockSpec(memory_space=pl.ANY),
            scratch_shapes=[pltpu.VMEM((2, rs, d), F32), pltpu.VMEM((rs, d), F32),
                            pltpu.VMEM((d, tf), BF16), pltpu.VMEM((d, tf), BF16), pltpu.VMEM((tf, d), BF16),
                            pltpu.SemaphoreType.DMA((2,)), pltpu.SemaphoreType.DMA]),
        compiler_params=_params(("arbitrary", "arbitrary"), VMEM_LIMIT_BYTES),
        name="moe_experts",
    )(slot_tok, item_e, item_b0, item_nb, counts, hn, w_gate_up, w_gate_up,
      b_gate_up.reshape(n_exp, 1, two_f), b_gate_up.reshape(n_exp, 1, two_f),
      w_down, b_down.reshape(n_exp, 1, d))


def _combine_kernel(pos_ref, ye_ref, w_ref, h_ref, g_ref, o_ref, buf, sems, *, tc, blk0, n_steps, final_norm):
    i = pl.program_id(0)
    n_rows = TOP_K * tc

    def copy(r, src_row, par):
        return pltpu.make_async_copy(ye_ref.at[pl.ds(src_row, 1), :], buf.at[par, pl.ds(r, 1), :],
                                     sems.at[par])

    def start(r, step, par, j):
        copy(r, pos_ref[(blk0 + step) * n_rows + r], par).start(priority=j % 2)

    def drain(par):
        def body(r, _):
            copy(0, 0, par).wait()
            return 0
        lax.fori_loop(0, n_rows, body, 0, unroll=8)

    @pl.when(i == 0)
    def _():
        def body(g, _):
            for j in range(8):
                start(g * 8 + j, 0, 0, j)
            return 0
        lax.fori_loop(0, n_rows // 8, body, 0)

    par = i % 2
    drain(par)

    nxt = jnp.minimum(i + 1, n_steps - 1)
    for r in range(n_rows):
        start(r, nxt, 1 - par, r)
    w = w_ref[...]
    y = h_ref[...]
    for k in range(TOP_K):
        y = y + w[:, k:k + 1] * buf[par, k * tc:(k + 1) * tc, :]
    o_ref[...] = _rms(y, g_ref[...]) if final_norm else y

    @pl.when(i == n_steps - 1)
    def _():
        drain(1 - par)


def combine(pos, ye, top_w, h, g, row0, rows, final_norm, tc=COMBINE_ROWS):
    d = h.shape[1]
    blk0 = row0 // tc
    n_steps = rows // tc
    return pl.pallas_call(
        functools.partial(_combine_kernel, tc=tc, blk0=blk0, n_steps=n_steps, final_norm=final_norm),
        out_shape=jax.ShapeDtypeStruct((rows, d), F32),
        grid_spec=pltpu.PrefetchScalarGridSpec(
            num_scalar_prefetch=1, grid=(n_steps,),
            in_specs=[pl.BlockSpec(memory_space=pl.ANY),
                      pl.BlockSpec((tc, TOP_K), lambda i, p: (blk0 + i, 0)),
                      pl.BlockSpec((tc, d), lambda i, p: (blk0 + i, 0)),
                      pl.BlockSpec((1, d), lambda i, p: (0, 0))],
            out_specs=pl.BlockSpec((tc, d), lambda i, p: (i, 0)),
            scratch_shapes=[pltpu.VMEM((2, TOP_K * tc, d), F32), pltpu.SemaphoreType.DMA((2,))]),
        compiler_params=_params(("arbitrary",), VMEM_LIMIT_BYTES),
        name="moe_combine",
    )(pos, ye, top_w, h, g)


def route(logits, tc):
    m, n_exp = logits.shape
    a = m * TOP_K
    rows, group = MOE_ROWS, MOE_GROUP
    top_logit, top_idx = lax.top_k(logits, TOP_K)
    top_w = jax.nn.softmax(top_logit, axis=-1)
    flat_e = top_idx.reshape(a)
    onehot = (flat_e[:, None] == jnp.arange(n_exp, dtype=jnp.int32)[None, :]).astype(jnp.int32)
    csum = jnp.cumsum(onehot, axis=0)
    counts = csum[-1]
    rank = jnp.take_along_axis(csum, flat_e[:, None], axis=1)[:, 0] - 1
    nb_e = (counts + rows - 1) // rows
    blk_start = jnp.cumsum(nb_e) - nb_e
    dest = blk_start[flat_e] * rows + rank
    n_blocks = a // rows + n_exp
    slot_tok = jnp.zeros(((n_blocks + group) * rows,), jnp.int32).at[dest].set(
        jnp.arange(a, dtype=jnp.int32) // TOP_K)
    ni_e = (nb_e + group - 1) // group
    item_end = jnp.cumsum(ni_e)
    n_valid = item_end[-1]
    n_items = (n_blocks + n_exp * (group - 1)) // group
    kk = jnp.arange(n_items, dtype=jnp.int32)
    item_e = jnp.minimum(jnp.sum(item_end[None, :] <= kk[:, None], axis=1), n_exp - 1).astype(jnp.int32)
    local = kk - (item_end - ni_e)[item_e]
    valid = kk < n_valid
    item_e = jnp.where(valid, item_e, item_e[jnp.maximum(n_valid - 1, 0)])
    item_b0 = jnp.where(valid, blk_start[item_e] + local * group, 0).astype(jnp.int32)
    item_nb = jnp.where(valid, jnp.clip(nb_e[item_e] - local * group, 0, group), 0).astype(jnp.int32)
    pos = dest.reshape(m // tc, tc, TOP_K).transpose(0, 2, 1).reshape(a)
    counts = jnp.stack([n_valid, jnp.sum(nb_e)]).astype(jnp.int32)
    items = (item_e, item_b0, item_nb, counts)
    return top_w, slot_tok, pos.astype(jnp.int32), items, n_items, n_blocks


def _rope_tables(pos, half):
    inv = jnp.power(ROPE_THETA, -jnp.arange(half, dtype=F32) / half)
    ang = pos.astype(F32)[:, None] * inv[None, :]
    return jnp.cos(ang), jnp.sin(ang)


def kernel(x_prompt, x_sample, cache_latent, cache_k_rope, state_retention, g_mix_norm, w_in, g_q_norm, w_uq, g_kv_norm, w_uk, w_uv, w_o, g_ffn_norm, w_router, b_router, w_gate_up, b_gate_up, w_down, b_down, g_final):
    bp, t, d = x_prompt.shape
    db, ts, _ = x_sample.shape
    depth = w_in.shape[0]
    past = cache_latent.shape[2]
    ql, ha, qk = w_uq.shape[1:]
    kl = w_uk.shape[1]
    rope = cache_k_rope.shape[-1]
    nope = qk - rope
    hr, dk = state_retention.shape[2:4]
    assert bp == 1 and nope == LANES and w_uv.shape[-1] == LANES and 2 * rope == LANES
    assert past % CHUNK == 0 and ts <= CHUNK
    mp, ms = bp * t, db * ts
    m = mp + ms
    tk_s = past + ts
    mla_scale = float(qk) ** -0.5 * 1.4426950408889634

    pos = jnp.concatenate([jnp.arange(t, dtype=jnp.int32),
                           jnp.tile(past + jnp.arange(ts, dtype=jnp.int32), db)])
    cos_a, sin_a = _rope_tables(pos, rope // 2)
    z = lambda w: jnp.zeros((m, w), F32)
    rope_c = jnp.concatenate([cos_a, cos_a, z(LANES - rope)], axis=1)
    rope_sa = jnp.concatenate([z(rope // 2), sin_a, z(LANES - rope)], axis=1)
    rope_sb = jnp.concatenate([-sin_a, z(LANES - rope // 2)], axis=1)
    cos_r, sin_r = _rope_tables(pos, dk // 2)
    tabs_p = retention_tables(RET_CHUNK, hr, dk)
    tabs_s = retention_tables(ts, hr, dk)

    xp, xs = x_prompt.reshape(mp, d), x_sample.reshape(ms, d)
    lat_p, kr_p, st_p, lat_s, kr_s, st_s = [], [], [], [], [], []
    small = ql + kl + rope
    for l in range(depth):
        wa = jnp.pad(w_in[l][:, :small], ((0, 0), (0, LANES - rope))).astype(BF16)
        wb = w_in[l][:, small:].astype(BF16)
        wq = jnp.pad(w_uq[l], ((0, 0), (0, 0), (0, 2 * LANES - qk))).reshape(ql, ha * 2 * LANES).astype(BF16)
        wk = w_uk[l].reshape(kl, ha * nope).astype(BF16)
        wv = w_uv[l].reshape(kl, ha * LANES).astype(BF16)
        wo = w_o[l].astype(BF16)

        xn = norm_cast(xp, xs, g_mix_norm[l].reshape(1, d))
        cqn, ckv, ckv_b, kr, kr_pad = proj_small(xn, wa, g_q_norm[l].reshape(1, ql), g_kv_norm[l].reshape(1, kl),
                                                 rope_c, rope_sa, rope_sb, ql, kl, rope)
        big = proj_big(xn, wb, cos_r, sin_r, 2 * hr * dk, dk, float(dk) ** -0.5)
        qcat = q_proj(cqn, wq, rope_c, rope_sa, rope_sb, ha, mla_scale)

        k_p, vt_p = kv_expand(ckv_b, kr_pad, wk, wv.T, mp, ha)
        oa_p = attention_causal(qcat, k_p, vt_p, ha, t=mp, tq=512)
        c_all = jnp.concatenate([cache_latent[l], ckv[mp:].reshape(db, ts, kl)], axis=1)
        kr_all = jnp.concatenate([cache_k_rope[l], kr[mp:].reshape(db, ts, rope)], axis=1)
        kr_all = jnp.pad(kr_all, ((0, 0), (0, 0), (0, LANES - rope)))
        oa_s = attention_absorbed(qcat, c_all.reshape(db * tk_s, kl).astype(BF16),
                                  kr_all.reshape(db * tk_s, LANES).astype(BF16),
                                  w_uk[l].transpose(1, 2, 0).astype(BF16), w_uv[l].transpose(1, 0, 2).astype(BF16),
                                  ha, batch=db, tq=ts, q_row0=mp, tk=tk_s)

        or_p, s_p = retention(big, jnp.zeros((bp, hr, dk, dk), F32), tabs_p, hr, dk,
                              batch=bp, t_total=t, row0=0, chunk=RET_CHUNK)
        or_s, s_s = retention(big, state_retention[l].astype(F32), tabs_s, hr, dk,
                              batch=db, t_total=ts, row0=mp, chunk=ts)

        h1, hn, logits = mixer_out(oa_p, oa_s, or_p, or_s, big, 3, xp, xs, wo, g_ffn_norm[l].reshape(1, d),
                                   w_router[l], b_router[l].reshape(1, -1))

        top_w, slot_tok, pos_flat, items, n_items, n_blocks = route(logits, COMBINE_ROWS)
        ye = expert_mlp(slot_tok, items, hn, w_gate_up[l], b_gate_up[l], w_down[l], b_down[l], n_items, n_blocks)
        last = l == depth - 1
        g_out = g_final.reshape(1, d)
        y_p = combine(pos_flat, ye, top_w, h1, g_out, 0, mp, last)
        y_s = combine(pos_flat, ye, top_w, h1, g_out, mp, ms, last)
        xp, xs = y_p, y_s

        lat_p.append(ckv[:mp].reshape(bp, t, kl))
        kr_p.append(kr[:mp].reshape(bp, t, rope))
        st_p.append(s_p)
        lat_s.append(ckv[mp:].reshape(db, ts, kl))
        kr_s.append(kr[mp:].reshape(db, ts, rope))
        st_s.append(s_s)

    return (y_p.reshape(bp, t, d), y_s.reshape(db, ts, d),
            jnp.stack(lat_p), jnp.stack(kr_p), jnp.stack(st_p),
            jnp.stack(lat_s), jnp.stack(kr_s), jnp.stack(st_s))
```

```python
import functools

import jax
import jax.numpy as jnp
from jax import lax
from jax.experimental import pallas as pl
from jax.experimental.pallas import tpu as pltpu

F32 = jnp.float32
BF16 = jnp.bfloat16

NORM_EPS = 1e-6
ROPE_THETA = 10000.0
CHUNK = 64
TOP_K = 4
SWIGLU_LIMIT = 7.0
SWIGLU_ALPHA = 1.702
LOG2_E = 1.4426950408889634

LANES = 128
VMEM_LIMIT_BYTES = 56 * 1024 * 1024

RET_CHUNK = 256
MOE_ROWS = 128
MOE_GROUP = 10
MOE_CHAIN = 8
PROJ_ROWS = 1152
MOE_FTILE = 256
COMBINE_ROWS = 128
MIXER_SUBTILES = 2
ABSORBED_HEADS = 4


def _params(sem, vmem=None):
    return pltpu.CompilerParams(dimension_semantics=sem, vmem_limit_bytes=vmem)


def _rms(x, g):
    return x * lax.rsqrt(jnp.mean(x * x, axis=-1, keepdims=True) + NORM_EPS) * g


def _sigmoid(x):
    return 0.5 * jnp.tanh(0.5 * x) + 0.5


def _group_specs(rows_p, rows_s, tm, d):
    npb, nsb = rows_p // tm, rows_s // tm
    prompt = pl.BlockSpec((tm, d), lambda i: (jnp.minimum(i, npb - 1), 0))
    sample = pl.BlockSpec((tm, d), lambda i: (jnp.clip(i - npb, 0, nsb - 1), 0))
    return npb, prompt, sample


def _norm_cast_kernel(xp_ref, xs_ref, g_ref, o_ref, *, prompt_blocks):
    x = jnp.where(pl.program_id(0) < prompt_blocks, xp_ref[...], xs_ref[...])
    o_ref[...] = _rms(x, g_ref[...]).astype(o_ref.dtype)


def norm_cast(xp, xs, g, tm=512):
    d = xp.shape[1]
    m = xp.shape[0] + xs.shape[0]
    npb, prompt, sample = _group_specs(xp.shape[0], xs.shape[0], tm, d)
    return pl.pallas_call(
        functools.partial(_norm_cast_kernel, prompt_blocks=npb),
        out_shape=jax.ShapeDtypeStruct((m, d), BF16),
        grid=(m // tm,),
        in_specs=[prompt, sample, pl.BlockSpec((1, d), lambda i: (0, 0))],
        out_specs=pl.BlockSpec((tm, d), lambda i: (i, 0)),
        compiler_params=_params(("parallel",), VMEM_LIMIT_BYTES),
        name="norm_cast",
    )(xp, xs, g)


def _rope_hi(x, c, sa, sb):
    return x * c + pltpu.roll(x, 32, 1) * sa + pltpu.roll(x, 96, 1) * sb


def _proj_small_kernel(x_ref, w_ref, gq_ref, gkv_ref, c_ref, sa_ref, sb_ref,
                       cqn_ref, ckv_ref, ckvb_ref, kr_ref, krp_ref, *, ql, kl, rope):
    acc = jnp.dot(x_ref[...], w_ref[...], preferred_element_type=F32)
    cqn_ref[...] = _rms(acc[:, :ql], gq_ref[...]).astype(cqn_ref.dtype)
    ckv = _rms(acc[:, ql:ql + kl], gkv_ref[...])
    ckv_ref[...] = ckv
    ckvb_ref[...] = ckv.astype(ckvb_ref.dtype)
    kr = _rope_hi(acc[:, ql + kl:ql + kl + LANES], c_ref[...], sa_ref[...], sb_ref[...])
    kr_ref[...] = kr[:, :rope]
    krp_ref[...] = kr.astype(krp_ref.dtype)


def proj_small(xn, wa, gq, gkv, c, sa, sb, ql, kl, rope, tm=256):
    m, d = xn.shape
    n = wa.shape[1]
    row = lambda w: pl.BlockSpec((tm, w), lambda i: (i, 0))
    full = lambda a: pl.BlockSpec(a.shape, lambda i: (0, 0))
    return pl.pallas_call(
        functools.partial(_proj_small_kernel, ql=ql, kl=kl, rope=rope),
        out_shape=(jax.ShapeDtypeStruct((m, ql), BF16),
                   jax.ShapeDtypeStruct((m, kl), F32),
                   jax.ShapeDtypeStruct((m, kl), BF16),
                   jax.ShapeDtypeStruct((m, rope), F32),
                   jax.ShapeDtypeStruct((m, LANES), BF16)),
        grid=(m // tm,),
        in_specs=[row(d), full(wa), full(gq), full(gkv), row(LANES), row(LANES), row(LANES)],
        out_specs=(row(ql), row(kl), row(kl), row(rope), row(LANES)),
        compiler_params=_params(("parallel",), VMEM_LIMIT_BYTES),
        name="proj_small",
    )(xn, wa, gq, gkv, c, sa, sb)


def _proj_big_kernel(x_ref, w_ref, cos_ref, sin_ref, o_ref, *, n_rope, dk, k_scale):
    j = pl.program_id(0)
    acc = jnp.dot(x_ref[...], w_ref[...], preferred_element_type=F32)

    @pl.when(j >= n_rope)
    def _():
        o_ref[...] = acc.astype(o_ref.dtype)

    @pl.when(j < n_rope)
    def _():
        scale = jnp.where(j >= n_rope // 2, k_scale, 1.0).astype(F32)
        c = cos_ref[...] * scale
        s = sin_ref[...] * scale
        half = dk // 2
        for h in range(acc.shape[1] // dk):
            x1 = acc[:, h * dk:h * dk + half]
            x2 = acc[:, h * dk + half:(h + 1) * dk]
            o_ref[:, h * dk:h * dk + half] = (x1 * c - x2 * s).astype(o_ref.dtype)
            o_ref[:, h * dk + half:(h + 1) * dk] = (x2 * c + x1 * s).astype(o_ref.dtype)


def proj_big(xn, wb, cos_r, sin_r, rope_cols, dk, k_scale, tm=None, tn=1024):
    m, d = xn.shape
    n = wb.shape[1]
    if tm is None:
        tm = max(t for t in range(16, PROJ_ROWS + 1, 16) if m % t == 0)
    return pl.pallas_call(
        functools.partial(_proj_big_kernel, n_rope=rope_cols // tn, dk=dk, k_scale=k_scale),
        out_shape=jax.ShapeDtypeStruct((m, n), BF16),
        grid=(n // tn, m // tm),
        in_specs=[pl.BlockSpec((tm, d), lambda j, i: (i, 0)),
                  pl.BlockSpec((d, tn), lambda j, i: (0, j)),
                  pl.BlockSpec((tm, dk // 2), lambda j, i: (i, 0)),
                  pl.BlockSpec((tm, dk // 2), lambda j, i: (i, 0))],
        out_specs=pl.BlockSpec((tm, tn), lambda j, i: (i, j)),
        compiler_params=_params(("parallel", "parallel"), VMEM_LIMIT_BYTES),
        name="proj_big",
    )(xn, wb, cos_r, sin_r)


def _q_proj_kernel(x_ref, w_ref, c_ref, sa_ref, sb_ref, o_ref, *, heads, scale):
    acc = jnp.dot(x_ref[...], w_ref[...], preferred_element_type=F32)
    c = c_ref[...] * scale
    sa = sa_ref[...] * scale
    sb = sb_ref[...] * scale
    for h in range(heads):
        lo = 2 * h * LANES
        o_ref[:, lo:lo + LANES] = (acc[:, lo:lo + LANES] * scale).astype(o_ref.dtype)
        o_ref[:, lo + LANES:lo + 2 * LANES] = _rope_hi(
            acc[:, lo + LANES:lo + 2 * LANES], c, sa, sb).astype(o_ref.dtype)


def q_proj(cqn, wq, c, sa, sb, heads, scale, tm=512):
    m, k = cqn.shape
    n = wq.shape[1]
    row = lambda w: pl.BlockSpec((tm, w), lambda i: (i, 0))
    return pl.pallas_call(
        functools.partial(_q_proj_kernel, heads=heads, scale=scale),
        out_shape=jax.ShapeDtypeStruct((m, n), BF16),
        grid=(m // tm,),
        in_specs=[row(k), pl.BlockSpec((k, n), lambda i: (0, 0)), row(LANES), row(LANES), row(LANES)],
        out_specs=row(n),
        compiler_params=_params(("parallel",), VMEM_LIMIT_BYTES),
        name="q_proj",
    )(cqn, wq, c, sa, sb)


def _kv_expand_kernel(c_ref, kr_ref, wk_ref, wvt_ref, k_ref, vt_ref, *, heads):
    c = c_ref[...]
    kn = jnp.dot(c, wk_ref[...], preferred_element_type=F32)
    vt_ref[...] = lax.dot_general(wvt_ref[...], c, (((1,), (1,)), ((), ())),
                                  preferred_element_type=F32).astype(vt_ref.dtype)
    kr = kr_ref[...]
    for h in range(heads):
        k_ref[:, 2 * h * LANES:(2 * h + 1) * LANES] = kn[:, h * LANES:(h + 1) * LANES].astype(k_ref.dtype)
        k_ref[:, (2 * h + 1) * LANES:(2 * h + 2) * LANES] = kr


def kv_expand(c_bf, kr_pad, wk, wvt, rows, heads, tm=512):
    kl = c_bf.shape[1]
    dv = heads * LANES
    row = lambda w: pl.BlockSpec((tm, w), lambda i: (i, 0))
    full = lambda a: pl.BlockSpec(a.shape, lambda i: (0, 0))
    return pl.pallas_call(
        functools.partial(_kv_expand_kernel, heads=heads),
        out_shape=(jax.ShapeDtypeStruct((rows, 2 * heads * LANES), BF16),
                   jax.ShapeDtypeStruct((dv, rows), BF16)),
        grid=(rows // tm,),
        in_specs=[row(kl), row(LANES), full(wk), full(wvt)],
        out_specs=(row(2 * heads * LANES), pl.BlockSpec((dv, tm), lambda i: (0, i))),
        compiler_params=_params(("parallel",), VMEM_LIMIT_BYTES),
        name="kv_expand",
    )(c_bf, kr_pad, wk, wvt)


def _attn_absorbed_kernel(q_ref, c_ref, kr_ref, wukt_ref, wuv_ref, o_ref, *, heads):
    tq = q_ref.shape[0]
    c = c_ref[...]
    kr = kr_ref[...]
    nt = (((1,), (1,)), ((), ()))
    for g in range(heads // ABSORBED_HEADS):
        hs = range(g * ABSORBED_HEADS, (g + 1) * ABSORBED_HEADS)
        q_lat = jnp.concatenate(
            [jnp.dot(q_ref[:, 2 * h * LANES:(2 * h + 1) * LANES], wukt_ref[h],
                     preferred_element_type=F32).astype(BF16) for h in hs], axis=0)
        q_rope = jnp.concatenate([q_ref[:, (2 * h + 1) * LANES:(2 * h + 2) * LANES] for h in hs], axis=0)
        s = (lax.dot_general(q_lat, c, nt, preferred_element_type=F32)
             + lax.dot_general(q_rope, kr, nt, preferred_element_type=F32))
        p = jnp.exp2(s - jnp.max(s, axis=-1, keepdims=True))
        l = jnp.sum(p, axis=-1, keepdims=True)
        o_lat = (jnp.dot(p.astype(BF16), c, preferred_element_type=F32) / l).astype(BF16)
        for i, h in enumerate(hs):
            o_ref[:, h * LANES:(h + 1) * LANES] = jnp.dot(
                o_lat[i * tq:(i + 1) * tq], wuv_ref[h], preferred_element_type=F32).astype(o_ref.dtype)


def attention_absorbed(q, c_all, kr_all, wukt, wuv, heads, *, batch, tq, q_row0, tk):
    assert heads % ABSORBED_HEADS == 0 and q_row0 % tq == 0
    kl = c_all.shape[1]
    qb0 = q_row0 // tq
    full = lambda a: pl.BlockSpec(a.shape, lambda b: (0, 0, 0))
    return pl.pallas_call(
        functools.partial(_attn_absorbed_kernel, heads=heads),
        out_shape=jax.ShapeDtypeStruct((batch * tq, heads * LANES), BF16),
        grid=(batch,),
        in_specs=[pl.BlockSpec((tq, 2 * heads * LANES), lambda b: (qb0 + b, 0)),
                  pl.BlockSpec((tk, kl), lambda b: (b, 0)),
                  pl.BlockSpec((tk, LANES), lambda b: (b, 0)),
                  full(wukt), full(wuv)],
        out_specs=pl.BlockSpec((tq, heads * LANES), lambda b: (b, 0)),
        compiler_params=_params(("parallel",), VMEM_LIMIT_BYTES),
        name="attention_absorbed",
    )(q, c_all, kr_all, wukt, wuv)


def _attn_t_kernel(q_ref, k_ref, vt_ref, bias_ref, o_ref, sa_ref, sb_ref, *, tq, tk):
    i = pl.program_id(1)
    dq, dv = 2 * LANES, LANES
    n_heads = q_ref.shape[1] // dq

    def scores(j, s_ref):
        off = pl.multiple_of(j * tk, tk)
        for h in range(n_heads):
            q = q_ref[:, h * dq:(h + 1) * dq]
            k = k_ref[pl.ds(off, tk), h * dq:(h + 1) * dq]
            s_ref[h] = lax.dot_general(k, q, (((1,), (1,)), ((), ())), preferred_element_type=F32)

    def softmax_pv(j, s_ref, carries, masked):
        off = pl.multiple_of(j * tk, tk)
        out = []
        for h in range(n_heads):
            m, l, acc = carries[h]
            s = s_ref[h]
            if masked:
                s = jnp.minimum(s, bias_ref[...])
            m_new = jnp.maximum(m, jnp.max(s, axis=0, keepdims=True))
            alpha = jnp.exp2(m - m_new)
            p = jnp.exp2(s - m_new)
            l = alpha * l + jnp.sum(p, axis=0, keepdims=True)
            vt = vt_ref[h * dv:(h + 1) * dv, pl.ds(off, tk)]
            acc = alpha * acc + jnp.dot(vt, p.astype(BF16), preferred_element_type=F32)
            out.append((m_new, l, acc))
        return tuple(out)

    def finish(carries):
        for h, (_, l, acc) in enumerate(carries):
            o_ref[:, h * dv:(h + 1) * dv] = (acc / l).T.astype(o_ref.dtype)

    scores(0, sa_ref)

    def pair(t, carries):
        scores(2 * t + 1, sb_ref)
        carries = softmax_pv(2 * t, sa_ref, carries, False)
        scores(2 * t + 2, sa_ref)
        return softmax_pv(2 * t + 1, sb_ref, carries, False)

    init = (jnp.full((1, tq), -jnp.inf, F32), jnp.zeros((1, tq), F32), jnp.zeros((dv, tq), F32))
    carries = lax.fori_loop(0, i // 2, pair, (init,) * n_heads)

    @pl.when(i % 2 == 0)
    def _():
        finish(softmax_pv(i, sa_ref, carries, True))

    @pl.when(i % 2 == 1)
    def _():
        scores(i, sb_ref)
        finish(softmax_pv(i, sb_ref, softmax_pv(i - 1, sa_ref, carries, False), True))


def attention_causal(q, k, vt, heads, *, t, tq, hp=2):
    assert tq % CHUNK == 0 and heads % hp == 0
    dq = 2 * LANES
    chunk_of = jnp.arange(tq, dtype=jnp.int32) // CHUNK
    ceiling = jnp.where(chunk_of[:, None] <= chunk_of[None, :],
                        jnp.finfo(F32).max, -1e30).astype(F32)
    return pl.pallas_call(
        functools.partial(_attn_t_kernel, tq=tq, tk=tq),
        out_shape=jax.ShapeDtypeStruct((t, heads * LANES), BF16),
        grid=(heads // hp, t // tq),
        in_specs=[pl.BlockSpec((tq, hp * dq), lambda h, i: (i, h)),
                  pl.BlockSpec((t, hp * dq), lambda h, i: (0, h)),
                  pl.BlockSpec((hp * LANES, t), lambda h, i: (h, 0)),
                  pl.BlockSpec((tq, tq), lambda h, i: (0, 0))],
        out_specs=pl.BlockSpec((tq, hp * LANES), lambda h, i: (i, h)),
        scratch_shapes=[pltpu.VMEM((hp, tq, tq), F32), pltpu.VMEM((hp, tq, tq), F32)],
        compiler_params=_params(("parallel", "arbitrary"), VMEM_LIMIT_BYTES),
        name="attention_causal",
    )(q, k, vt, ceiling)


def _retention_kernel(q_ref, k_ref, v_ref, s0_ref, d_ref, xi_ref, zeta_ref, gc_ref,
                      o_ref, sout_ref, s_scr, *, n_chunks):
    c = pl.program_id(2)
    dk = s_scr.shape[-1]

    @pl.when(c == 0)
    def _():
        s_scr[...] = s0_ref[...]

    for h in range(s_scr.shape[0]):
        cols = slice(h * dk, (h + 1) * dk)
        q = q_ref[:, cols]
        k = k_ref[:, cols]
        v = v_ref[:, cols]
        s = s_scr[h]
        a = lax.dot_general(q, k, (((1,), (1,)), ((), ())), preferred_element_type=F32) * d_ref[h]
        o = jnp.dot(a.astype(BF16), v, preferred_element_type=F32)
        o = o + jnp.dot(q, s.astype(BF16), preferred_element_type=F32) * xi_ref[h]
        kz = (k.astype(F32) * zeta_ref[h]).astype(BF16)
        s_new = s * gc_ref[h] + lax.dot_general(kz, v, (((0,), (0,)), ((), ())),
                                                preferred_element_type=F32)
        s_scr[h] = s_new
        o_ref[:, cols] = (o * lax.rsqrt(jnp.mean(o * o, axis=-1, keepdims=True) + NORM_EPS)).astype(o_ref.dtype)

    @pl.when(c == n_chunks - 1)
    def _():
        sout_ref[...] = s_scr[...]


def retention(big, s0, tabs, heads, dk, *, batch, t_total, row0, chunk, hp=4):
    d_tab, xi_tab, zeta_tab, gc_tab = tabs
    assert heads % hp == 0
    nc = t_total // chunk
    rb0 = row0 // chunk
    hg = heads // hp
    rowblk = lambda g0: pl.BlockSpec((chunk, hp * dk), lambda b, h, c: (rb0 + b * nc + c, g0 + h))
    per_head = lambda a: pl.BlockSpec((hp,) + a.shape[1:], lambda b, h, c: (h, 0, 0))
    st_spec = pl.BlockSpec((None, hp, dk, dk), lambda b, h, c: (b, h, 0, 0))
    return pl.pallas_call(
        functools.partial(_retention_kernel, n_chunks=nc),
        out_shape=(jax.ShapeDtypeStruct((batch * t_total, heads * dk), BF16),
                   jax.ShapeDtypeStruct((batch, heads, dk, dk), F32)),
        grid=(batch, hg, nc),
        in_specs=[rowblk(0), rowblk(hg), rowblk(2 * hg), st_spec,
                  per_head(d_tab), per_head(xi_tab), per_head(zeta_tab), per_head(gc_tab)],
        out_specs=(pl.BlockSpec((chunk, hp * dk), lambda b, h, c: (b * nc + c, h)), st_spec),
        scratch_shapes=[pltpu.VMEM((hp, dk, dk), F32)],
        compiler_params=_params(("parallel", "parallel", "arbitrary"), VMEM_LIMIT_BYTES),
        name="retention",
    )(big, big, big, s0, d_tab, xi_tab, zeta_tab, gc_tab)


def retention_tables(length, heads, dk):
    lg = jnp.log1p(-jnp.exp2(-5.0 - jnp.arange(heads, dtype=F32)))
    idx = jnp.arange(length, dtype=F32)
    diff = idx[:, None] - idx[None, :]
    decay = jnp.where(diff >= 0, jnp.exp(jnp.maximum(diff, 0.0)[None] * lg[:, None, None]), 0.0)
    xi = jnp.exp((idx + 1.0)[None, :] * lg[:, None])
    zeta = jnp.exp((length - 1.0 - idx)[None, :] * lg[:, None])
    gc = jnp.exp(length * lg)
    bc = lambda a: jnp.broadcast_to(a[:, :, None], (heads, a.shape[1], dk))
    return decay, bc(xi), bc(zeta), jnp.broadcast_to(gc[:, None, None], (heads, 1, dk))


def _mixer_out_kernel(oap_ref, oas_ref, orp_ref, ors_ref, rg_ref, ga_ref, gb_ref, xp_ref, xs_ref, wo_ref,
                      g_ref, wrh_ref, wrl_ref, br_ref, h_ref, hn_ref, lg_ref, *, prompt_blocks):
    is_prompt = pl.program_id(0) < prompt_blocks
    tm = h_ref.shape[0]
    sub = tm // MIXER_SUBTILES
    for s in range(MIXER_SUBTILES):
        r = slice(s * sub, (s + 1) * sub)
        oa = jnp.where(is_prompt, oap_ref[r, :], oas_ref[r, :]).astype(F32)
        orn = jnp.where(is_prompt, orp_ref[r, :], ors_ref[r, :]).astype(F32)
        x = jnp.where(is_prompt, xp_ref[r, :], xs_ref[r, :])
        rg = rg_ref[r, :].astype(F32)
        yb = rg * _sigmoid(rg) * orn
        merged = _sigmoid(ga_ref[r, :].astype(F32)) * oa + _sigmoid(gb_ref[r, :].astype(F32)) * yb
        h = x + jnp.dot(merged.astype(BF16), wo_ref[...], preferred_element_type=F32)
        h_ref[r, :] = h
        hn = _rms(h, g_ref[...])
        hn_ref[r, :] = hn
        hn_hi = hn.astype(BF16)
        hn_lo = (hn - hn_hi.astype(F32)).astype(BF16)
        lg_ref[r, :] = (jnp.dot(hn_hi, wrh_ref[...], preferred_element_type=F32)
                        + jnp.dot(hn_lo, wrh_ref[...], preferred_element_type=F32)
                        + jnp.dot(hn_hi, wrl_ref[...], preferred_element_type=F32) + br_ref[...])


def mixer_out(oa_p, oa_s, or_p, or_s, big, gate_blk0, xp, xs, wo, g, wr, br, tm=256):
    d = xp.shape[1]
    m = xp.shape[0] + xs.shape[0]
    e = wr.shape[1]
    wr_hi = wr.astype(BF16)
    wr_lo = (wr - wr_hi.astype(F32)).astype(BF16)
    npb, prompt, sample = _group_specs(xp.shape[0], xs.shape[0], tm, d)
    row = lambda col: pl.BlockSpec((tm, d), lambda i: (i, col))
    full = lambda a: pl.BlockSpec(a.shape, lambda i: (0, 0))
    return pl.pallas_call(
        functools.partial(_mixer_out_kernel, prompt_blocks=npb),
        out_shape=(jax.ShapeDtypeStruct((m, d), F32),
                   jax.ShapeDtypeStruct((m, d), F32),
                   jax.ShapeDtypeStruct((m, e), F32)),
        grid=(m // tm,),
        in_specs=[prompt, sample, prompt, sample, row(gate_blk0), row(gate_blk0 + 1), row(gate_blk0 + 2),
                  prompt, sample, full(wo), full(g), full(wr_hi), full(wr_lo), full(br)],
        out_specs=(row(0), row(0), pl.BlockSpec((tm, e), lambda i: (i, 0))),
        compiler_params=_params(("parallel",), VMEM_LIMIT_BYTES),
        name="mixer_out",
    )(oa_p, oa_s, or_p, or_s, big, big, big, xp, xs, wo, g, wr_hi, wr_lo, br)


def _expert_kernel(tok_ref, ie_ref, ib_ref, inb_ref, nv_ref, hn_ref, wg_ref, wu_ref, bg_ref, bu_ref, wd_ref,
                   bd_ref, ye_ref, x32, y32, wgb, wub, wdb, sem_x, sem_out, *, rows, group, n_f, n_items,
                   n_blocks):
    k = pl.program_id(0)
    f = pl.program_id(1)
    nb = inb_ref[k]
    blk0 = ib_ref[k]
    n_valid = nv_ref[0]
    valid = k < n_valid
    has_next = k + 1 < n_valid
    last_f = f == n_f - 1
    cur = k % 2
    rs = rows * group
    d = y32.shape[1]
    per_block = rs // (n_f * group)

    def gather_copy(slot, buf, row):
        return pltpu.make_async_copy(hn_ref.at[pl.ds(tok_ref[slot], 1), :],
                                     x32.at[buf, pl.ds(row, 1), :], sem_x.at[buf])

    def out_copy(b):
        return pltpu.make_async_copy(y32.at[pl.ds(b * rows, rows), :],
                                     ye_ref.at[pl.ds((blk0 + b) * rows, rows), :], sem_out)

    @pl.when((k == 0) & (f == 0))
    def _():
        def body(r, _):
            gather_copy(blk0 * rows + r, 0, r).start()
            return 0
        lax.fori_loop(0, rs, body, 0, unroll=8)

    @pl.when(valid & (f == 0))
    def _():
        def drain(r, _):
            gather_copy(0, cur, 0).wait()
            return 0
        lax.fori_loop(0, rs, drain, 0, unroll=8)

        def init(b, _):
            y32[pl.ds(pl.multiple_of(b * rows, rows), rows), :] = jnp.broadcast_to(bd_ref[...], (rows, d))
            return 0
        lax.fori_loop(0, nb, init, 0)

    def cast_weights():
        wg, wu, wd = wg_ref[...].astype(BF16), wu_ref[...].astype(BF16), wd_ref[...].astype(BF16)
        wgb[...] = wg
        wub[...] = wu
        wdb[...] = wd
        return wg, wu, wd

    @pl.when(valid & (nb < MOE_CHAIN))
    def _():
        cast_weights()

    def prefetch(b, next_slot0):
        row0 = (f * group + b) * per_block
        for r in range(per_block):
            gather_copy(next_slot0 + row0 + r, 1 - cur, row0 + r).start(priority=1)

    def compute_blocks(b0, n, next_slot0, cast=False):
        r = pl.ds(pl.multiple_of(b0 * rows, rows), n * rows)
        x = x32[cur, r, :].astype(BF16)
        wg, wu, wd = cast_weights() if cast else (wgb[...], wub[...], wdb[...])
        gate = jnp.minimum(jnp.dot(x, wg, preferred_element_type=F32) + bg_ref[...], SWIGLU_LIMIT)
        up = jnp.clip(jnp.dot(x, wu, preferred_element_type=F32) + bu_ref[...],
                      -SWIGLU_LIMIT, SWIGLU_LIMIT)
        h = (up + 1.0) * (gate * _sigmoid(gate * SWIGLU_ALPHA))
        y32[r, :] += jnp.dot(h.astype(BF16), wd, preferred_element_type=F32)
        if next_slot0 is not None:
            for i in range(n):
                prefetch(b0 + i, next_slot0)

        @pl.when(last_f)
        def _():
            for i in range(n):
                out_copy(b0 + i).start()

    def run(next_slot0):
        @pl.when(nb >= MOE_CHAIN)
        def _():
            compute_blocks(0, MOE_CHAIN, next_slot0, cast=True)

        def chain(p, _):
            compute_blocks(MOE_CHAIN * p, MOE_CHAIN, next_slot0)
            return 0
        lax.fori_loop(1, nb // MOE_CHAIN, chain, 0)
        n = MOE_CHAIN // 2
        while n >= 1:
            @pl.when(nb & n != 0)
            def _(n=n):
                compute_blocks(nb - (nb & (2 * n - 1)), n, next_slot0)
            n //= 2

    @pl.when(has_next)
    def _():
        next_slot0 = ib_ref[k + 1] * rows
        run(next_slot0)

        def rest(b, _):
            prefetch(b, next_slot0)
            return 0
        lax.fori_loop(nb, group, rest, 0)

    @pl.when(valid & jnp.logical_not(has_next))
    def _():
        run(None)

    @pl.when(valid & last_f)
    def _():
        def body(b, _):
            out_copy(b).wait()
            return 0
        lax.fori_loop(0, nb, body, 0)

    @pl.when((k == n_items - 1) & last_f)
    def _():
        used = nv_ref[1]
        y32[0:rows, :] = jnp.zeros((rows, d), F32)

        def fill(b):
            return pltpu.make_async_copy(y32.at[pl.ds(0, rows), :], ye_ref.at[pl.ds(b * rows, rows), :], sem_out)

        def start(b, _):
            fill(b).start()
            return 0

        def wait(b, _):
            fill(b).wait()
            return 0
        lax.fori_loop(used, n_blocks, start, 0)
        lax.fori_loop(used, n_blocks, wait, 0)


def expert_mlp(slot_tok, items, hn, w_gate_up, b_gate_up, w_down, b_down, n_items, n_blocks):
    item_e, item_b0, item_nb, counts = items
    n_exp, d, two_f = w_gate_up.shape
    d_ff = two_f // 2
    tf = MOE_FTILE
    n_f = d_ff // tf
    rows, group = MOE_ROWS, MOE_GROUP
    rs = rows * group
    assert rs % (n_f * group) == 0

    def fi(k, f, nv):
        return jnp.where(k < nv[0], f, n_f - 1)

    in_specs = [
        pl.BlockSpec(memory_space=pl.ANY),
        pl.BlockSpec((None, d, tf), lambda k, f, tok, ie, ib, inb, nv: (ie[k], 0, fi(k, f, nv))),
        pl.BlockSpec((None, d, tf), lambda k, f, tok, ie, ib, inb, nv: (ie[k], 0, n_f + fi(k, f, nv))),
        pl.BlockSpec((None, 1, tf), lambda k, f, tok, ie, ib, inb, nv: (ie[k], 0, fi(k, f, nv))),
        pl.BlockSpec((None, 1, tf), lambda k, f, tok, ie, ib, inb, nv: (ie[k], 0, n_f + fi(k, f, nv))),
        pl.BlockSpec((None, tf, d), lambda k, f, tok, ie, ib, inb, nv: (ie[k], fi(k, f, nv), 0)),
        pl.BlockSpec((None, 1, d), lambda k, f, tok, ie, ib, inb, nv: (ie[k], 0, 0)),
    ]
    return pl.pallas_call(
        functools.partial(_expert_kernel, rows=rows, group=group, n_f=n_f, n_items=n_items, n_blocks=n_blocks),
        out_shape=jax.ShapeDtypeStruct((n_blocks * rows, d), F32),
        grid_spec=pltpu.PrefetchScalarGridSpec(
            num_scalar_prefetch=5, grid=(n_items, n_f),
            in_specs=in_specs,
            out_specs=pl.BlockSpec(memory_space=pl.ANY),
            scratch_shapes=[pltpu.VMEM((2, rs, d), F32), pltpu.VMEM((rs, d), F32),
                            pltpu.VMEM((d, tf), BF16), pltpu.VMEM((d, tf), BF16), pltpu.VMEM((tf, d), BF16),
                            pltpu.SemaphoreType.DMA((2,)), pltpu.SemaphoreType.DMA]),
        compiler_params=_params(("arbitrary", "arbitrary"), VMEM_LIMIT_BYTES),
        name="moe_experts",
    )(slot_tok, item_e, item_b0, item_nb, counts, hn, w_gate_up, w_gate_up,
      b_gate_up.reshape(n_exp, 1, two_f), b_gate_up.reshape(n_exp, 1, two_f),
      w_down, b_down.reshape(n_exp, 1, d))


def _combine_kernel(pos_ref, ye_ref, w_ref, h_ref, g_ref, o_ref, buf, sems, *, tc, blk0, n_steps, final_norm):
    i = pl.program_id(0)
    n_rows = TOP_K * tc

    def copy(r, src_row, par):
        return pltpu.make_async_copy(ye_ref.at[pl.ds(src_row, 1), :], buf.at[par, pl.ds(r, 1), :],
                                     sems.at[par])

    def start(r, step, par, j):
        copy(r, pos_ref[(blk0 + step) * n_rows + r], par).start(priority=j % 2)

    def drain(par):
        def body(r, _):
            copy(0, 0, par).wait()
            return 0
        lax.fori_loop(0, n_rows, body, 0, unroll=8)

    @pl.when(i == 0)
    def _():
        def body(g, _):
            for j in range(8):
                start(g * 8 + j, 0, 0, j)
            return 0
        lax.fori_loop(0, n_rows // 8, body, 0)

    par = i % 2
    drain(par)

    nxt = jnp.minimum(i + 1, n_steps - 1)
    for r in range(n_rows):
        start(r, nxt, 1 - par, r)
    w = w_ref[...]
    y = h_ref[...]
    for k in range(TOP_K):
        y = y + w[:, k:k + 1] * buf[par, k * tc:(k + 1) * tc, :]
    o_ref[...] = _rms(y, g_ref[...]) if final_norm else y

    @pl.when(i == n_steps - 1)
    def _():
        drain(1 - par)


def combine(pos, ye, top_w, h, g, row0, rows, final_norm, tc=COMBINE_ROWS):
    d = h.shape[1]
    blk0 = row0 // tc
    n_steps = rows // tc
    return pl.pallas_call(
        functools.partial(_combine_kernel, tc=tc, blk0=blk0, n_steps=n_steps, final_norm=final_norm),
        out_shape=jax.ShapeDtypeStruct((rows, d), F32),
        grid_spec=pltpu.PrefetchScalarGridSpec(
            num_scalar_prefetch=1, grid=(n_steps,),
            in_specs=[pl.BlockSpec(memory_space=pl.ANY),
                      pl.BlockSpec((tc, TOP_K), lambda i, p: (blk0 + i, 0)),
                      pl.BlockSpec((tc, d), lambda i, p: (blk0 + i, 0)),
                      pl.BlockSpec((1, d), lambda i, p: (0, 0))],
            out_specs=pl.BlockSpec((tc, d), lambda i, p: (i, 0)),
            scratch_shapes=[pltpu.VMEM((2, TOP_K * tc, d), F32), pltpu.SemaphoreType.DMA((2,))]),
        compiler_params=_params(("arbitrary",), VMEM_LIMIT_BYTES),
        name="moe_combine",
    )(pos, ye, top_w, h, g)


def route(logits, tc):
    m, n_exp = logits.shape
    a = m * TOP_K
    rows, group = MOE_ROWS, MOE_GROUP
    top_logit, top_idx = lax.top_k(logits, TOP_K)
    top_w = jax.nn.softmax(top_logit, axis=-1)
    flat_e = top_idx.reshape(a)
    onehot = (flat_e[:, None] == jnp.arange(n_exp, dtype=jnp.int32)[None, :]).astype(jnp.int32)
    csum = jnp.cumsum(onehot, axis=0)
    counts = csum[-1]
    rank = jnp.take_along_axis(csum, flat_e[:, None], axis=1)[:, 0] - 1
    nb_e = (counts + rows - 1) // rows
    blk_start = jnp.cumsum(nb_e) - nb_e
    dest = blk_start[flat_e] * rows + rank
    n_blocks = a // rows + n_exp
    slot_tok = jnp.zeros(((n_blocks + group) * rows,), jnp.int32).at[dest].set(
        jnp.arange(a, dtype=jnp.int32) // TOP_K)
    ni_e = (nb_e + group - 1) // group
    item_end = jnp.cumsum(ni_e)
    n_valid = item_end[-1]
    n_items = (n_blocks + n_exp * (group - 1)) // group
    kk = jnp.arange(n_items, dtype=jnp.int32)
    item_e = jnp.minimum(jnp.sum(item_end[None, :] <= kk[:, None], axis=1), n_exp - 1).astype(jnp.int32)
    local = kk - (item_end - ni_e)[item_e]
    valid = kk < n_valid
    item_e = jnp.where(valid, item_e, item_e[jnp.maximum(n_valid - 1, 0)])
    item_b0 = jnp.where(valid, blk_start[item_e] + local * group, 0).astype(jnp.int32)
    item_nb = jnp.where(valid, jnp.clip(nb_e[item_e] - local * group, 0, group), 0).astype(jnp.int32)
    pos = dest.reshape(m // tc, tc, TOP_K).transpose(0, 2, 1).reshape(a)
    counts = jnp.stack([n_valid, jnp.sum(nb_e)]).astype(jnp.int32)
    items = (item_e, item_b0, item_nb, counts)
    return top_w, slot_tok, pos.astype(jnp.int32), items, n_items, n_blocks


def _rope_tables(pos, half):
    inv = jnp.power(ROPE_THETA, -jnp.arange(half, dtype=F32) / half)
    ang = pos.astype(F32)[:, None] * inv[None, :]
    return jnp.cos(ang), jnp.sin(ang)


def kernel(x_prompt, x_sample, cache_latent, cache_k_rope, state_retention, g_mix_norm, w_in, g_q_norm, w_uq, g_kv_norm, w_uk, w_uv, w_o, g_ffn_norm, w_router, b_router, w_gate_up, b_gate_up, w_down, b_down, g_final):
    bp, t, d = x_prompt.shape
    db, ts, _ = x_sample.shape
    depth = w_in.shape[0]
    past = cache_latent.shape[2]
    ql, ha, qk = w_uq.shape[1:]
    kl = w_uk.shape[1]
    rope = cache_k_rope.shape[-1]
    nope = qk - rope
    hr, dk = state_retention.shape[2:4]
    assert bp == 1 and nope == LANES and w_uv.shape[-1] == LANES and 2 * rope == LANES
    assert past % CHUNK == 0 and ts <= CHUNK
    mp, ms = bp * t, db * ts
    m = mp + ms
    tk_s = past + ts
    mla_scale = float(qk) ** -0.5 * LOG2_E

    pos = jnp.concatenate([jnp.arange(t, dtype=jnp.int32),
                           jnp.tile(past + jnp.arange(ts, dtype=jnp.int32), db)])
    cos_a, sin_a = _rope_tables(pos, rope // 2)
    z = lambda w: jnp.zeros((m, w), F32)
    rope_c = jnp.concatenate([cos_a, cos_a, z(LANES - rope)], axis=1)
    rope_sa = jnp.concatenate([z(rope // 2), sin_a, z(LANES - rope)], axis=1)
    rope_sb = jnp.concatenate([-sin_a, z(LANES - rope // 2)], axis=1)
    cos_r, sin_r = _rope_tables(pos, dk // 2)
    tabs_p = retention_tables(RET_CHUNK, hr, dk)
    tabs_s = retention_tables(ts, hr, dk)

    xp, xs = x_prompt.reshape(mp, d), x_sample.reshape(ms, d)
    lat_p, kr_p, st_p, lat_s, kr_s, st_s = [], [], [], [], [], []
    small = ql + kl + rope
    for l in range(depth):
        wa = jnp.pad(w_in[l][:, :small], ((0, 0), (0, LANES - rope))).astype(BF16)
        wb = w_in[l][:, small:].astype(BF16)
        wq = jnp.pad(w_uq[l], ((0, 0), (0, 0), (0, 2 * LANES - qk))).reshape(ql, ha * 2 * LANES).astype(BF16)
        wk = w_uk[l].reshape(kl, ha * nope).astype(BF16)
        wv = w_uv[l].reshape(kl, ha * LANES).astype(BF16)
        wo = w_o[l].astype(BF16)

        xn = norm_cast(xp, xs, g_mix_norm[l].reshape(1, d))
        cqn, ckv, ckv_b, kr, kr_pad = proj_small(xn, wa, g_q_norm[l].reshape(1, ql), g_kv_norm[l].reshape(1, kl),
                                                 rope_c, rope_sa, rope_sb, ql, kl, rope)
        big = proj_big(xn, wb, cos_r, sin_r, 2 * hr * dk, dk, float(dk) ** -0.5)
        qcat = q_proj(cqn, wq, rope_c, rope_sa, rope_sb, ha, mla_scale)

        k_p, vt_p = kv_expand(ckv_b, kr_pad, wk, wv.T, mp, ha)
        oa_p = attention_causal(qcat, k_p, vt_p, ha, t=mp, tq=512)
        c_all = jnp.concatenate([cache_latent[l], ckv[mp:].reshape(db, ts, kl)], axis=1)
        kr_all = jnp.concatenate([cache_k_rope[l], kr[mp:].reshape(db, ts, rope)], axis=1)
        kr_all = jnp.pad(kr_all, ((0, 0), (0, 0), (0, LANES - rope)))
        oa_s = attention_absorbed(qcat, c_all.reshape(db * tk_s, kl).astype(BF16),
                                  kr_all.reshape(db * tk_s, LANES).astype(BF16),
                                  w_uk[l].transpose(1, 2, 0).astype(BF16), w_uv[l].transpose(1, 0, 2).astype(BF16),
                                  ha, batch=db, tq=ts, q_row0=mp, tk=tk_s)

        or_p, s_p = retention(big, jnp.zeros((bp, hr, dk, dk), F32), tabs_p, hr, dk,
                              batch=bp, t_total=t, row0=0, chunk=RET_CHUNK)
        or_s, s_s = retention(big, state_retention[l].astype(F32), tabs_s, hr, dk,
                              batch=db, t_total=ts, row0=mp, chunk=ts)

        h1, hn, logits = mixer_out(oa_p, oa_s, or_p, or_s, big, 3, xp, xs, wo, g_ffn_norm[l].reshape(1, d),
                                   w_router[l], b_router[l].reshape(1, -1))

        top_w, slot_tok, pos_flat, items, n_items, n_blocks = route(logits, COMBINE_ROWS)
        ye = expert_mlp(slot_tok, items, hn, w_gate_up[l], b_gate_up[l], w_down[l], b_down[l], n_items, n_blocks)
        last = l == depth - 1
        g_out = g_final.reshape(1, d)
        y_p = combine(pos_flat, ye, top_w, h1, g_out, 0, mp, last)
        y_s = combine(pos_flat, ye, top_w, h1, g_out, mp, ms, last)
        xp, xs = y_p, y_s

        lat_p.append(ckv[:mp].reshape(bp, t, kl))
        kr_p.append(kr[:mp].reshape(bp, t, rope))
        st_p.append(s_p)
        lat_s.append(ckv[mp:].reshape(db, ts, kl))
        kr_s.append(kr[mp:].reshape(db, ts, rope))
        st_s.append(s_s)

    return (y_p.reshape(bp, t, d), y_s.reshape(db, ts, d),
            jnp.stack(lat_p), jnp.stack(kr_p), jnp.stack(st_p),
            jnp.stack(lat_s), jnp.stack(kr_s), jnp.stack(st_s))
```

```python
import functools

import jax
import jax.numpy as jnp
from jax import lax
from jax.experimental import pallas as pl
from jax.experimental.pallas import tpu as pltpu

F32 = jnp.float32
BF16 = jnp.bfloat16

NORM_EPS = 1e-6
ROPE_THETA = 10000.0
CHUNK = 64
TOP_K = 4
SWIGLU_LIMIT = 7.0
SWIGLU_ALPHA = 1.702
LOG2_E = 1.4426950408889634

LANES = 128
VMEM_LIMIT_BYTES = 56 * 1024 * 1024

RET_CHUNK = 256
MOE_ROWS = 128
MOE_GROUP = 10
MOE_CHAIN = 8
PROJ_ROWS = 1152
MOE_FTILE = 256
COMBINE_ROWS = 128
MIXER_SUBTILES = 2
ABSORBED_HEADS = 4


def _params(sem, vmem=None):
    return pltpu.CompilerParams(dimension_semantics=sem, vmem_limit_bytes=vmem)


def _rms(x, g):
    return x * lax.rsqrt(jnp.mean(x * x, axis=-1, keepdims=True) + NORM_EPS) * g


def _sigmoid(x):
    return 0.5 * jnp.tanh(0.5 * x) + 0.5


def _group_specs(rows_p, rows_s, tm, d):
    npb, nsb = rows_p // tm, rows_s // tm
    prompt = pl.BlockSpec((tm, d), lambda i: (jnp.minimum(i, npb - 1), 0))
    sample = pl.BlockSpec((tm, d), lambda i: (jnp.clip(i - npb, 0, nsb - 1), 0))
    return npb, prompt, sample


def _norm_cast_kernel(xp_ref, xs_ref, g_ref, o_ref, *, prompt_blocks):
    x = jnp.where(pl.program_id(0) < prompt_blocks, xp_ref[...], xs_ref[...])
    o_ref[...] = _rms(x, g_ref[...]).astype(o_ref.dtype)


def norm_cast(xp, xs, g, tm=512):
    d = xp.shape[1]
    m = xp.shape[0] + xs.shape[0]
    npb, prompt, sample = _group_specs(xp.shape[0], xs.shape[0], tm, d)
    return pl.pallas_call(
        functools.partial(_norm_cast_kernel, prompt_blocks=npb),
        out_shape=jax.ShapeDtypeStruct((m, d), BF16),
        grid=(m // tm,),
        in_specs=[prompt, sample, pl.BlockSpec((1, d), lambda i: (0, 0))],
        out_specs=pl.BlockSpec((tm, d), lambda i: (i, 0)),
        compiler_params=_params(("parallel",), VMEM_LIMIT_BYTES),
        name="norm_cast",
    )(xp, xs, g)


def _rope_hi(x, c, sa, sb):
    return x * c + pltpu.roll(x, 32, 1) * sa + pltpu.roll(x, 96, 1) * sb


def _proj_small_kernel(x_ref, w_ref, gq_ref, gkv_ref, c_ref, sa_ref, sb_ref,
                       cqn_ref, ckv_ref, ckvb_ref, kr_ref, krp_ref, *, ql, kl, rope):
    acc = jnp.dot(x_ref[...], w_ref[...], preferred_element_type=F32)
    cqn_ref[...] = _rms(acc[:, :ql], gq_ref[...]).astype(cqn_ref.dtype)
    ckv = _rms(acc[:, ql:ql + kl], gkv_ref[...])
    ckv_ref[...] = ckv
    ckvb_ref[...] = ckv.astype(ckvb_ref.dtype)
    kr = _rope_hi(acc[:, ql + kl:ql + kl + LANES], c_ref[...], sa_ref[...], sb_ref[...])
    kr_ref[...] = kr[:, :rope]
    krp_ref[...] = kr.astype(krp_ref.dtype)


def proj_small(xn, wa, gq, gkv, c, sa, sb, ql, kl, rope, tm=256):
    m, d = xn.shape
    n = wa.shape[1]
    row = lambda w: pl.BlockSpec((tm, w), lambda i: (i, 0))
    full = lambda a: pl.BlockSpec(a.shape, lambda i: (0, 0))
    return pl.pallas_call(
        functools.partial(_proj_small_kernel, ql=ql, kl=kl, rope=rope),
        out_shape=(jax.ShapeDtypeStruct((m, ql), BF16),
                   jax.ShapeDtypeStruct((m, kl), F32),
                   jax.ShapeDtypeStruct((m, kl), BF16),
                   jax.ShapeDtypeStruct((m, rope), F32),
                   jax.ShapeDtypeStruct((m, LANES), BF16)),
        grid=(m // tm,),
        in_specs=[row(d), full(wa), full(gq), full(gkv), row(LANES), row(LANES), row(LANES)],
        out_specs=(row(ql), row(kl), row(kl), row(rope), row(LANES)),
        compiler_params=_params(("parallel",), VMEM_LIMIT_BYTES),
        name="proj_small",
    )(xn, wa, gq, gkv, c, sa, sb)


def _proj_big_kernel(x_ref, w_ref, cos_ref, sin_ref, o_ref, *, n_rope, dk, k_scale):
    j = pl.program_id(0)
    acc = jnp.dot(x_ref[...], w_ref[...], preferred_element_type=F32)

    @pl.when(j >= n_rope)
    def _():
        o_ref[...] = acc.astype(o_ref.dtype)

    @pl.when(j < n_rope)
    def _():
        scale = jnp.where(j >= n_rope // 2, k_scale, 1.0).astype(F32)
        c = cos_ref[...] * scale
        s = sin_ref[...] * scale
        half = dk // 2
        for h in range(acc.shape[1] // dk):
            x1 = acc[:, h * dk:h * dk + half]
            x2 = acc[:, h * dk + half:(h + 1) * dk]
            o_ref[:, h * dk:h * dk + half] = (x1 * c - x2 * s).astype(o_ref.dtype)
            o_ref[:, h * dk + half:(h + 1) * dk] = (x2 * c + x1 * s).astype(o_ref.dtype)


def proj_big(xn, wb, cos_r, sin_r, rope_cols, dk, k_scale, tm=None, tn=1024):
    m, d = xn.shape
    n = wb.shape[1]
    if tm is None:
        tm = max(t for t in range(16, PROJ_ROWS + 1, 16) if m % t == 0)
    return pl.pallas_call(
        functools.partial(_proj_big_kernel, n_rope=rope_cols // tn, dk=dk, k_scale=k_scale),
        out_shape=jax.ShapeDtypeStruct((m, n), BF16),
        grid=(n // tn, m // tm),
        in_specs=[pl.BlockSpec((tm, d), lambda j, i: (i, 0)),
                  pl.BlockSpec((d, tn), lambda j, i: (0, j)),
                  pl.BlockSpec((tm, dk // 2), lambda j, i: (i, 0)),
                  pl.BlockSpec((tm, dk // 2), lambda j, i: (i, 0))],
        out_specs=pl.BlockSpec((tm, tn), lambda j, i: (i, j)),
        compiler_params=_params(("parallel", "parallel"), VMEM_LIMIT_BYTES),
        name="proj_big",
    )(xn, wb, cos_r, sin_r)


def _q_proj_kernel(x_ref, w_ref, c_ref, sa_ref, sb_ref, o_ref, *, heads, scale):
    acc = jnp.dot(x_ref[...], w_ref[...], preferred_element_type=F32)
    c = c_ref[...] * scale
    sa = sa_ref[...] * scale
    sb = sb_ref[...] * scale
    for h in range(heads):
        lo = 2 * h * LANES
        o_ref[:, lo:lo + LANES] = (acc[:, lo:lo + LANES] * scale).astype(o_ref.dtype)
        o_ref[:, lo + LANES:lo + 2 * LANES] = _rope_hi(
            acc[:, lo + LANES:lo + 2 * LANES], c, sa, sb).astype(o_ref.dtype)


def q_proj(cqn, wq, c, sa, sb, heads, scale, tm=512):
    m, k = cqn.shape
    n = wq.shape[1]
    row = lambda w: pl.BlockSpec((tm, w), lambda i: (i, 0))
    return pl.pallas_call(
        functools.partial(_q_proj_kernel, heads=heads, scale=scale),
        out_shape=jax.ShapeDtypeStruct((m, n), BF16),
        grid=(m // tm,),
        in_specs=[row(k), pl.BlockSpec((k, n), lambda i: (0, 0)), row(LANES), row(LANES), row(LANES)],
        out_specs=row(n),
        compiler_params=_params(("parallel",), VMEM_LIMIT_BYTES),
        name="q_proj",
    )(cqn, wq, c, sa, sb)


def _kv_expand_kernel(c_ref, kr_ref, wk_ref, wvt_ref, k_ref, vt_ref, *, heads):
    c = c_ref[...]
    kn = jnp.dot(c, wk_ref[...], preferred_element_type=F32)
    vt_ref[...] = lax.dot_general(wvt_ref[...], c, (((1,), (1,)), ((), ())),
                                  preferred_element_type=F32).astype(vt_ref.dtype)
    kr = kr_ref[...]
    for h in range(heads):
        k_ref[:, 2 * h * LANES:(2 * h + 1) * LANES] = kn[:, h * LANES:(h + 1) * LANES].astype(k_ref.dtype)
        k_ref[:, (2 * h + 1) * LANES:(2 * h + 2) * LANES] = kr


def kv_expand(c_bf, kr_pad, wk, wvt, rows, heads, tm=512):
    kl = c_bf.shape[1]
    dv = heads * LANES
    row = lambda w: pl.BlockSpec((tm, w), lambda i: (i, 0))
    full = lambda a: pl.BlockSpec(a.shape, lambda i: (0, 0))
    return pl.pallas_call(
        functools.partial(_kv_expand_kernel, heads=heads),
        out_shape=(jax.ShapeDtypeStruct((rows, 2 * heads * LANES), BF16),
                   jax.ShapeDtypeStruct((dv, rows), BF16)),
        grid=(rows // tm,),
        in_specs=[row(kl), row(LANES), full(wk), full(wvt)],
        out_specs=(row(2 * heads * LANES), pl.BlockSpec((dv, tm), lambda i: (0, i))),
        compiler_params=_params(("parallel",), VMEM_LIMIT_BYTES),
        name="kv_expand",
    )(c_bf, kr_pad, wk, wvt)


def _attn_absorbed_kernel(q_ref, c_ref, kr_ref, wukt_ref, wuv_ref, o_ref, *, heads):
    tq = q_ref.shape[0]
    c = c_ref[...]
    kr = kr_ref[...]
    nt = (((1,), (1,)), ((), ()))
    for g in range(heads // ABSORBED_HEADS):
        hs = range(g * ABSORBED_HEADS, (g + 1) * ABSORBED_HEADS)
        q_lat = jnp.concatenate(
            [jnp.dot(q_ref[:, 2 * h * LANES:(2 * h + 1) * LANES], wukt_ref[h],
                     preferred_element_type=F32).astype(BF16) for h in hs], axis=0)
        q_rope = jnp.concatenate([q_ref[:, (2 * h + 1) * LANES:(2 * h + 2) * LANES] for h in hs], axis=0)
        s = (lax.dot_general(q_lat, c, nt, preferred_element_type=F32)
             + lax.dot_general(q_rope, kr, nt, preferred_element_type=F32))
        p = jnp.exp2(s - jnp.max(s, axis=-1, keepdims=True))
        l = jnp.sum(p, axis=-1, keepdims=True)
        o_lat = (jnp.dot(p.astype(BF16), c, preferred_element_type=F32) / l).astype(BF16)
        for i, h in enumerate(hs):
            o_ref[:, h * LANES:(h + 1) * LANES] = jnp.dot(
                o_lat[i * tq:(i + 1) * tq], wuv_ref[h], preferred_element_type=F32).astype(o_ref.dtype)


def attention_absorbed(q, c_all, kr_all, wukt, wuv, heads, *, batch, tq, q_row0, tk):
    assert heads % ABSORBED_HEADS == 0 and q_row0 % tq == 0
    kl = c_all.shape[1]
    qb0 = q_row0 // tq
    full = lambda a: pl.BlockSpec(a.shape, lambda b: (0, 0, 0))
    return pl.pallas_call(
        functools.partial(_attn_absorbed_kernel, heads=heads),
        out_shape=jax.ShapeDtypeStruct((batch * tq, heads * LANES), BF16),
        grid=(batch,),
        in_specs=[pl.BlockSpec((tq, 2 * heads * LANES), lambda b: (qb0 + b, 0)),
                  pl.BlockSpec((tk, kl), lambda b: (b, 0)),
                  pl.BlockSpec((tk, LANES), lambda b: (b, 0)),
                  full(wukt), full(wuv)],
        out_specs=pl.BlockSpec((tq, heads * LANES), lambda b: (b, 0)),
        compiler_params=_params(("parallel",), VMEM_LIMIT_BYTES),
        name="attention_absorbed",
    )(q, c_all, kr_all, wukt, wuv)


def _attn_t_kernel(q_ref, k_ref, vt_ref, bias_ref, o_ref, sa_ref, sb_ref, *, tq, tk):
    i = pl.program_id(1)
    dq, dv = 2 * LANES, LANES
    n_heads = q_ref.shape[1] // dq

    def scores(j, s_ref):
        off = pl.multiple_of(j * tk, tk)
        for h in range(n_heads):
            q = q_ref[:, h * dq:(h + 1) * dq]
            k = k_ref[pl.ds(off, tk), h * dq:(h + 1) * dq]
            s_ref[h] = lax.dot_general(k, q, (((1,), (1,)), ((), ())), preferred_element_type=F32)

    def softmax_pv(j, s_ref, carries, masked):
        off = pl.multiple_of(j * tk, tk)
        out = []
        for h in range(n_heads):
            m, l, acc = carries[h]
            s = s_ref[h]
            if masked:
                s = jnp.minimum(s, bias_ref[...])
            m_new = jnp.maximum(m, jnp.max(s, axis=0, keepdims=True))
            alpha = jnp.exp2(m - m_new)
            p = jnp.exp2(s - m_new)
            l = alpha * l + jnp.sum(p, axis=0, keepdims=True)
            vt = vt_ref[h * dv:(h + 1) * dv, pl.ds(off, tk)]
            acc = alpha * acc + jnp.dot(vt, p.astype(BF16), preferred_element_type=F32)
            out.append((m_new, l, acc))
        return tuple(out)

    def finish(carries):
        for h, (_, l, acc) in enumerate(carries):
            o_ref[:, h * dv:(h + 1) * dv] = (acc / l).T.astype(o_ref.dtype)

    scores(0, sa_ref)

    def pair(t, carries):
        scores(2 * t + 1, sb_ref)
        carries = softmax_pv(2 * t, sa_ref, carries, False)
        scores(2 * t + 2, sa_ref)
        return softmax_pv(2 * t + 1, sb_ref, carries, False)

    init = (jnp.full((1, tq), -jnp.inf, F32), jnp.zeros((1, tq), F32), jnp.zeros((dv, tq), F32))
    carries = lax.fori_loop(0, i // 2, pair, (init,) * n_heads)

    @pl.when(i % 2 == 0)
    def _():
        finish(softmax_pv(i, sa_ref, carries, True))

    @pl.when(i % 2 == 1)
    def _():
        scores(i, sb_ref)
        finish(softmax_pv(i, sb_ref, softmax_pv(i - 1, sa_ref, carries, False), True))


def attention_causal(q, k, vt, heads, *, t, tq, hp=2):
    assert tq % CHUNK == 0 and heads % hp == 0
    dq = 2 * LANES
    chunk_of = jnp.arange(tq, dtype=jnp.int32) // CHUNK
    ceiling = jnp.where(chunk_of[:, None] <= chunk_of[None, :],
                        jnp.finfo(F32).max, -1e30).astype(F32)
    return pl.pallas_call(
        functools.partial(_attn_t_kernel, tq=tq, tk=tq),
        out_shape=jax.ShapeDtypeStruct((t, heads * LANES), BF16),
        grid=(heads // hp, t // tq),
        in_specs=[pl.BlockSpec((tq, hp * dq), lambda h, i: (i, h)),
                  pl.BlockSpec((t, hp * dq), lambda h, i: (0, h)),
                  pl.BlockSpec((hp * LANES, t), lambda h, i: (h, 0)),
                  pl.BlockSpec((tq, tq), lambda h, i: (0, 0))],
        out_specs=pl.BlockSpec((tq, hp * LANES), lambda h, i: (i, h)),
        scratch_shapes=[pltpu.VMEM((hp, tq, tq), F32), pltpu.VMEM((hp, tq, tq), F32)],
        compiler_params=_params(("parallel", "arbitrary"), VMEM_LIMIT_BYTES),
        name="attention_causal",
    )(q, k, vt, ceiling)


def _retention_kernel(q_ref, k_ref, v_ref, s0_ref, d_ref, xi_ref, zeta_ref, gc_ref,
                      o_ref, sout_ref, s_scr, *, n_chunks):
    c = pl.program_id(2)
    dk = s_scr.shape[-1]

    @pl.when(c == 0)
    def _():
        s_scr[...] = s0_ref[...]

    for h in range(s_scr.shape[0]):
        cols = slice(h * dk, (h + 1) * dk)
        q = q_ref[:, cols]
        k = k_ref[:, cols]
        v = v_ref[:, cols]
        s = s_scr[h]
        a = lax.dot_general(q, k, (((1,), (1,)), ((), ())), preferred_element_type=F32) * d_ref[h]
        o = jnp.dot(a.astype(BF16), v, preferred_element_type=F32)
        o = o + jnp.dot(q, s.astype(BF16), preferred_element_type=F32) * xi_ref[h]
        kz = (k.astype(F32) * zeta_ref[h]).astype(BF16)
        s_new = s * gc_ref[h] + lax.dot_general(kz, v, (((0,), (0,)), ((), ())),
                                                preferred_element_type=F32)
        s_scr[h] = s_new
        o_ref[:, cols] = (o * lax.rsqrt(jnp.mean(o * o, axis=-1, keepdims=True) + NORM_EPS)).astype(o_ref.dtype)

    @pl.when(c == n_chunks - 1)
    def _():
        sout_ref[...] = s_scr[...]


def retention(big, s0, tabs, heads, dk, *, batch, t_total, row0, chunk, hp=4):
    d_tab, xi_tab, zeta_tab, gc_tab = tabs
    assert heads % hp == 0
    nc = t_total // chunk
    rb0 = row0 // chunk
    hg = heads // hp
    rowblk = lambda g0: pl.BlockSpec((chunk, hp * dk), lambda b, h, c: (rb0 + b * nc + c, g0 + h))
    per_head = lambda a: pl.BlockSpec((hp,) + a.shape[1:], lambda b, h, c: (h, 0, 0))
    st_spec = pl.BlockSpec((None, hp, dk, dk), lambda b, h, c: (b, h, 0, 0))
    return pl.pallas_call(
        functools.partial(_retention_kernel, n_chunks=nc),
        out_shape=(jax.ShapeDtypeStruct((batch * t_total, heads * dk), BF16),
                   jax.ShapeDtypeStruct((batch, heads, dk, dk), F32)),
        grid=(batch, hg, nc),
        in_specs=[rowblk(0), rowblk(hg), rowblk(2 * hg), st_spec,
                  per_head(d_tab), per_head(xi_tab), per_head(zeta_tab), per_head(gc_tab)],
        out_specs=(pl.BlockSpec((chunk, hp * dk), lambda b, h, c: (b * nc + c, h)), st_spec),
        scratch_shapes=[pltpu.VMEM((hp, dk, dk), F32)],
        compiler_params=_params(("parallel", "parallel", "arbitrary"), VMEM_LIMIT_BYTES),
        name="retention",
    )(big, big, big, s0, d_tab, xi_tab, zeta_tab, gc_tab)


def retention_tables(length, heads, dk):
    lg = jnp.log1p(-jnp.exp2(-5.0 - jnp.arange(heads, dtype=F32)))
    idx = jnp.arange(length, dtype=F32)
    diff = idx[:, None] - idx[None, :]
    decay = jnp.where(diff >= 0, jnp.exp(jnp.maximum(diff, 0.0)[None] * lg[:, None, None]), 0.0)
    xi = jnp.exp((idx + 1.0)[None, :] * lg[:, None])
    zeta = jnp.exp((length - 1.0 - idx)[None, :] * lg[:, None])
    gc = jnp.exp(length * lg)
    bc = lambda a: jnp.broadcast_to(a[:, :, None], (heads, a.shape[1], dk))
    return decay, bc(xi), bc(zeta), jnp.broadcast_to(gc[:, None, None], (heads, 1, dk))


def _mixer_out_kernel(oap_ref, oas_ref, orp_ref, ors_ref, rg_ref, ga_ref, gb_ref, xp_ref, xs_ref, wo_ref,
                      g_ref, wrh_ref, wrl_ref, br_ref, h_ref, hn_ref, lg_ref, *, prompt_blocks):
    is_prompt = pl.program_id(0) < prompt_blocks
    tm = h_ref.shape[0]
    sub = tm // MIXER_SUBTILES
    for s in range(MIXER_SUBTILES):
        r = slice(s * sub, (s + 1) * sub)
        oa = jnp.where(is_prompt, oap_ref[r, :], oas_ref[r, :]).astype(F32)
        orn = jnp.where(is_prompt, orp_ref[r, :], ors_ref[r, :]).astype(F32)
        x = jnp.where(is_prompt, xp_ref[r, :], xs_ref[r, :])
        rg = rg_ref[r, :].astype(F32)
        yb = rg * _sigmoid(rg) * orn
        merged = _sigmoid(ga_ref[r, :].astype(F32)) * oa + _sigmoid(gb_ref[r, :].astype(F32)) * yb
        h = x + jnp.dot(merged.astype(BF16), wo_ref[...], preferred_element_type=F32)
        h_ref[r, :] = h
        hn = _rms(h, g_ref[...])
        hn_ref[r, :] = hn
        hn_hi = hn.astype(BF16)
        hn_lo = (hn - hn_hi.astype(F32)).astype(BF16)
        lg_ref[r, :] = (jnp.dot(hn_hi, wrh_ref[...], preferred_element_type=F32)
                        + jnp.dot(hn_lo, wrh_ref[...], preferred_element_type=F32)
                        + jnp.dot(hn_hi, wrl_ref[...], preferred_element_type=F32) + br_ref[...])


def mixer_out(oa_p, oa_s, or_p, or_s, big, gate_blk0, xp, xs, wo, g, wr, br, tm=256):
    d = xp.shape[1]
    m = xp.shape[0] + xs.shape[0]
    e = wr.shape[1]
    wr_hi = wr.astype(BF16)
    wr_lo = (wr - wr_hi.astype(F32)).astype(BF16)
    npb, prompt, sample = _group_specs(xp.shape[0], xs.shape[0], tm, d)
    row = lambda col: pl.BlockSpec((tm, d), lambda i: (i, col))
    full = lambda a: pl.BlockSpec(a.shape, lambda i: (0, 0))
    return pl.pallas_call(
        functools.partial(_mixer_out_kernel, prompt_blocks=npb),
        out_shape=(jax.ShapeDtypeStruct((m, d), F32),
                   jax.ShapeDtypeStruct((m, d), F32),
                   jax.ShapeDtypeStruct((m, e), F32)),
        grid=(m // tm,),
        in_specs=[prompt, sample, prompt, sample, row(gate_blk0), row(gate_blk0 + 1), row(gate_blk0 + 2),
                  prompt, sample, full(wo), full(g), full(wr_hi), full(wr_lo), full(br)],
        out_specs=(row(0), row(0), pl.BlockSpec((tm, e), lambda i: (i, 0))),
        compiler_params=_params(("parallel",), VMEM_LIMIT_BYTES),
        name="mixer_out",
    )(oa_p, oa_s, or_p, or_s, big, big, big, xp, xs, wo, g, wr_hi, wr_lo, br)


def _expert_kernel(tok_ref, ie_ref, ib_ref, inb_ref, nv_ref, hn_ref, wg_ref, wu_ref, bg_ref, bu_ref, wd_ref,
                   bd_ref, ye_ref, x32, y32, wgb, wub, wdb, sem_x, sem_out, *, rows, group, n_f, n_items,
                   n_blocks, balanced_nb):
    k = pl.program_id(0)
    f = pl.program_id(1)
    nb = inb_ref[k]
    blk0 = ib_ref[k]
    n_valid = nv_ref[0]
    valid = k < n_valid
    has_next = k + 1 < n_valid
    last_f = f == n_f - 1
    cur = k % 2
    rs = rows * group
    d = y32.shape[1]
    per_block = rs // (n_f * group)

    def gather_copy(slot, buf, row):
        return pltpu.make_async_copy(hn_ref.at[pl.ds(tok_ref[slot], 1), :],
                                     x32.at[buf, pl.ds(row, 1), :], sem_x.at[buf])

    def out_copy(b):
        return pltpu.make_async_copy(y32.at[pl.ds(b * rows, rows), :],
                                     ye_ref.at[pl.ds((blk0 + b) * rows, rows), :], sem_out)

    @pl.when((k == 0) & (f == 0))
    def _():
        def body(r, _):
            gather_copy(blk0 * rows + r, 0, r).start()
            return 0
        lax.fori_loop(0, rs, body, 0, unroll=8)

    @pl.when(valid & (f == 0))
    def _():
        def drain(r, _):
            gather_copy(0, cur, 0).wait()
            return 0
        lax.fori_loop(0, rs, drain, 0, unroll=8)

        def init(b, _):
            y32[pl.ds(pl.multiple_of(b * rows, rows), rows), :] = jnp.broadcast_to(bd_ref[...], (rows, d))
            return 0
        lax.fori_loop(0, nb, init, 0)

    def cast_weights():
        wg, wu, wd = wg_ref[...].astype(BF16), wu_ref[...].astype(BF16), wd_ref[...].astype(BF16)
        wgb[...] = wg
        wub[...] = wu
        wdb[...] = wd
        return wg, wu, wd

    @pl.when(valid & (nb < MOE_CHAIN))
    def _():
        cast_weights()

    def prefetch(b, next_slot0):
        row0 = (f * group + b) * per_block
        for r in range(per_block):
            gather_copy(next_slot0 + row0 + r, 1 - cur, row0 + r).start(priority=1)

    def compute_blocks(b0, n, next_slot0, cast=False):
        r = pl.ds(pl.multiple_of(b0 * rows, rows), n * rows)
        x = x32[cur, r, :].astype(BF16)
        wg, wu, wd = cast_weights() if cast else (wgb[...], wub[...], wdb[...])
        gate = jnp.minimum(jnp.dot(x, wg, preferred_element_type=F32) + bg_ref[...], SWIGLU_LIMIT)
        up = jnp.clip(jnp.dot(x, wu, preferred_element_type=F32) + bu_ref[...],
                      -SWIGLU_LIMIT, SWIGLU_LIMIT)
        h = (up + 1.0) * (gate * _sigmoid(gate * SWIGLU_ALPHA))
        y32[r, :] += jnp.dot(h.astype(BF16), wd, preferred_element_type=F32)
        if next_slot0 is not None:
            for i in range(n):
                prefetch(b0 + i, next_slot0)

        @pl.when(last_f)
        def _():
            for i in range(n):
                out_copy(b0 + i).start()

    def run(next_slot0):
        def general():
            @pl.when(nb >= MOE_CHAIN)
            def _():
                compute_blocks(0, MOE_CHAIN, next_slot0, cast=True)

            def chain(p, _):
                compute_blocks(MOE_CHAIN * p, MOE_CHAIN, next_slot0)
                return 0
            lax.fori_loop(1, nb // MOE_CHAIN, chain, 0)
            n = MOE_CHAIN // 2
            while n >= 1:
                @pl.when(nb & n != 0)
                def _(n=n):
                    compute_blocks(nb - (nb & (2 * n - 1)), n, next_slot0)
                n //= 2

        if balanced_nb is None:
            general()
        else:
            @pl.when(nb == balanced_nb)
            def _():
                compute_blocks(0, balanced_nb, next_slot0, cast=True)

            @pl.when(nb != balanced_nb)
            def _():
                general()

    @pl.when(has_next)
    def _():
        next_slot0 = ib_ref[k + 1] * rows
        run(next_slot0)

        def rest(b, _):
            prefetch(b, next_slot0)
            return 0
        lax.fori_loop(nb, group, rest, 0)

    @pl.when(valid & jnp.logical_not(has_next))
    def _():
        run(None)

    @pl.when(valid & last_f)
    def _():
        def body(b, _):
            out_copy(b).wait()
            return 0
        lax.fori_loop(0, nb, body, 0)

    @pl.when((k == n_items - 1) & last_f)
    def _():
        used = nv_ref[1]
        y32[0:rows, :] = jnp.zeros((rows, d), F32)

        def fill(b):
            return pltpu.make_async_copy(y32.at[pl.ds(0, rows), :], ye_ref.at[pl.ds(b * rows, rows), :], sem_out)

        def start(b, _):
            fill(b).start()
            return 0

        def wait(b, _):
            fill(b).wait()
            return 0
        lax.fori_loop(used, n_blocks, start, 0)
        lax.fori_loop(used, n_blocks, wait, 0)


def expert_mlp(slot_tok, items, hn, w_gate_up, b_gate_up, w_down, b_down, n_items, n_blocks):
    item_e, item_b0, item_nb, counts = items
    n_exp, d, two_f = w_gate_up.shape
    d_ff = two_f // 2
    tf = MOE_FTILE
    n_f = d_ff // tf
    rows, group = MOE_ROWS, MOE_GROUP
    rs = rows * group
    assert rs % (n_f * group) == 0
    balanced_nb = -(-(n_blocks - n_exp) // n_exp)
    if not MOE_CHAIN < balanced_nb <= group:
        balanced_nb = None

    def fi(k, f, nv):
        return jnp.where(k < nv[0], f, n_f - 1)

    in_specs = [
        pl.BlockSpec(memory_space=pl.ANY),
        pl.BlockSpec((None, d, tf), lambda k, f, tok, ie, ib, inb, nv: (ie[k], 0, fi(k, f, nv))),
        pl.BlockSpec((None, d, tf), lambda k, f, tok, ie, ib, inb, nv: (ie[k], 0, n_f + fi(k, f, nv))),
        pl.BlockSpec((None, 1, tf), lambda k, f, tok, ie, ib, inb, nv: (ie[k], 0, fi(k, f, nv))),
        pl.BlockSpec((None, 1, tf), lambda k, f, tok, ie, ib, inb, nv: (ie[k], 0, n_f + fi(k, f, nv))),
        pl.BlockSpec((None, tf, d), lambda k, f, tok, ie, ib, inb, nv: (ie[k], fi(k, f, nv), 0)),
        pl.BlockSpec((None, 1, d), lambda k, f, tok, ie, ib, inb, nv: (ie[k], 0, 0)),
    ]
    return pl.pallas_call(
        functools.partial(_expert_kernel, rows=rows, group=group, n_f=n_f, n_items=n_items, n_blocks=n_blocks,
                          balanced_nb=balanced_nb),
        out_shape=jax.ShapeDtypeStruct((n_blocks * rows, d), F32),
        grid_spec=pltpu.PrefetchScalarGridSpec(
            num_scalar_prefetch=5, grid=(n_items, n_f),
            in_specs=in_specs,
            out_specs=pl.BlockSpec(memory_space=pl.ANY),
            scratch_shapes=[pltpu.VMEM((2, rs, d), F32), pltpu.VMEM((rs, d), F32),
                            pltpu.VMEM((d, tf), BF16), pltpu.VMEM((d, tf), BF16), pltpu.VMEM((tf, d), BF16),
                            pltpu.SemaphoreType.DMA((2,)), pltpu.SemaphoreType.DMA]),
        compiler_params=_params(("arbitrary", "arbitrary"), VMEM_LIMIT_BYTES),
        name="moe_experts",
    )(slot_tok, item_e, item_b0, item_nb, counts, hn, w_gate_up, w_gate_up,
      b_gate_up.reshape(n_exp, 1, two_f), b_gate_up.reshape(n_exp, 1, two_f),
      w_down, b_down.reshape(n_exp, 1, d))


def _combine_kernel(pos_ref, ye_ref, w_ref, h_ref, g_ref, o_ref, buf, sems, *, tc, blk0, n_steps, final_norm):
    i = pl.program_id(0)
    n_rows = TOP_K * tc

    def copy(r, src_row, par):
        return pltpu.make_async_copy(ye_ref.at[pl.ds(src_row, 1), :], buf.at[par, pl.ds(r, 1), :],
                                     sems.at[par])

    def start(r, step, par, j):
        copy(r, pos_ref[(blk0 + step) * n_rows + r], par).start(priority=j % 2)

    def drain(par):
        def body(r, _):
            copy(0, 0, par).wait()
            return 0
        lax.fori_loop(0, n_rows, body, 0, unroll=8)

    @pl.when(i == 0)
    def _():
        def body(g, _):
            for j in range(8):
                start(g * 8 + j, 0, 0, j)
            return 0
        lax.fori_loop(0, n_rows // 8, body, 0)

    par = i % 2
    drain(par)

    nxt = jnp.minimum(i + 1, n_steps - 1)
    for r in range(n_rows):
        start(r, nxt, 1 - par, r)
    w = w_ref[...]
    y = h_ref[...]
    for k in range(TOP_K):
        y = y + w[:, k:k + 1] * buf[par, k * tc:(k + 1) * tc, :]
    o_ref[...] = _rms(y, g_ref[...]) if final_norm else y

    @pl.when(i == n_steps - 1)
    def _():
        drain(1 - par)


def combine(pos, ye, top_w, h, g, row0, rows, final_norm, tc=COMBINE_ROWS):
    d = h.shape[1]
    blk0 = row0 // tc
    n_steps = rows // tc
    return pl.pallas_call(
        functools.partial(_combine_kernel, tc=tc, blk0=blk0, n_steps=n_steps, final_norm=final_norm),
        out_shape=jax.ShapeDtypeStruct((rows, d), F32),
        grid_spec=pltpu.PrefetchScalarGridSpec(
            num_scalar_prefetch=1, grid=(n_steps,),
            in_specs=[pl.BlockSpec(memory_space=pl.ANY),
                      pl.BlockSpec((tc, TOP_K), lambda i, p: (blk0 + i, 0)),
                      pl.BlockSpec((tc, d), lambda i, p: (blk0 + i, 0)),
                      pl.BlockSpec((1, d), lambda i, p: (0, 0))],
            out_specs=pl.BlockSpec((tc, d), lambda i, p: (i, 0)),
            scratch_shapes=[pltpu.VMEM((2, TOP_K * tc, d), F32), pltpu.SemaphoreType.DMA((2,))]),
        compiler_params=_params(("arbitrary",), VMEM_LIMIT_BYTES),
        name="moe_combine",
    )(pos, ye, top_w, h, g)


def route(logits, tc):
    m, n_exp = logits.shape
    a = m * TOP_K
    rows, group = MOE_ROWS, MOE_GROUP
    top_logit, top_idx = lax.top_k(logits, TOP_K)
    top_w = jax.nn.softmax(top_logit, axis=-1)
    flat_e = top_idx.reshape(a)
    onehot = (flat_e[:, None] == jnp.arange(n_exp, dtype=jnp.int32)[None, :]).astype(jnp.int32)
    csum = jnp.cumsum(onehot, axis=0)
    counts = csum[-1]
    rank = jnp.take_along_axis(csum, flat_e[:, None], axis=1)[:, 0] - 1
    nb_e = (counts + rows - 1) // rows
    blk_start = jnp.cumsum(nb_e) - nb_e
    dest = blk_start[flat_e] * rows + rank
    n_blocks = a // rows + n_exp
    slot_tok = jnp.zeros(((n_blocks + group) * rows,), jnp.int32).at[dest].set(
        jnp.arange(a, dtype=jnp.int32) // TOP_K)
    ni_e = (nb_e + group - 1) // group
    item_end = jnp.cumsum(ni_e)
    n_valid = item_end[-1]
    n_items = (n_blocks + n_exp * (group - 1)) // group
    kk = jnp.arange(n_items, dtype=jnp.int32)
    item_e = jnp.minimum(jnp.sum(item_end[None, :] <= kk[:, None], axis=1), n_exp - 1).astype(jnp.int32)
    local = kk - (item_end - ni_e)[item_e]
    valid = kk < n_valid
    item_e = jnp.where(valid, item_e, item_e[jnp.maximum(n_valid - 1, 0)])
    item_b0 = jnp.where(valid, blk_start[item_e] + local * group, 0).astype(jnp.int32)
    item_nb = jnp.where(valid, jnp.clip(nb_e[item_e] - local * group, 0, group), 0).astype(jnp.int32)
    pos = dest.reshape(m // tc, tc, TOP_K).transpose(0, 2, 1).reshape(a)
    counts = jnp.stack([n_valid, jnp.sum(nb_e)]).astype(jnp.int32)
    items = (item_e, item_b0, item_nb, counts)
    return top_w, slot_tok, pos.astype(jnp.int32), items, n_items, n_blocks


def _rope_tables(pos, half):
    inv = jnp.power(ROPE_THETA, -jnp.arange(half, dtype=F32) / half)
    ang = pos.astype(F32)[:, None] * inv[None, :]
    return jnp.cos(ang), jnp.sin(ang)


def kernel(x_prompt, x_sample, cache_latent, cache_k_rope, state_retention, g_mix_norm, w_in, g_q_norm, w_uq, g_kv_norm, w_uk, w_uv, w_o, g_ffn_norm, w_router, b_router, w_gate_up, b_gate_up, w_down, b_down, g_final):
    bp, t, d = x_prompt.shape
    db, ts, _ = x_sample.shape
    depth = w_in.shape[0]
    past = cache_latent.shape[2]
    ql, ha, qk = w_uq.shape[1:]
    kl = w_uk.shape[1]
    rope = cache_k_rope.shape[-1]
    nope = qk - rope
    hr, dk = state_retention.shape[2:4]
    assert bp == 1 and nope == LANES and w_uv.shape[-1] == LANES and 2 * rope == LANES
    assert past % CHUNK == 0 and ts <= CHUNK
    mp, ms = bp * t, db * ts
    m = mp + ms
    tk_s = past + ts
    mla_scale = float(qk) ** -0.5 * LOG2_E

    pos = jnp.concatenate([jnp.arange(t, dtype=jnp.int32),
                           jnp.tile(past + jnp.arange(ts, dtype=jnp.int32), db)])
    cos_a, sin_a = _rope_tables(pos, rope // 2)
    z = lambda w: jnp.zeros((m, w), F32)
    rope_c = jnp.concatenate([cos_a, cos_a, z(LANES - rope)], axis=1)
    rope_sa = jnp.concatenate([z(rope // 2), sin_a, z(LANES - rope)], axis=1)
    rope_sb = jnp.concatenate([-sin_a, z(LANES - rope // 2)], axis=1)
    cos_r, sin_r = _rope_tables(pos, dk // 2)
    tabs_p = retention_tables(RET_CHUNK, hr, dk)
    tabs_s = retention_tables(ts, hr, dk)

    xp, xs = x_prompt.reshape(mp, d), x_sample.reshape(ms, d)
    lat_p, kr_p, st_p, lat_s, kr_s, st_s = [], [], [], [], [], []
    small = ql + kl + rope
    for l in range(depth):
        wa = jnp.pad(w_in[l][:, :small], ((0, 0), (0, LANES - rope))).astype(BF16)
        wb = w_in[l][:, small:].astype(BF16)
        wq = jnp.pad(w_uq[l], ((0, 0), (0, 0), (0, 2 * LANES - qk))).reshape(ql, ha * 2 * LANES).astype(BF16)
        wk = w_uk[l].reshape(kl, ha * nope).astype(BF16)
        wv = w_uv[l].reshape(kl, ha * LANES).astype(BF16)
        wo = w_o[l].astype(BF16)

        xn = norm_cast(xp, xs, g_mix_norm[l].reshape(1, d))
        cqn, ckv, ckv_b, kr, kr_pad = proj_small(xn, wa, g_q_norm[l].reshape(1, ql), g_kv_norm[l].reshape(1, kl),
                                                 rope_c, rope_sa, rope_sb, ql, kl, rope)
        big = proj_big(xn, wb, cos_r, sin_r, 2 * hr * dk, dk, float(dk) ** -0.5)
        qcat = q_proj(cqn, wq, rope_c, rope_sa, rope_sb, ha, mla_scale)

        k_p, vt_p = kv_expand(ckv_b, kr_pad, wk, wv.T, mp, ha)
        oa_p = attention_causal(qcat, k_p, vt_p, ha, t=mp, tq=512)
        c_all = jnp.concatenate([cache_latent[l], ckv[mp:].reshape(db, ts, kl)], axis=1)
        kr_all = jnp.concatenate([cache_k_rope[l], kr[mp:].reshape(db, ts, rope)], axis=1)
        kr_all = jnp.pad(kr_all, ((0, 0), (0, 0), (0, LANES - rope)))
        oa_s = attention_absorbed(qcat, c_all.reshape(db * tk_s, kl).astype(BF16),
                                  kr_all.reshape(db * tk_s, LANES).astype(BF16),
                                  w_uk[l].transpose(1, 2, 0).astype(BF16), w_uv[l].transpose(1, 0, 2).astype(BF16),
                                  ha, batch=db, tq=ts, q_row0=mp, tk=tk_s)

        or_p, s_p = retention(big, jnp.zeros((bp, hr, dk, dk), F32), tabs_p, hr, dk,
                              batch=bp, t_total=t, row0=0, chunk=RET_CHUNK)
        or_s, s_s = retention(big, state_retention[l].astype(F32), tabs_s, hr, dk,
                              batch=db, t_total=ts, row0=mp, chunk=ts)

        h1, hn, logits = mixer_out(oa_p, oa_s, or_p, or_s, big, 3, xp, xs, wo, g_ffn_norm[l].reshape(1, d),
                                   w_router[l], b_router[l].reshape(1, -1))

        top_w, slot_tok, pos_flat, items, n_items, n_blocks = route(logits, COMBINE_ROWS)
        ye = expert_mlp(slot_tok, items, hn, w_gate_up[l], b_gate_up[l], w_down[l], b_down[l], n_items, n_blocks)
        last = l == depth - 1
        g_out = g_final.reshape(1, d)
        y_p = combine(pos_flat, ye, top_w, h1, g_out, 0, mp, last)
        y_s = combine(pos_flat, ye, top_w, h1, g_out, mp, ms, last)
        xp, xs = y_p, y_s

        lat_p.append(ckv[:mp].reshape(bp, t, kl))
        kr_p.append(kr[:mp].reshape(bp, t, rope))
        st_p.append(s_p)
        lat_s.append(ckv[mp:].reshape(db, ts, kl))
        kr_s.append(kr[mp:].reshape(db, ts, rope))
        st_s.append(s_s)

    return (y_p.reshape(bp, t, d), y_s.reshape(db, ts, d),
            jnp.stack(lat_p), jnp.stack(kr_p), jnp.stack(st_p),
            jnp.stack(lat_s), jnp.stack(kr_s), jnp.stack(st_s))
```
